```python
import math
import jax
import jax.numpy as jnp
from jax import lax
import numpy as np

D_MODEL = 2048
BATCH = 16
SEQ = 256
DEPTH = 2
DEC_BATCH = 8
DEC_SEQ = 2048
PAST_LEN = 256

GRID_W = 64
HEAD_DIM = 128
ROPE_QUARTER = HEAD_DIM // 4
ROPE_THETA = 10000.0
NORM_EPS = 1e-6
Q_BLOCK = 128

A_HEADS = 8
A_DK = 128
A_DV = 128
A_CHUNK = 32
A_W = A_HEADS * A_DV
B_HEADS = 8
B_KV_HEADS = 2
B_GROUP = B_HEADS // B_KV_HEADS
B_W = B_HEADS * HEAD_DIM
C_HEADS = 16
C_KV_HEADS = 4
C_GROUP = C_HEADS // C_KV_HEADS
C_W = C_HEADS * HEAD_DIM
WINDOW = 128

L0_SIZES = (A_HEADS * A_DK, A_HEADS * A_DK, A_HEADS * A_DK, A_W, A_W,
            B_W, B_KV_HEADS * HEAD_DIM, B_KV_HEADS * HEAD_DIM, B_W)
L0_IN = 3 * A_HEADS * A_DK + 2 * A_W + 2 * B_W + 2 * B_KV_HEADS * HEAD_DIM
L1_SIZES = (C_W, C_KV_HEADS * HEAD_DIM, C_KV_HEADS * HEAD_DIM, C_W)
L1_IN = 2 * C_W + 2 * C_KV_HEADS * HEAD_DIM

kernel_name = "hybrid_hgrn2_axialgqa_swa_prefix_dit_step"


def _split(u, sizes):
    out, start = [], 0
    for s in sizes:
        out.append(u[..., start:start + s])
        start += s
    return out


def _rmsnorm(x, g):
    xf = x.astype(jnp.float32)
    y = xf * lax.rsqrt(jnp.mean(xf * xf, axis=-1, keepdims=True) + NORM_EPS)
    return (y * g.astype(jnp.float32)).astype(x.dtype)


def _adaln(cond, w_mod, b_mod):
    m = jax.nn.silu(cond) @ w_mod + b_mod
    return jnp.split(m, 3, axis=-1)


def _modulate(x, norm_g, shift, scale):
    return _rmsnorm(x, norm_g) * (1.0 + scale) + shift


def _axial_rope_tables(n_tokens):
    rows = n_tokens // GRID_W
    row = jnp.repeat(jnp.arange(rows), GRID_W).astype(jnp.float32)
    col = (jnp.arange(rows * GRID_W) % GRID_W).astype(jnp.float32)
    inv = ROPE_THETA ** (-jnp.arange(ROPE_QUARTER, dtype=jnp.float32) / ROPE_QUARTER)
    ar = row[:, None] * inv
    ac = col[:, None] * inv
    ang = jnp.concatenate([ar, ar, ac, ac], axis=-1)
    return jnp.cos(ang), jnp.sin(ang)


def _apply_rope(x, cos, sin):
    shp = (x.shape[1],) + (1,) * (x.ndim - 3) + (HEAD_DIM,)
    xr = x.reshape(x.shape[:-1] + (2, 2, ROPE_QUARTER))
    rot = jnp.stack([-xr[..., 1, :], xr[..., 0, :]], axis=-2).reshape(x.shape)
    return x * cos.reshape(shp).astype(x.dtype) + rot * sin.reshape(shp).astype(x.dtype)


def _attend(q, k, v, mask=None, sink=None):
    s = jnp.einsum('bqhgd,bkhd->bhgqk', q.astype(jnp.float32), k.astype(jnp.float32))
    s = s / math.sqrt(HEAD_DIM)
    if mask is not None:
        s = jnp.where(mask, s, -jnp.inf)
    if sink is not None:
        sk = jnp.broadcast_to(sink.astype(jnp.float32)[None, :, :, None, None], s.shape[:-1] + (1,))
        p = jax.nn.softmax(jnp.concatenate([s, sk], axis=-1), axis=-1)[..., :-1]
    else:
        p = jax.nn.softmax(s, axis=-1)
    return jnp.einsum('bhgqk,bkhd->bqhgd', p.astype(v.dtype), v)


def _blocked_dense(q, k, v, sink=None):
    bsz, lq = q.shape[:2]
    nb = lq // Q_BLOCK
    qb = jnp.moveaxis(q.reshape((bsz, nb, Q_BLOCK) + q.shape[2:]), 1, 0)
    out = lax.map(lambda qi: _attend(qi, k, v, None, sink), qb)
    return jnp.moveaxis(out, 0, 1).reshape(q.shape)


def _banded_window(q, k, v, k_ctx, v_ctx, sink):
    bsz, L = q.shape[:2]
    nb = L // Q_BLOCK
    pad = ((0, 0), (Q_BLOCK, Q_BLOCK), (0, 0), (0, 0))
    kp = jnp.pad(k, pad)
    vp = jnp.pad(v, pad)
    qb = jnp.moveaxis(q.reshape((bsz, nb, Q_BLOCK) + q.shape[2:]), 1, 0)
    r = jnp.arange(Q_BLOCK)
    cidx = jnp.arange(3 * Q_BLOCK)
    ctx_mask = jnp.ones((Q_BLOCK, k_ctx.shape[1]), dtype=bool)

    def blk(args):
        b, qi = args
        kw = lax.dynamic_slice_in_dim(kp, b * Q_BLOCK, 3 * Q_BLOCK, axis=1)
        vw = lax.dynamic_slice_in_dim(vp, b * Q_BLOCK, 3 * Q_BLOCK, axis=1)
        qpos = b * Q_BLOCK + r
        kpos = (b - 1) * Q_BLOCK + cidx
        band = (jnp.abs(qpos[:, None] - kpos[None, :]) <= WINDOW) & (kpos >= 0)[None, :] & (kpos < L)[None, :]
        mask = jnp.concatenate([band, ctx_mask], axis=1)
        return _attend(qi, jnp.concatenate([kw, k_ctx], axis=1),
                       jnp.concatenate([vw, v_ctx], axis=1), mask, sink)

    out = lax.map(blk, (jnp.arange(nb), qb))
    return jnp.moveaxis(out, 0, 1).reshape(q.shape)


def _hgrn2_chunk_scan(q, k, v, logf, s0):
    bsz, L, H, _ = q.shape
    dv = v.shape[-1]
    n = L // A_CHUNK

    def chunks(t):
        return t.reshape(bsz, n, A_CHUNK, H, t.shape[-1])

    q, k, v, logf = chunks(q), chunks(k), chunks(v), chunks(logf)
    b = jnp.cumsum(logf, axis=2)
    b_last = b[:, :, -1:]
    q_dec = q * jnp.exp(b)
    k_inv = k * jnp.exp(-b)
    k_end = k * jnp.exp(b_last - b)
    causal = jnp.tril(jnp.ones((A_CHUNK, A_CHUNK), dtype=bool))
    att = jnp.where(causal, jnp.einsum('bnthd,bnshd->bnhts', q_dec, k_inv), 0.0)
    o_intra = jnp.einsum('bnhts,bnshv->bnthv', att, v)

    def step(S, xs):
        qd, ke, vc, dec = xs
        o = jnp.einsum('bthd,bhdv->bthv', qd, S)
        S = dec[..., None] * S + jnp.einsum('bshd,bshv->bhdv', ke, vc)
        return S, o

    xs = (jnp.moveaxis(q_dec, 1, 0), jnp.moveaxis(k_end, 1, 0), jnp.moveaxis(v, 1, 0),
          jnp.moveaxis(jnp.exp(b_last[:, :, 0]), 1, 0))
    s_final, o_inter = lax.scan(step, s0, xs)
    o = o_intra + jnp.moveaxis(o_inter, 0, 1)
    return o.reshape(bsz, L, H, dv), s_final


def _layer0_mixer(h, w_in, w_out, lb, a_onorm, b_qnorm, b_knorm, ctx, rope):
    bsz, L, _ = h.shape
    aq, af_f, af_b, ai, ag, bq, bk, bv, bg = _split(h @ w_in, L0_SIZES)
    q = jax.nn.silu(aq.astype(jnp.float32)).reshape(bsz, L, A_HEADS, A_DK)
    v = ai.astype(jnp.float32).reshape(bsz, L, A_HEADS, A_DV)

    def gates(fr, lbd):
        f = lbd + (1.0 - lbd) * jax.nn.sigmoid(fr.astype(jnp.float32).reshape(bsz, L, A_HEADS, A_DK))
        return 1.0 - f, jnp.log(f)

    k_f, lf_f = gates(af_f, lb[0].reshape(A_HEADS, A_DK))
    k_b, lf_b = gates(af_b, lb[1].reshape(A_HEADS, A_DK))
    if ctx is None:
        s0f = jnp.zeros((bsz, A_HEADS, A_DK, A_DV), jnp.float32)
        s0b = s0f
    else:
        s0f = ctx[0][:, 0].astype(jnp.float32)
        s0b = ctx[0][:, 1].astype(jnp.float32)
    o_f, s_f = _hgrn2_chunk_scan(q, k_f, v, lf_f, s0f)
    flip = lambda t: jnp.flip(t, axis=1)
    o_bw, s_b = _hgrn2_chunk_scan(flip(q), flip(k_b), flip(v), flip(lf_b), s0b)
    o_a = _rmsnorm(o_f + flip(o_bw), a_onorm).reshape(bsz, L, A_W).astype(h.dtype) * jax.nn.silu(ag)
    qb = _rmsnorm(bq.reshape(bsz, L, B_KV_HEADS, B_GROUP, HEAD_DIM), b_qnorm)
    kb = _rmsnorm(bk.reshape(bsz, L, B_KV_HEADS, HEAD_DIM), b_knorm)
    vb = bv.reshape(bsz, L, B_KV_HEADS, HEAD_DIM)
    if ctx is None:
        o_b = _blocked_dense(qb, kb, vb)
    else:
        cos, sin = rope
        o_b = _blocked_dense(_apply_rope(qb, cos, sin),
                             jnp.concatenate([_apply_rope(kb, cos, sin), ctx[1]], axis=1),
                             jnp.concatenate([vb, ctx[2]], axis=1))
    o_b = o_b.reshape(bsz, L, B_W) * jax.nn.silu(bg)
    y = jnp.concatenate([o_a, o_b], axis=-1) @ w_out
    if ctx is None:
        return y, (jnp.stack([s_f, s_b], axis=1).astype(h.dtype), kb, vb)
    return y, None


def _layer1_mixer(h, w_in, w_out, qnorm, knorm, sink, ctx, rope):
    bsz, L, _ = h.shape
    cq, ck, cv, cg = _split(h @ w_in, L1_SIZES)
    q = _rmsnorm(cq.reshape(bsz, L, C_KV_HEADS, C_GROUP, HEAD_DIM), qnorm)
    k = _rmsnorm(ck.reshape(bsz, L, C_KV_HEADS, HEAD_DIM), knorm)
    v = cv.reshape(bsz, L, C_KV_HEADS, HEAD_DIM)
    sink_g = sink.reshape(C_KV_HEADS, C_GROUP)
    if ctx is None:
        o = _blocked_dense(q, k, v, sink_g)
    else:
        cos, sin = rope
        o = _banded_window(_apply_rope(q, cos, sin), _apply_rope(k, cos, sin), v, ctx[0], ctx[1], sink_g)
    y = (o.reshape(bsz, L, C_W) * jax.nn.silu(cg)) @ w_out
    if ctx is None:
        return y, (k, v)
    return y, None


def setup_inputs(seed: int = 0) -> dict:
    key = jax.random.key(seed)
    ks = jax.random.split(key, 32)
    D = D_MODEL

    def nrm(k, shape, s):
        return jax.random.normal(k, shape, jnp.float32) * s

    return {
        "x_prompt": nrm(ks[0], (BATCH, SEQ, D), 1.0),
        "x_sample": nrm(ks[1], (DEC_BATCH, DEC_SEQ, D), 1.0),
        "state_l0_hgrn": nrm(ks[2], (DEC_BATCH, 2, A_HEADS, A_DK, A_DV), 0.5),
        "cache_l0_k": nrm(ks[3], (DEC_BATCH, PAST_LEN, B_KV_HEADS, HEAD_DIM), 1.0),
        "cache_l0_v": nrm(ks[4], (DEC_BATCH, PAST_LEN, B_KV_HEADS, HEAD_DIM), 1.0),
        "cache_l1_k": nrm(ks[5], (DEC_BATCH, PAST_LEN, C_KV_HEADS, HEAD_DIM), 1.0),
        "cache_l1_v": nrm(ks[6], (DEC_BATCH, PAST_LEN, C_KV_HEADS, HEAD_DIM), 1.0),
        "c": nrm(ks[7], (DEC_BATCH, D), 1.0),
        "c_ctx": nrm(ks[8], (D,), 1.0),
        "lb_gamma": nrm(ks[9], (DEPTH + 1, 2, A_HEADS * A_DK), 0.1),
        "l0_norm": 1.0 + nrm(ks[10], (D,), 0.1),
        "l0_w_mod": nrm(ks[11], (D, 3 * D), D ** -0.5),
        "l0_b_mod": nrm(ks[12], (3 * D,), 0.02),
        "l0_w_in": nrm(ks[13], (D, L0_IN), D ** -0.5),
        "l0_w_out": nrm(ks[14], (A_W + B_W, D), (A_W + B_W) ** -0.5),
        "l0_a_onorm": 1.0 + nrm(ks[15], (A_DV,), 0.1),
        "l0_b_qnorm": 1.0 + nrm(ks[16], (HEAD_DIM,), 0.1),
        "l0_b_knorm": 1.0 + nrm(ks[17], (HEAD_DIM,), 0.1),
        "l1_norm": 1.0 + nrm(ks[18], (D,), 0.1),
        "l1_w_mod": nrm(ks[19], (D, 3 * D), D ** -0.5),
        "l1_b_mod": nrm(ks[20], (3 * D,), 0.02),
        "l1_w_in": nrm(ks[21], (D, L1_IN), D ** -0.5),
        "l1_w_out": nrm(ks[22], (C_W, D), C_W ** -0.5),
        "l1_c_qnorm": 1.0 + nrm(ks[23], (HEAD_DIM,), 0.1),
        "l1_c_knorm": 1.0 + nrm(ks[24], (HEAD_DIM,), 0.1),
        "l1_c_sink": nrm(ks[25], (C_HEADS,), 0.5),
    }


def reference(x_prompt, x_sample, state_l0_hgrn, cache_l0_k, cache_l0_v, cache_l1_k, cache_l1_v,
              c, c_ctx, lb_gamma,
              l0_norm, l0_w_mod, l0_b_mod, l0_w_in, l0_w_out, l0_a_onorm, l0_b_qnorm, l0_b_knorm,
              l1_norm, l1_w_mod, l1_b_mod, l1_w_in, l1_w_out, l1_c_qnorm, l1_c_knorm, l1_c_sink):
    lb_all = jnp.cumsum(jax.nn.softmax(lb_gamma.astype(jnp.float32), axis=0), axis=0)
    rope = _axial_rope_tables(x_sample.shape[1])
    mod_params = [(l0_norm, l0_w_mod, l0_b_mod), (l1_norm, l1_w_mod, l1_b_mod)]
    y_prompt, y_sample = x_prompt, x_sample
    for layer in range(DEPTH):
        norm_g, w_mod, b_mod = mod_params[layer]
        sh_p, sc_p, gt_p = _adaln(c_ctx, w_mod, b_mod)
        sh_s, sc_s, gt_s = _adaln(c[:, None, :], w_mod, b_mod)
        h_p = _modulate(y_prompt, norm_g, sh_p, sc_p)
        h_s = _modulate(y_sample, norm_g, sh_s, sc_s)
        if layer % 2 == 0:
            lb = lb_all[layer]
            out_p, (new_state_l0_hgrn, new_cache_l0_k, new_cache_l0_v) = _layer0_mixer(
                h_p, l0_w_in, l0_w_out, lb, l0_a_onorm, l0_b_qnorm, l0_b_knorm, None, None)
            out_s, _ = _layer0_mixer(
                h_s, l0_w_in, l0_w_out, lb, l0_a_onorm, l0_b_qnorm, l0_b_knorm,
                (state_l0_hgrn, cache_l0_k, cache_l0_v), rope)
        else:
            out_p, (new_cache_l1_k, new_cache_l1_v) = _layer1_mixer(
                h_p, l1_w_in, l1_w_out, l1_c_qnorm, l1_c_knorm, l1_c_sink, None, None)
            out_s, _ = _layer1_mixer(
                h_s, l1_w_in, l1_w_out, l1_c_qnorm, l1_c_knorm, l1_c_sink,
                (cache_l1_k, cache_l1_v), rope)
        y_prompt = y_prompt + gt_p * out_p
        y_sample = y_sample + gt_s * out_s
    return (y_prompt, y_sample, new_state_l0_hgrn, new_cache_l0_k, new_cache_l0_v, new_cache_l1_k, new_cache_l1_v)
```

```python
import functools
import math

import jax
import jax.numpy as jnp
from jax import lax
from jax.experimental import pallas as pl
from jax.experimental.pallas import tpu as pltpu

F32 = jnp.float32
BF16 = jnp.bfloat16

HEAD_DIM = 128
GRID_W = 64
ROPE_QUARTER = HEAD_DIM // 4
ROPE_THETA = 10000.0
NORM_EPS = 1e-6
A_HEADS = 8
A_CHUNK = 32
B_KV_HEADS = 2
C_KV_HEADS = 4
GROUP = 4
WINDOW = 128
NEG = -1e30
VMEM_LIMIT = 56 * 1024 * 1024

_NT = (((1,), (1,)), ((), ()))
_TN = (((0,), (0,)), ((), ()))


def _params(*sem):
    return pltpu.CompilerParams(dimension_semantics=sem, vmem_limit_bytes=VMEM_LIMIT)


def _silu(x):
    return x * jax.nn.sigmoid(x)


def _rms(x, g):
    ms = jnp.mean(x * x, axis=-1, keepdims=True)
    return x * lax.rsqrt(ms + NORM_EPS) * g


def _adaln_kernel(c_ref, w_ref, b_ref, o_ref):
    a = _silu(c_ref[...]).astype(BF16)
    o_ref[...] = jnp.dot(a, w_ref[...].astype(BF16), preferred_element_type=F32) + b_ref[...]


def _adaln(cond, w_mod, b_mod):
    r, d = cond.shape
    n = w_mod.shape[1]
    tn = _tile(n, 512)
    return pl.pallas_call(
        _adaln_kernel,
        grid=(n // tn,),
        in_specs=[pl.BlockSpec((r, d), lambda j: (0, 0)),
                  pl.BlockSpec((d, tn), lambda j: (0, j)),
                  pl.BlockSpec((1, tn), lambda j: (0, j))],
        out_specs=pl.BlockSpec((r, tn), lambda j: (0, j)),
        out_shape=jax.ShapeDtypeStruct((r, n), F32),
        compiler_params=_params("parallel"),
        name="adaln",
    )(cond, w_mod, b_mod.reshape(1, n))


def _inproj_kernel(x_ref, g_ref, sh_ref, sc_ref, w_ref, o_ref, h_ref, *, rc):
    @pl.when(pl.program_id(1) == 0)
    def _():
        g = g_ref[...]
        mul = 1.0 + sc_ref[...]
        sh = sh_ref[...]

        def body(r, carry):
            rows = pl.ds(pl.multiple_of(r * rc, rc), rc)
            h_ref[rows, :] = (_rms(x_ref[rows, :], g) * mul + sh).astype(BF16)
            return carry

        lax.fori_loop(0, x_ref.shape[0] // rc, body, 0)

    o_ref[...] = jnp.dot(h_ref[...], w_ref[...], preferred_element_type=F32).astype(o_ref.dtype)


def _inproj(x2d, norm_g, shift, scale, w, rows_per_mod, tm, tn):
    m, d = x2d.shape
    n = w.shape[1]
    mod_spec = pl.BlockSpec((None, 1, d), lambda i, j: ((i * tm) // rows_per_mod, 0, 0))
    return pl.pallas_call(
        functools.partial(_inproj_kernel, rc=32),
        grid=(m // tm, n // tn),
        in_specs=[pl.BlockSpec((tm, d), lambda i, j: (i, 0)),
                  pl.BlockSpec((1, d), lambda i, j: (0, 0)),
                  mod_spec, mod_spec,
                  pl.BlockSpec((d, tn), lambda i, j: (0, j))],
        out_specs=pl.BlockSpec((tm, tn), lambda i, j: (i, j)),
        out_shape=jax.ShapeDtypeStruct((m, n), BF16),
        scratch_shapes=[pltpu.VMEM((tm, d), BF16)],
        compiler_params=_params("parallel", "arbitrary"),
        name="inproj",
    )(x2d, norm_g.reshape(1, d), shift, scale, w)


def _outproj_kernel(*refs, widths):
    o_refs = refs[:len(widths)]
    w_ref, x_ref, gt_ref, y_ref = refs[len(widths):]
    acc = None
    start = 0
    for o_ref, wd in zip(o_refs, widths):
        part = jnp.dot(o_ref[...], w_ref[start:start + wd, :], preferred_element_type=F32)
        acc = part if acc is None else acc + part
        start += wd
    y_ref[...] = x_ref[...] + gt_ref[...] * acc


def _outproj(os, w, x2d, gate, rows_per_mod, tm):
    m, d = x2d.shape
    widths = tuple(o.shape[1] for o in os)
    k = w.shape[0]
    return pl.pallas_call(
        functools.partial(_outproj_kernel, widths=widths),
        grid=(m // tm,),
        in_specs=[pl.BlockSpec((tm, wd), lambda i: (i, 0)) for wd in widths] + [
            pl.BlockSpec((k, d), lambda i: (0, 0)),
            pl.BlockSpec((tm, d), lambda i: (i, 0)),
            pl.BlockSpec((None, 1, d), lambda i: ((i * tm) // rows_per_mod, 0, 0))],
        out_specs=pl.BlockSpec((tm, d), lambda i: (i, 0)),
        out_shape=jax.ShapeDtypeStruct((m, d), F32),
        compiler_params=_params("parallel"),
        name="outproj",
    )(*os, w, x2d, gate)


def _cumsum_rows(x, reverse):
    n = x.shape[0]
    row = lax.broadcasted_iota(jnp.int32, x.shape, 0)
    s = 1
    while s < n:
        if reverse:
            x = x + jnp.where(row < n - s, pltpu.roll(x, n - s, axis=0), 0.0)
        else:
            x = x + jnp.where(row >= s, pltpu.roll(x, s, axis=0), 0.0)
        s *= 2
    return x


def _hgrn_kernel(*refs, seq, hb, has_s0, emit_state):
    q_ref, ff_ref, fb_ref, v_ref, g_ref, lb_ref, on_ref = refs[:7]
    pos = 7
    s0_ref = None
    if has_s0:
        s0_ref = refs[pos]
        pos += 1
    o_ref = refs[pos]
    pos += 1
    so_ref = None
    if emit_state:
        so_ref = refs[pos]
        pos += 1
    oacc_ref, st_ref = refs[pos:]

    nc = seq // A_CHUNK
    ri = lax.broadcasted_iota(jnp.int32, (A_CHUNK, A_CHUNK), 0)
    ci = lax.broadcasted_iota(jnp.int32, (A_CHUNK, A_CHUNK), 1)
    masks = (ri >= ci, ci >= ri)

    for j in range(hb):
        for d in range(2):
            if has_s0:
                st_ref[j, d] = s0_ref[d, j].T
            else:
                st_ref[j, d] = jnp.zeros((HEAD_DIM, HEAD_DIM), F32)

    def chunk(j, d, n, accumulate):
        cols = slice(j * HEAD_DIM, (j + 1) * HEAD_DIM)
        rows = pl.ds(pl.multiple_of(n * A_CHUNK, A_CHUNK), A_CHUNK)
        q = _silu(q_ref[rows, cols].astype(F32))
        v = v_ref[rows, cols]
        fr = (fb_ref if d else ff_ref)[rows, cols].astype(F32)
        lb = lb_ref[d:d + 1, cols]
        f = lb + (1.0 - lb) * jax.nn.sigmoid(fr)
        k = 1.0 - f
        b = _cumsum_rows(jnp.log(f), reverse=bool(d))
        btot = b[0:1, :] if d else b[A_CHUNK - 1:A_CHUNK, :]
        qd = (q * jnp.exp(b)).astype(BF16)
        ki = (k * jnp.exp(-b)).astype(BF16)
        ke = (k * jnp.exp(btot - b)).astype(BF16)
        att = lax.dot_general(qd, ki, _NT, preferred_element_type=F32)
        att = jnp.where(masks[d], att, 0.0).astype(BF16)
        st = st_ref[j, d]
        o = (jnp.dot(att, v, preferred_element_type=F32)
             + lax.dot_general(qd, st.astype(BF16), _NT, preferred_element_type=F32))
        st_ref[j, d] = st * jnp.exp(btot) + lax.dot_general(v, ke, _TN, preferred_element_type=F32)
        if accumulate:
            oacc_ref[rows, cols] += o
        else:
            oacc_ref[rows, cols] = o

    def body(n, carry, accumulate):
        for j in range(hb):
            chunk(j, 0, n, accumulate)
            chunk(j, 1, nc - 1 - n, accumulate)
        return carry

    lax.fori_loop(0, nc // 2, functools.partial(body, accumulate=False), 0)
    lax.fori_loop(nc // 2, nc, functools.partial(body, accumulate=True), 0)

    rc = 128
    on = on_ref[...]

    def epilogue(r, carry):
        rows = pl.ds(pl.multiple_of(r * rc, rc), rc)
        for j in range(hb):
            cols = slice(j * HEAD_DIM, (j + 1) * HEAD_DIM)
            y = _rms(oacc_ref[rows, cols], on)
            o_ref[rows, cols] = (y * _silu(g_ref[rows, cols].astype(F32))).astype(o_ref.dtype)
        return carry

    lax.fori_loop(0, seq // rc, epilogue, 0)

    if emit_state:
        for j in range(hb):
            for d in range(2):
                so_ref[d, j] = st_ref[j, d].T


def _hgrn(u3, lb, onorm, s0, hb, emit_state):
    bsz, seq, _ = u3.shape
    bw = hb * HEAD_DIM
    nh = A_HEADS // hb

    def seg(s):
        return pl.BlockSpec((None, seq, bw), lambda b, h: (b, 0, s * nh + h))

    in_specs = [seg(0), seg(1), seg(2), seg(3), seg(4),
                pl.BlockSpec((2, bw), lambda b, h: (0, h)),
                pl.BlockSpec((1, HEAD_DIM), lambda b, h: (0, 0))]
    args = [u3, u3, u3, u3, u3, lb, onorm.reshape(1, HEAD_DIM)]
    st_spec = pl.BlockSpec((None, 2, hb, HEAD_DIM, HEAD_DIM), lambda b, h: (b, 0, h, 0, 0))
    if s0 is not None:
        in_specs.append(st_spec)
        args.append(s0)
    out_specs = [pl.BlockSpec((None, seq, bw), lambda b, h: (b, 0, h))]
    out_shape = [jax.ShapeDtypeStruct((bsz, seq, A_HEADS * HEAD_DIM), BF16)]
    if emit_state:
        out_specs.append(st_spec)
        out_shape.append(jax.ShapeDtypeStruct((bsz, 2, A_HEADS, HEAD_DIM, HEAD_DIM), F32))
    return pl.pallas_call(
        functools.partial(_hgrn_kernel, seq=seq, hb=hb, has_s0=s0 is not None, emit_state=emit_state),
        grid=(bsz, nh),
        in_specs=in_specs,
        out_specs=out_specs,
        out_shape=out_shape,
        scratch_shapes=[pltpu.VMEM((seq, bw), F32),
                        pltpu.VMEM((hb, 2, HEAD_DIM, HEAD_DIM), F32)],
        compiler_params=_params("parallel", "parallel"),
        name="hgrn",
    )(*args)


def _rope(x, cos, sin_a, sin_b):
    return (x * cos + pltpu.roll(x, HEAD_DIM - ROPE_QUARTER, axis=1) * sin_a
            + pltpu.roll(x, ROPE_QUARTER, axis=1) * sin_b)


def _attn_kernel(*refs, seq, tq, kv_heads, ctx_len, rope, window, sink, emit_kv, tk):
    refs = list(refs)
    q_ref, gate_ref, k_ref, v_ref = refs[:4]
    pos = 4
    kc_ref = vc_ref = cos_ref = sa_ref = sb_ref = sink_ref = ko_ref = vo_ref = None
    if ctx_len:
        kc_ref, vc_ref = refs[pos:pos + 2]
        pos += 2
    if rope:
        cos_ref, sa_ref, sb_ref = refs[pos:pos + 3]
        pos += 3
    qn_ref, kn_ref = refs[pos:pos + 2]
    pos += 2
    if sink:
        sink_ref = refs[pos]
        pos += 1
    o_ref = refs[pos]
    pos += 1
    if emit_kv:
        ko_ref, vo_ref = refs[pos:pos + 2]
        pos += 2
    ks_ref, vs_ref, qs_ref = refs[pos:]

    kvh = pl.program_id(1)
    qi = pl.program_id(2)
    pad = WINDOW if window else 0
    ctx_off = seq + 2 * pad
    scale = 1.0 / math.sqrt(HEAD_DIM)

    @pl.when(qi == 0)
    def _():
        if window:
            zeros = jnp.zeros((pad, HEAD_DIM), BF16)
            for ref in (ks_ref, vs_ref):
                ref[0:pad, :] = zeros
                ref[pad + seq:ctx_off, :] = zeros
        rc = 256
        kn = kn_ref[...]

        def body(r, carry):
            rows = pl.ds(pl.multiple_of(r * rc, rc), rc)
            dst = pl.ds(pl.multiple_of(pad + r * rc, HEAD_DIM), rc)
            k = _rms(k_ref[rows, :].astype(F32), kn)
            if emit_kv:
                ko_ref[rows, :] = k
                vo_ref[rows, :] = v_ref[rows, :].astype(F32)
            if rope:
                k = _rope(k, cos_ref[rows, :], sa_ref[rows, :], sb_ref[rows, :])
            ks_ref[dst, :] = k.astype(BF16)
            vs_ref[dst, :] = v_ref[rows, :]
            return carry

        lax.fori_loop(0, seq // rc, body, 0)
        if ctx_len:
            ks_ref[ctx_off:ctx_off + ctx_len, :] = kc_ref[...].astype(BF16)
            vs_ref[ctx_off:ctx_off + ctx_len, :] = vc_ref[...].astype(BF16)

    qn = qn_ref[...]
    qrows = pl.ds(pl.multiple_of(qi * tq, tq), tq)
    for h in range(GROUP):
        x = _rms(q_ref[:, h * HEAD_DIM:(h + 1) * HEAD_DIM].astype(F32), qn)
        if rope:
            x = _rope(x, cos_ref[qrows, :], sa_ref[qrows, :], sb_ref[qrows, :])
        qs_ref[h * tq:(h + 1) * tq, :] = (x * scale).astype(BF16)
    q = qs_ref[...]
    nrow = GROUP * tq

    if sink:
        m = jnp.concatenate([jnp.full((tq, 1), sink_ref[kvh * GROUP + h], F32) for h in range(GROUP)], axis=0)
        l = jnp.ones((nrow, 1), F32)
    else:
        m = jnp.full((nrow, 1), NEG, F32)
        l = jnp.zeros((nrow, 1), F32)
    acc = jnp.zeros((nrow, HEAD_DIM), F32)

    def update(state, k, v, mask=None):
        m, l, acc = state
        s = lax.dot_general(q, k, _NT, preferred_element_type=F32)
        if mask is not None:
            s = jnp.where(mask, s, NEG)
        m_new = jnp.maximum(m, jnp.max(s, axis=-1, keepdims=True))
        alpha = jnp.exp(m - m_new)
        p = jnp.exp(s - m_new)
        l = alpha * l + jnp.sum(p, axis=-1, keepdims=True)
        acc = alpha * acc + jnp.dot(p.astype(BF16), v, preferred_element_type=F32)
        return m_new, l, acc

    state = (m, l, acc)
    if window:
        span = 3 * WINDOW
        start = pl.multiple_of(qi * tq, HEAD_DIM)
        r = lax.broadcasted_iota(jnp.int32, (nrow, span), 0) & (tq - 1)
        c = lax.broadcasted_iota(jnp.int32, (nrow, span), 1)
        kpos = c + (qi * tq - WINDOW)
        mask = (c >= r) & (c <= r + 2 * WINDOW) & (kpos >= 0) & (kpos < seq)
        state = update(state, ks_ref[pl.ds(start, span), :], vs_ref[pl.ds(start, span), :], mask)
    else:
        for c0 in range(0, seq, tk):
            state = update(state, ks_ref[c0:c0 + tk, :], vs_ref[c0:c0 + tk, :])
    if ctx_len:
        state = update(state, ks_ref[ctx_off:ctx_off + ctx_len, :], vs_ref[ctx_off:ctx_off + ctx_len, :])
    m, l, acc = state
    o = acc / l
    for h in range(GROUP):
        cols = slice(h * HEAD_DIM, (h + 1) * HEAD_DIM)
        o_ref[:, cols] = (o[h * tq:(h + 1) * tq, :] * _silu(gate_ref[:, cols].astype(F32))).astype(o_ref.dtype)


def _attn(u3, q_off, k_off, v_off, g_off, kv_heads, qnorm, knorm, *, tq, ctx=None, rope=None,
          window=False, sink=None, emit_kv=False):
    bsz, seq, _ = u3.shape
    gw = GROUP * HEAD_DIM
    heads = kv_heads * GROUP
    ctx_len = 0 if ctx is None else ctx[0].shape[1]
    pad = WINDOW if window else 0
    assert not window or tq == WINDOW
    tk = min(seq, 512)

    def wide(off):
        return pl.BlockSpec((None, tq, gw), lambda b, h, i: (b, i, off // gw + h))

    def narrow(off, rows):
        return pl.BlockSpec((None, rows, HEAD_DIM), lambda b, h, i: (b, 0, off // HEAD_DIM + h))

    def whole(shape):
        return pl.BlockSpec(shape, lambda b, h, i: (0,) * len(shape))

    in_specs = [wide(q_off), wide(g_off), narrow(k_off, seq), narrow(v_off, seq)]
    args = [u3, u3, u3, u3]
    if ctx is not None:
        for cache in ctx:
            in_specs.append(narrow(0, ctx_len))
            args.append(cache.reshape(bsz, ctx_len, kv_heads * HEAD_DIM))
    if rope is not None:
        for t in rope:
            in_specs.append(whole((seq, HEAD_DIM)))
            args.append(t)
    in_specs += [whole((1, HEAD_DIM)), whole((1, HEAD_DIM))]
    args += [qnorm.reshape(1, HEAD_DIM), knorm.reshape(1, HEAD_DIM)]
    if sink is not None:
        in_specs.append(pl.BlockSpec(memory_space=pltpu.SMEM))
        args.append(sink)
    out_specs = [pl.BlockSpec((None, tq, gw), lambda b, h, i: (b, i, h))]
    out_shape = [jax.ShapeDtypeStruct((bsz, seq, heads * HEAD_DIM), BF16)]
    if emit_kv:
        for _ in range(2):
            out_specs.append(narrow(0, seq))
            out_shape.append(jax.ShapeDtypeStruct((bsz, seq, kv_heads * HEAD_DIM), F32))
    rows = seq + 2 * pad + ctx_len
    kern = functools.partial(_attn_kernel, seq=seq, tq=tq, kv_heads=kv_heads, ctx_len=ctx_len,
                             rope=rope is not None, window=window, sink=sink is not None,
                             emit_kv=emit_kv, tk=tk)
    return pl.pallas_call(
        kern,
        grid=(bsz, kv_heads, seq // tq),
        in_specs=in_specs,
        out_specs=out_specs,
        out_shape=out_shape,
        scratch_shapes=[pltpu.VMEM((rows, HEAD_DIM), BF16),
                        pltpu.VMEM((rows, HEAD_DIM), BF16),
                        pltpu.VMEM((GROUP * tq, HEAD_DIM), BF16)],
        compiler_params=_params("parallel", "parallel", "arbitrary"),
        name="attn",
    )(*args)


def _rope_tables(n_tokens):
    rows = n_tokens // GRID_W
    row = jnp.repeat(jnp.arange(rows), GRID_W).astype(F32)
    col = (jnp.arange(rows * GRID_W) % GRID_W).astype(F32)
    inv = ROPE_THETA ** (-jnp.arange(ROPE_QUARTER, dtype=F32) / ROPE_QUARTER)
    ar = row[:, None] * inv
    ac = col[:, None] * inv
    ang = jnp.concatenate([ar, ar, ac, ac], axis=-1)
    cos, sin = jnp.cos(ang), jnp.sin(ang)
    first = (jnp.arange(HEAD_DIM) % (2 * ROPE_QUARTER)) < ROPE_QUARTER
    return cos, jnp.where(first, -sin, 0.0), jnp.where(first, 0.0, sin)


def _tile(m, pref):
    t = pref
    while m % t:
        t //= 2
    return t


def kernel(x_prompt, x_sample, state_l0_hgrn, cache_l0_k, cache_l0_v, cache_l1_k, cache_l1_v, c, c_ctx, lb_gamma, l0_norm, l0_w_mod, l0_b_mod, l0_w_in, l0_w_out, l0_a_onorm, l0_b_qnorm, l0_b_knorm, l1_norm, l1_w_mod, l1_b_mod, l1_w_in, l1_w_out, l1_c_qnorm, l1_c_knorm, l1_c_sink):
    pb, pl_, d = x_prompt.shape
    sb, sl, _ = x_sample.shape
    aw = A_HEADS * HEAD_DIM
    bkv = B_KV_HEADS * HEAD_DIM
    bw = B_KV_HEADS * GROUP * HEAD_DIM
    ckv = C_KV_HEADS * HEAD_DIM
    cw = C_KV_HEADS * GROUP * HEAD_DIM

    lb = jnp.cumsum(jax.nn.softmax(lb_gamma.astype(F32), axis=0), axis=0)[0]
    rope = _rope_tables(sl)

    nrow = -(-(sb + 1) // 8) * 8
    cond = jnp.zeros((nrow, d), F32).at[:sb].set(c).at[sb].set(c_ctx)
    xs = (x_prompt.reshape(pb * pl_, d), x_sample.reshape(sb * sl, d))
    tms = (_tile(pb * pl_, 1024), _tile(sl, 1024))
    rows_per_mod = (pb * pl_, sl)

    def mods(w_mod, b_mod):
        m = _adaln(cond, w_mod, b_mod)
        parts = [m[:, i * d:(i + 1) * d] for i in range(3)]
        return ([p[sb:sb + 1, None, :] for p in parts], [p[:sb, None, :] for p in parts])

    def tn_for(n):
        for t in (1280, 1024, 768, 512, 256, 128):
            if n % t == 0:
                return t
        return n

    mod_p, mod_s = mods(l0_w_mod, l0_b_mod)
    w_in = l0_w_in.astype(BF16)
    w_out = l0_w_out.astype(BF16)
    tn = tn_for(w_in.shape[1])
    u_p = _inproj(xs[0], l0_norm, mod_p[0], mod_p[1], w_in, rows_per_mod[0], tms[0], tn).reshape(pb, pl_, -1)
    u_s = _inproj(xs[1], l0_norm, mod_s[0], mod_s[1], w_in, rows_per_mod[1], tms[1], tn).reshape(sb, sl, -1)

    oa_p, new_state = _hgrn(u_p, lb, l0_a_onorm, None, hb=A_HEADS, emit_state=True)
    (oa_s,) = _hgrn(u_s, lb, l0_a_onorm, state_l0_hgrn, hb=2, emit_state=False)

    q_off = 5 * aw
    k_off = q_off + bw
    v_off = k_off + bkv
    g_off = v_off + bkv
    ob_p, k0, v0 = _attn(u_p, q_off, k_off, v_off, g_off, B_KV_HEADS, l0_b_qnorm, l0_b_knorm,
                         tq=_tile(pl_, 256), emit_kv=True)
    (ob_s,) = _attn(u_s, q_off, k_off, v_off, g_off, B_KV_HEADS, l0_b_qnorm, l0_b_knorm,
                    tq=_tile(sl, 256), ctx=(cache_l0_k, cache_l0_v), rope=rope)

    otm = (_tile(pb * pl_, 512), _tile(sl, 512))
    y_p = _outproj([oa_p.reshape(pb * pl_, aw), ob_p.reshape(pb * pl_, bw)], w_out, xs[0], mod_p[2],
                   rows_per_mod[0], otm[0])
    y_s = _outproj([oa_s.reshape(sb * sl, aw), ob_s.reshape(sb * sl, bw)], w_out, xs[1], mod_s[2],
                   rows_per_mod[1], otm[1])

    mod_p, mod_s = mods(l1_w_mod, l1_b_mod)
    w_in = l1_w_in.astype(BF16)
    w_out = l1_w_out.astype(BF16)
    tn = tn_for(w_in.shape[1])
    u_p = _inproj(y_p, l1_norm, mod_p[0], mod_p[1], w_in, rows_per_mod[0], tms[0], tn).reshape(pb, pl_, -1)
    u_s = _inproj(y_s, l1_norm, mod_s[0], mod_s[1], w_in, rows_per_mod[1], tms[1], tn).reshape(sb, sl, -1)

    k_off = cw
    v_off = k_off + ckv
    g_off = v_off + ckv
    oc_p, k1, v1 = _attn(u_p, 0, k_off, v_off, g_off, C_KV_HEADS, l1_c_qnorm, l1_c_knorm,
                         tq=_tile(pl_, 256), sink=l1_c_sink, emit_kv=True)
    (oc_s,) = _attn(u_s, 0, k_off, v_off, g_off, C_KV_HEADS, l1_c_qnorm, l1_c_knorm,
                    tq=WINDOW, ctx=(cache_l1_k, cache_l1_v), rope=rope, window=True, sink=l1_c_sink)

    z_p = _outproj([oc_p.reshape(pb * pl_, cw)], w_out, y_p, mod_p[2], rows_per_mod[0], otm[0])
    z_s = _outproj([oc_s.reshape(sb * sl, cw)], w_out, y_s, mod_s[2], rows_per_mod[1], otm[1])

    return (z_p.reshape(pb, pl_, d), z_s.reshape(sb, sl, d), new_state,
            k0.reshape(pb, pl_, B_KV_HEADS, HEAD_DIM), v0.reshape(pb, pl_, B_KV_HEADS, HEAD_DIM),
            k1.reshape(pb, pl_, C_KV_HEADS, HEAD_DIM), v1.reshape(pb, pl_, C_KV_HEADS, HEAD_DIM))
```

```python
import functools
import math

import jax
import jax.numpy as jnp
from jax import lax
from jax.experimental import pallas as pl
from jax.experimental.pallas import tpu as pltpu

F32 = jnp.float32
BF16 = jnp.bfloat16

HEAD_DIM = 128
GRID_W = 64
ROPE_QUARTER = HEAD_DIM // 4
ROPE_THETA = 10000.0
NORM_EPS = 1e-6
A_HEADS = 8
A_CHUNK = 32
B_KV_HEADS = 2
C_KV_HEADS = 4
GROUP = 4
WINDOW = 128
NEG = -1e30
LOG2E = math.log2(math.e)
VMEM_LIMIT = 56 * 1024 * 1024

_NT = (((1,), (1,)), ((), ()))
_TN = (((0,), (0,)), ((), ()))


def _params(*sem):
    return pltpu.CompilerParams(dimension_semantics=sem, vmem_limit_bytes=VMEM_LIMIT)


def _silu(x):
    return x * jax.nn.sigmoid(x)


def _rms(x, g):
    ms = jnp.mean(x * x, axis=-1, keepdims=True)
    return x * lax.rsqrt(ms + NORM_EPS) * g


def _adaln_kernel(c_ref, w_ref, b_ref, o_ref):
    a = _silu(c_ref[...]).astype(BF16)
    o_ref[...] = jnp.dot(a, w_ref[...].astype(BF16), preferred_element_type=F32) + b_ref[...]


def _adaln(cond, w_mod, b_mod):
    r, d = cond.shape
    n = w_mod.shape[1]
    tn = _tile(n, 512)
    return pl.pallas_call(
        _adaln_kernel,
        grid=(n // tn,),
        in_specs=[pl.BlockSpec((r, d), lambda j: (0, 0)),
                  pl.BlockSpec((d, tn), lambda j: (0, j)),
                  pl.BlockSpec((1, tn), lambda j: (0, j))],
        out_specs=pl.BlockSpec((r, tn), lambda j: (0, j)),
        out_shape=jax.ShapeDtypeStruct((r, n), F32),
        compiler_params=_params("parallel"),
        name="adaln",
    )(cond, w_mod, b_mod.reshape(1, n))


def _inproj_kernel(x_ref, g_ref, sh_ref, sc_ref, w_ref, o_ref, h_ref, *, rc):
    @pl.when(pl.program_id(1) == 0)
    def _():
        g = g_ref[...]
        mul = 1.0 + sc_ref[...]
        sh = sh_ref[...]

        def body(r, carry):
            rows = pl.ds(pl.multiple_of(r * rc, rc), rc)
            h_ref[rows, :] = (_rms(x_ref[rows, :], g) * mul + sh).astype(BF16)
            return carry

        lax.fori_loop(0, x_ref.shape[0] // rc, body, 0)

    o_ref[...] = jnp.dot(h_ref[...], w_ref[...], preferred_element_type=F32).astype(o_ref.dtype)


def _inproj(x2d, norm_g, shift, scale, w, rows_per_mod, tm, tn):
    m, d = x2d.shape
    n = w.shape[1]
    mod_spec = pl.BlockSpec((None, 1, d), lambda i, j: ((i * tm) // rows_per_mod, 0, 0))
    return pl.pallas_call(
        functools.partial(_inproj_kernel, rc=32),
        grid=(m // tm, n // tn),
        in_specs=[pl.BlockSpec((tm, d), lambda i, j: (i, 0)),
                  pl.BlockSpec((1, d), lambda i, j: (0, 0)),
                  mod_spec, mod_spec,
                  pl.BlockSpec((d, tn), lambda i, j: (0, j))],
        out_specs=pl.BlockSpec((tm, tn), lambda i, j: (i, j)),
        out_shape=jax.ShapeDtypeStruct((m, n), BF16),
        scratch_shapes=[pltpu.VMEM((tm, d), BF16)],
        compiler_params=_params("parallel", "arbitrary"),
        name="inproj",
    )(x2d, norm_g.reshape(1, d), shift, scale, w)


def _outproj_kernel(*refs, widths):
    o_refs = refs[:len(widths)]
    w_ref, x_ref, gt_ref, y_ref = refs[len(widths):]
    acc = None
    start = 0
    for o_ref, wd in zip(o_refs, widths):
        part = jnp.dot(o_ref[...], w_ref[start:start + wd, :], preferred_element_type=F32)
        acc = part if acc is None else acc + part
        start += wd
    y_ref[...] = x_ref[...] + gt_ref[...] * acc


def _outproj(os, w, x2d, gate, rows_per_mod, tm):
    m, d = x2d.shape
    widths = tuple(o.shape[1] for o in os)
    k = w.shape[0]
    return pl.pallas_call(
        functools.partial(_outproj_kernel, widths=widths),
        grid=(m // tm,),
        in_specs=[pl.BlockSpec((tm, wd), lambda i: (i, 0)) for wd in widths] + [
            pl.BlockSpec((k, d), lambda i: (0, 0)),
            pl.BlockSpec((tm, d), lambda i: (i, 0)),
            pl.BlockSpec((None, 1, d), lambda i: ((i * tm) // rows_per_mod, 0, 0))],
        out_specs=pl.BlockSpec((tm, d), lambda i: (i, 0)),
        out_shape=jax.ShapeDtypeStruct((m, d), F32),
        compiler_params=_params("parallel"),
        name="outproj",
    )(*os, w, x2d, gate)


def _cumsum_rows(x, reverse):
    n = x.shape[0]
    row = lax.broadcasted_iota(jnp.int32, x.shape, 0)
    s = 1
    while s < n:
        if reverse:
            x = x + jnp.where(row < n - s, pltpu.roll(x, n - s, axis=0), 0.0)
        else:
            x = x + jnp.where(row >= s, pltpu.roll(x, s, axis=0), 0.0)
        s *= 2
    return x


def _hgrn_kernel(*refs, seq, hb, unroll, has_s0, emit_state):
    q_ref, ff_ref, fb_ref, v_ref, g_ref, lb_ref, on_ref = refs[:7]
    pos = 7
    s0_ref = None
    if has_s0:
        s0_ref = refs[pos]
        pos += 1
    o_ref = refs[pos]
    pos += 1
    so_ref = None
    if emit_state:
        so_ref = refs[pos]
        pos += 1
    oacc_ref, st_ref = refs[pos:]

    nc = seq // A_CHUNK
    ri = lax.broadcasted_iota(jnp.int32, (A_CHUNK, A_CHUNK), 0)
    ci = lax.broadcasted_iota(jnp.int32, (A_CHUNK, A_CHUNK), 1)
    masks = (ri >= ci, ci >= ri)

    for j in range(hb):
        for d in range(2):
            if has_s0:
                st_ref[j, d] = s0_ref[d, j].T
            else:
                st_ref[j, d] = jnp.zeros((HEAD_DIM, HEAD_DIM), F32)

    def chunk(j, d, n, accumulate):
        cols = slice(j * HEAD_DIM, (j + 1) * HEAD_DIM)
        rows = pl.ds(pl.multiple_of(n * A_CHUNK, A_CHUNK), A_CHUNK)
        q = _silu(q_ref[rows, cols].astype(F32))
        v = v_ref[rows, cols]
        fr = (fb_ref if d else ff_ref)[rows, cols].astype(F32)
        lb = lb_ref[d:d + 1, cols]
        f = lb + (1.0 - lb) * jax.nn.sigmoid(fr)
        k = 1.0 - f
        b = _cumsum_rows(jnp.log(f), reverse=bool(d))
        btot = b[0:1, :] if d else b[A_CHUNK - 1:A_CHUNK, :]
        qd = (q * jnp.exp(b)).astype(BF16)
        ki = (k * jnp.exp(-b)).astype(BF16)
        ke = (k * jnp.exp(btot - b)).astype(BF16)
        att = lax.dot_general(qd, ki, _NT, preferred_element_type=F32)
        att = jnp.where(masks[d], att, 0.0).astype(BF16)
        st = st_ref[j, d]
        o = (jnp.dot(att, v, preferred_element_type=F32)
             + lax.dot_general(qd, st.astype(BF16), _NT, preferred_element_type=F32))
        st_ref[j, d] = st * jnp.exp(btot) + lax.dot_general(v, ke, _TN, preferred_element_type=F32)
        if accumulate:
            oacc_ref[rows, cols] += o
        else:
            oacc_ref[rows, cols] = o

    def body(n, carry, accumulate):
        for j in range(hb):
            chunk(j, 0, n, accumulate)
            chunk(j, 1, nc - 1 - n, accumulate)
        return carry

    lax.fori_loop(0, nc // 2, functools.partial(body, accumulate=False), 0, unroll=unroll)
    lax.fori_loop(nc // 2, nc, functools.partial(body, accumulate=True), 0, unroll=unroll)

    rc = 128
    on = on_ref[...]

    def epilogue(r, carry):
        rows = pl.ds(pl.multiple_of(r * rc, rc), rc)
        for j in range(hb):
            cols = slice(j * HEAD_DIM, (j + 1) * HEAD_DIM)
            y = _rms(oacc_ref[rows, cols], on)
            o_ref[rows, cols] = (y * _silu(g_ref[rows, cols].astype(F32))).astype(o_ref.dtype)
        return carry

    lax.fori_loop(0, seq // rc, epilogue, 0)

    if emit_state:
        for j in range(hb):
            for d in range(2):
                so_ref[d, j] = st_ref[j, d].T


def _hgrn(u3, lb, onorm, s0, hb, unroll, emit_state):
    bsz, seq, _ = u3.shape
    bw = hb * HEAD_DIM
    nh = A_HEADS // hb

    def seg(s):
        return pl.BlockSpec((None, seq, bw), lambda b, h: (b, 0, s * nh + h))

    in_specs = [seg(0), seg(1), seg(2), seg(3), seg(4),
                pl.BlockSpec((2, bw), lambda b, h: (0, h)),
                pl.BlockSpec((1, HEAD_DIM), lambda b, h: (0, 0))]
    args = [u3, u3, u3, u3, u3, lb, onorm.reshape(1, HEAD_DIM)]
    st_spec = pl.BlockSpec((None, 2, hb, HEAD_DIM, HEAD_DIM), lambda b, h: (b, 0, h, 0, 0))
    if s0 is not None:
        in_specs.append(st_spec)
        args.append(s0)
    out_specs = [pl.BlockSpec((None, seq, bw), lambda b, h: (b, 0, h))]
    out_shape = [jax.ShapeDtypeStruct((bsz, seq, A_HEADS * HEAD_DIM), BF16)]
    if emit_state:
        out_specs.append(st_spec)
        out_shape.append(jax.ShapeDtypeStruct((bsz, 2, A_HEADS, HEAD_DIM, HEAD_DIM), F32))
    return pl.pallas_call(
        functools.partial(_hgrn_kernel, seq=seq, hb=hb, unroll=unroll, has_s0=s0 is not None,
                          emit_state=emit_state),
        grid=(bsz, nh),
        in_specs=in_specs,
        out_specs=out_specs,
        out_shape=out_shape,
        scratch_shapes=[pltpu.VMEM((seq, bw), F32),
                        pltpu.VMEM((hb, 2, HEAD_DIM, HEAD_DIM), F32)],
        compiler_params=_params("parallel", "parallel"),
        name="hgrn",
    )(*args)


def _rope(x, cos, sin_a, sin_b):
    return (x * cos + pltpu.roll(x, HEAD_DIM - ROPE_QUARTER, axis=1) * sin_a
            + pltpu.roll(x, ROPE_QUARTER, axis=1) * sin_b)


def _attn_kernel(*refs, seq, tq, nqb, ctx_len, rope, window, sink, emit_kv):
    refs = list(refs)
    q_ref, gate_ref, k_ref, v_ref = refs[:4]
    pos = 4
    kc_ref = vc_ref = cos_ref = sa_ref = sb_ref = bias_ref = sink_ref = ko_ref = vo_ref = None
    if ctx_len:
        kc_ref, vc_ref = refs[pos:pos + 2]
        pos += 2
    if rope:
        cos_ref, sa_ref, sb_ref = refs[pos:pos + 3]
        pos += 3
    if window:
        bias_ref = refs[pos]
        pos += 1
    qn_ref, kn_ref = refs[pos:pos + 2]
    pos += 2
    if sink:
        sink_ref = refs[pos]
        pos += 1
    o_ref = refs[pos]
    pos += 1
    if emit_kv:
        ko_ref, vo_ref = refs[pos:pos + 2]
        pos += 2
    ks_ref, vs_ref = refs[pos:]

    kvh = pl.program_id(1)
    step = pl.program_id(2)
    nq = seq // tq
    pad = WINDOW if window else 0
    ctx_off = seq + 2 * pad
    qscale = LOG2E / math.sqrt(HEAD_DIM)

    @pl.when(step == 0)
    def _():
        if window:
            zeros = jnp.zeros((pad, HEAD_DIM), BF16)
            for ref in (ks_ref, vs_ref):
                ref[0:pad, :] = zeros
                ref[pad + seq:ctx_off, :] = zeros
        rc = 256
        kn = kn_ref[...]

        def body(r, carry):
            rows = pl.ds(pl.multiple_of(r * rc, rc), rc)
            dst = pl.ds(pl.multiple_of(pad + r * rc, HEAD_DIM), rc)
            k = _rms(k_ref[rows, :].astype(F32), kn)
            if emit_kv:
                ko_ref[rows, :] = k
                vo_ref[rows, :] = v_ref[rows, :].astype(F32)
            if rope:
                k = _rope(k, cos_ref[rows, :], sa_ref[rows, :], sb_ref[rows, :])
            ks_ref[dst, :] = k.astype(BF16)
            vs_ref[dst, :] = v_ref[rows, :]
            return carry

        lax.fori_loop(0, seq // rc, body, 0)
        if ctx_len:
            ks_ref[ctx_off:ctx_off + ctx_len, :] = kc_ref[...].astype(BF16)
            vs_ref[ctx_off:ctx_off + ctx_len, :] = vc_ref[...].astype(BF16)

    qn = qn_ref[...]
    sink_col = None
    if sink:
        sink_col = jnp.concatenate(
            [jnp.full((tq, 1), sink_ref[kvh * GROUP + h] * LOG2E, F32) for h in range(GROUP)], axis=0)

    def one_block(t):
        qi = step * nqb + t
        r0 = t * tq
        qrows = pl.ds(pl.multiple_of(qi * tq, tq), tq)
        xs = []
        for h in range(GROUP):
            x = _rms(q_ref[r0:r0 + tq, h * HEAD_DIM:(h + 1) * HEAD_DIM].astype(F32), qn)
            if rope:
                x = _rope(x, cos_ref[qrows, :], sa_ref[qrows, :], sb_ref[qrows, :])
            xs.append((x * qscale).astype(BF16))
        q = jnp.concatenate(xs, axis=0)

        if window:
            span = 3 * WINDOW
            start = pl.multiple_of(qi * tq, HEAD_DIM)
            edge = jnp.where(qi == 0, 0, jnp.where(qi == nq - 1, 2, 1))
            parts = [(ks_ref[pl.ds(start, span), :], vs_ref[pl.ds(start, span), :], bias_ref[edge])]
            if ctx_len:
                parts.append((ks_ref[ctx_off:ctx_off + ctx_len, :], vs_ref[ctx_off:ctx_off + ctx_len, :], None))
        else:
            parts = [(ks_ref[...], vs_ref[...], None)]

        ss = []
        for k, _, bias in parts:
            s = lax.dot_general(q, k, _NT, preferred_element_type=F32)
            ss.append(s if bias is None else s + bias)
        m = functools.reduce(jnp.maximum, [jnp.max(s, axis=-1, keepdims=True) for s in ss])
        l = None
        if sink:
            m = jnp.maximum(m, sink_col)
            l = jnp.exp2(sink_col - m)
        o = None
        for s, (_, v, _) in zip(ss, parts):
            p = jnp.exp2(s - m)
            ps = jnp.sum(p, axis=-1, keepdims=True)
            l = ps if l is None else l + ps
            pv = jnp.dot(p.astype(BF16), v, preferred_element_type=F32)
            o = pv if o is None else o + pv
        o = o * (1.0 / l)
        for h in range(GROUP):
            cols = slice(h * HEAD_DIM, (h + 1) * HEAD_DIM)
            gate = _silu(gate_ref[r0:r0 + tq, cols].astype(F32))
            o_ref[r0:r0 + tq, cols] = (o[h * tq:(h + 1) * tq, :] * gate).astype(o_ref.dtype)

    for t in range(nqb):
        one_block(t)


def _window_bias(tq):
    r = (jnp.arange(GROUP * tq) % tq)[:, None]
    c = jnp.arange(3 * WINDOW)[None, :]
    band = (c >= r) & (c <= r + 2 * WINDOW)
    keep = jnp.stack([band & (c >= WINDOW), band, band & (c < 2 * WINDOW)])
    return jnp.where(keep, 0.0, NEG).astype(F32)


def _attn(u3, q_off, k_off, v_off, g_off, kv_heads, qnorm, knorm, *, tq, nqb, ctx=None, rope=None,
          window=False, sink=None, emit_kv=False):
    bsz, seq, _ = u3.shape
    gw = GROUP * HEAD_DIM
    heads = kv_heads * GROUP
    ctx_len = 0 if ctx is None else ctx[0].shape[1]
    pad = WINDOW if window else 0
    assert not window or (tq == WINDOW and seq // tq >= 2)
    assert seq % (tq * nqb) == 0

    def wide(off):
        return pl.BlockSpec((None, tq * nqb, gw), lambda b, h, i: (b, i, off // gw + h))

    def narrow(off, rows):
        return pl.BlockSpec((None, rows, HEAD_DIM), lambda b, h, i: (b, 0, off // HEAD_DIM + h))

    def whole(shape):
        return pl.BlockSpec(shape, lambda b, h, i: (0,) * len(shape))

    in_specs = [wide(q_off), wide(g_off), narrow(k_off, seq), narrow(v_off, seq)]
    args = [u3, u3, u3, u3]
    if ctx is not None:
        for cache in ctx:
            in_specs.append(narrow(0, ctx_len))
            args.append(cache.reshape(bsz, ctx_len, kv_heads * HEAD_DIM))
    if rope is not None:
        for t in rope:
            in_specs.append(whole((seq, HEAD_DIM)))
            args.append(t)
    if window:
        in_specs.append(whole((3, GROUP * tq, 3 * WINDOW)))
        args.append(_window_bias(tq))
    in_specs += [whole((1, HEAD_DIM)), whole((1, HEAD_DIM))]
    args += [qnorm.reshape(1, HEAD_DIM), knorm.reshape(1, HEAD_DIM)]
    if sink is not None:
        in_specs.append(pl.BlockSpec(memory_space=pltpu.SMEM))
        args.append(sink)
    out_specs = [pl.BlockSpec((None, tq * nqb, gw), lambda b, h, i: (b, i, h))]
    out_shape = [jax.ShapeDtypeStruct((bsz, seq, heads * HEAD_DIM), BF16)]
    if emit_kv:
        for _ in range(2):
            out_specs.append(narrow(0, seq))
            out_shape.append(jax.ShapeDtypeStruct((bsz, seq, kv_heads * HEAD_DIM), F32))
    rows = seq + 2 * pad + ctx_len
    kern = functools.partial(_attn_kernel, seq=seq, tq=tq, nqb=nqb, ctx_len=ctx_len,
                             rope=rope is not None, window=window, sink=sink is not None,
                             emit_kv=emit_kv)
    return pl.pallas_call(
        kern,
        grid=(bsz, kv_heads, seq // (tq * nqb)),
        in_specs=in_specs,
        out_specs=out_specs,
        out_shape=out_shape,
        scratch_shapes=[pltpu.VMEM((rows, HEAD_DIM), BF16),
                        pltpu.VMEM((rows, HEAD_DIM), BF16)],
        compiler_params=_params("parallel", "parallel", "arbitrary"),
        name="attn",
    )(*args)


def _rope_tables(n_tokens):
    rows = n_tokens // GRID_W
    row = jnp.repeat(jnp.arange(rows), GRID_W).astype(F32)
    col = (jnp.arange(rows * GRID_W) % GRID_W).astype(F32)
    inv = ROPE_THETA ** (-jnp.arange(ROPE_QUARTER, dtype=F32) / ROPE_QUARTER)
    ar = row[:, None] * inv
    ac = col[:, None] * inv
    ang = jnp.concatenate([ar, ar, ac, ac], axis=-1)
    cos, sin = jnp.cos(ang), jnp.sin(ang)
    first = (jnp.arange(HEAD_DIM) % (2 * ROPE_QUARTER)) < ROPE_QUARTER
    return cos, jnp.where(first, -sin, 0.0), jnp.where(first, 0.0, sin)


def _tile(m, pref):
    t = pref
    while m % t:
        t //= 2
    return t


def kernel(x_prompt, x_sample, state_l0_hgrn, cache_l0_k, cache_l0_v, cache_l1_k, cache_l1_v, c, c_ctx, lb_gamma, l0_norm, l0_w_mod, l0_b_mod, l0_w_in, l0_w_out, l0_a_onorm, l0_b_qnorm, l0_b_knorm, l1_norm, l1_w_mod, l1_b_mod, l1_w_in, l1_w_out, l1_c_qnorm, l1_c_knorm, l1_c_sink):
    pb, pl_, d = x_prompt.shape
    sb, sl, _ = x_sample.shape
    aw = A_HEADS * HEAD_DIM
    bkv = B_KV_HEADS * HEAD_DIM
    bw = B_KV_HEADS * GROUP * HEAD_DIM
    ckv = C_KV_HEADS * HEAD_DIM
    cw = C_KV_HEADS * GROUP * HEAD_DIM

    lb = jnp.cumsum(jax.nn.softmax(lb_gamma.astype(F32), axis=0), axis=0)[0]
    rope = _rope_tables(sl)

    nrow = -(-(sb + 1) // 8) * 8
    cond = jnp.zeros((nrow, d), F32).at[:sb].set(c).at[sb].set(c_ctx)
    xs = (x_prompt.reshape(pb * pl_, d), x_sample.reshape(sb * sl, d))
    tms = (_tile(pb * pl_, 1024), _tile(sl, 1024))
    rows_per_mod = (pb * pl_, sl)

    def mods(w_mod, b_mod):
        m = _adaln(cond, w_mod, b_mod)
        parts = [m[:, i * d:(i + 1) * d] for i in range(3)]
        return ([p[sb:sb + 1, None, :] for p in parts], [p[:sb, None, :] for p in parts])

    def tn_for(n):
        for t in (1280, 1024, 768, 512, 256, 128):
            if n % t == 0:
                return t
        return n

    mod_p, mod_s = mods(l0_w_mod, l0_b_mod)
    w_in = l0_w_in.astype(BF16)
    w_out = l0_w_out.astype(BF16)
    tn = tn_for(w_in.shape[1])
    u_p = _inproj(xs[0], l0_norm, mod_p[0], mod_p[1], w_in, rows_per_mod[0], tms[0], tn).reshape(pb, pl_, -1)
    u_s = _inproj(xs[1], l0_norm, mod_s[0], mod_s[1], w_in, rows_per_mod[1], tms[1], tn).reshape(sb, sl, -1)

    oa_p, new_state = _hgrn(u_p, lb, l0_a_onorm, None, hb=A_HEADS, unroll=1, emit_state=True)
    (oa_s,) = _hgrn(u_s, lb, l0_a_onorm, state_l0_hgrn, hb=4, unroll=2, emit_state=False)

    q_off = 5 * aw
    k_off = q_off + bw
    v_off = k_off + bkv
    g_off = v_off + bkv
    ob_p, k0, v0 = _attn(u_p, q_off, k_off, v_off, g_off, B_KV_HEADS, l0_b_qnorm, l0_b_knorm,
                         tq=128, nqb=2, emit_kv=True)
    (ob_s,) = _attn(u_s, q_off, k_off, v_off, g_off, B_KV_HEADS, l0_b_qnorm, l0_b_knorm,
                    tq=128, nqb=2, ctx=(cache_l0_k, cache_l0_v), rope=rope)

    otm = (_tile(pb * pl_, 512), _tile(sl, 512))
    y_p = _outproj([oa_p.reshape(pb * pl_, aw), ob_p.reshape(pb * pl_, bw)], w_out, xs[0], mod_p[2],
                   rows_per_mod[0], otm[0])
    y_s = _outproj([oa_s.reshape(sb * sl, aw), ob_s.reshape(sb * sl, bw)], w_out, xs[1], mod_s[2],
                   rows_per_mod[1], otm[1])

    mod_p, mod_s = mods(l1_w_mod, l1_b_mod)
    w_in = l1_w_in.astype(BF16)
    w_out = l1_w_out.astype(BF16)
    tn = tn_for(w_in.shape[1])
    u_p = _inproj(y_p, l1_norm, mod_p[0], mod_p[1], w_in, rows_per_mod[0], tms[0], tn).reshape(pb, pl_, -1)
    u_s = _inproj(y_s, l1_norm, mod_s[0], mod_s[1], w_in, rows_per_mod[1], tms[1], tn).reshape(sb, sl, -1)

    k_off = cw
    v_off = k_off + ckv
    g_off = v_off + ckv
    oc_p, k1, v1 = _attn(u_p, 0, k_off, v_off, g_off, C_KV_HEADS, l1_c_qnorm, l1_c_knorm,
                         tq=128, nqb=2, sink=l1_c_sink, emit_kv=True)
    (oc_s,) = _attn(u_s, 0, k_off, v_off, g_off, C_KV_HEADS, l1_c_qnorm, l1_c_knorm,
                    tq=WINDOW, nqb=4, ctx=(cache_l1_k, cache_l1_v), rope=rope, window=True, sink=l1_c_sink)

    z_p = _outproj([oc_p.reshape(pb * pl_, cw)], w_out, y_p, mod_p[2], rows_per_mod[0], otm[0])
    z_s = _outproj([oc_s.reshape(sb * sl, cw)], w_out, y_s, mod_s[2], rows_per_mod[1], otm[1])

    return (z_p.reshape(pb, pl_, d), z_s.reshape(sb, sl, d), new_state,
            k0.reshape(pb, pl_, B_KV_HEADS, HEAD_DIM), v0.reshape(pb, pl_, B_KV_HEADS, HEAD_DIM),
            k1.reshape(pb, pl_, C_KV_HEADS, HEAD_DIM), v1.reshape(pb, pl_, C_KV_HEADS, HEAD_DIM))
```

```python
import functools
import math

import jax
import jax.numpy as jnp
import numpy as np
from jax import lax
from jax.experimental import pallas as pl
from jax.experimental.pallas import tpu as pltpu

F32 = jnp.float32
BF16 = jnp.bfloat16

HEAD_DIM = 128
GRID_W = 64
ROPE_QUARTER = HEAD_DIM // 4
ROPE_THETA = 10000.0
NORM_EPS = 1e-6
A_HEADS = 8
A_CHUNK = 32
B_KV_HEADS = 2
C_KV_HEADS = 4
GROUP = 4
WINDOW = 128
NEG = -1e30
LOG2E = math.log2(math.e)
VMEM_LIMIT = 56 * 1024 * 1024

_ROT_FIRST = (np.arange(HEAD_DIM) % (2 * ROPE_QUARTER)) < ROPE_QUARTER
_ROT_PARTNER = np.where(_ROT_FIRST, np.arange(HEAD_DIM) + ROPE_QUARTER, np.arange(HEAD_DIM) - ROPE_QUARTER)

_NT = (((1,), (1,)), ((), ()))
_TN = (((0,), (0,)), ((), ()))


def _params(*sem):
    return pltpu.CompilerParams(dimension_semantics=sem, vmem_limit_bytes=VMEM_LIMIT)


def _silu(x):
    return x * jax.nn.sigmoid(x)


def _rms(x, g):
    ms = jnp.mean(x * x, axis=-1, keepdims=True)
    return x * lax.rsqrt(ms + NORM_EPS) * g


def _adaln_kernel(c_ref, w_ref, b_ref, o_ref):
    a = _silu(c_ref[...]).astype(BF16)
    o_ref[...] = jnp.dot(a, w_ref[...].astype(BF16), preferred_element_type=F32) + b_ref[...]


def _adaln(cond, w_mod, b_mod):
    r, d = cond.shape
    n = w_mod.shape[1]
    tn = _tile(n, 512)
    return pl.pallas_call(
        _adaln_kernel,
        grid=(n // tn,),
        in_specs=[pl.BlockSpec((r, d), lambda j: (0, 0)),
                  pl.BlockSpec((d, tn), lambda j: (0, j)),
                  pl.BlockSpec((1, tn), lambda j: (0, j))],
        out_specs=pl.BlockSpec((r, tn), lambda j: (0, j)),
        out_shape=jax.ShapeDtypeStruct((r, n), F32),
        compiler_params=_params("parallel"),
        name="adaln",
    )(cond, w_mod, b_mod.reshape(1, n))


def _inproj_kernel(x_ref, g_ref, sh_ref, sc_ref, w_ref, o_ref, h_ref, *, rc):
    @pl.when(pl.program_id(1) == 0)
    def _():
        g = g_ref[...]
        mul = 1.0 + sc_ref[...]
        sh = sh_ref[...]

        def body(r, carry):
            rows = pl.ds(pl.multiple_of(r * rc, rc), rc)
            h_ref[rows, :] = (_rms(x_ref[rows, :], g) * mul + sh).astype(BF16)
            return carry

        lax.fori_loop(0, x_ref.shape[0] // rc, body, 0, unroll=2)

    o_ref[...] = jnp.dot(h_ref[...], w_ref[...], preferred_element_type=F32).astype(o_ref.dtype)


def _inproj(x2d, norm_g, shift, scale, w, rows_per_mod, tm, tn):
    m, d = x2d.shape
    n = w.shape[1]
    mod_spec = pl.BlockSpec((None, 1, d), lambda i, j: ((i * tm) // rows_per_mod, 0, 0))
    return pl.pallas_call(
        functools.partial(_inproj_kernel, rc=32),
        grid=(m // tm, n // tn),
        in_specs=[pl.BlockSpec((tm, d), lambda i, j: (i, 0)),
                  pl.BlockSpec((1, d), lambda i, j: (0, 0)),
                  mod_spec, mod_spec,
                  pl.BlockSpec((d, tn), lambda i, j: (0, j))],
        out_specs=pl.BlockSpec((tm, tn), lambda i, j: (i, j)),
        out_shape=jax.ShapeDtypeStruct((m, n), BF16),
        scratch_shapes=[pltpu.VMEM((tm, d), BF16)],
        compiler_params=_params("parallel", "arbitrary"),
        name="inproj",
    )(x2d, norm_g.reshape(1, d), shift, scale, w)


def _outproj_kernel(*refs, widths):
    o_refs = refs[:len(widths)]
    w_ref, x_ref, gt_ref, y_ref = refs[len(widths):]
    acc = None
    start = 0
    for o_ref, wd in zip(o_refs, widths):
        part = jnp.dot(o_ref[...], w_ref[start:start + wd, :], preferred_element_type=F32)
        acc = part if acc is None else acc + part
        start += wd
    y_ref[...] = x_ref[...] + gt_ref[...] * acc


def _outproj(os, w, x2d, gate, rows_per_mod, tm):
    m, d = x2d.shape
    widths = tuple(o.shape[1] for o in os)
    k = w.shape[0]
    return pl.pallas_call(
        functools.partial(_outproj_kernel, widths=widths),
        grid=(m // tm,),
        in_specs=[pl.BlockSpec((tm, wd), lambda i: (i, 0)) for wd in widths] + [
            pl.BlockSpec((k, d), lambda i: (0, 0)),
            pl.BlockSpec((tm, d), lambda i: (i, 0)),
            pl.BlockSpec((None, 1, d), lambda i: ((i * tm) // rows_per_mod, 0, 0))],
        out_specs=pl.BlockSpec((tm, d), lambda i: (i, 0)),
        out_shape=jax.ShapeDtypeStruct((m, d), F32),
        compiler_params=_params("parallel"),
        name="outproj",
    )(*os, w, x2d, gate)


def _cumsum_rows(x, reverse):
    n = x.shape[0]
    row = lax.broadcasted_iota(jnp.int32, x.shape, 0)
    s = 1
    while s < n:
        if reverse:
            x = x + jnp.where(row < n - s, pltpu.roll(x, n - s, axis=0), 0.0)
        else:
            x = x + jnp.where(row >= s, pltpu.roll(x, s, axis=0), 0.0)
        s *= 2
    return x


def _hgrn_kernel(*refs, seq, hb, unroll, has_s0, emit_state):
    q_ref, ff_ref, fb_ref, v_ref, g_ref, lb_ref, on_ref = refs[:7]
    pos = 7
    s0_ref = None
    if has_s0:
        s0_ref = refs[pos]
        pos += 1
    o_ref = refs[pos]
    pos += 1
    so_ref = None
    if emit_state:
        so_ref = refs[pos]
        pos += 1
    oacc_ref, st_ref = refs[pos:]

    nc = seq // A_CHUNK
    ri = lax.broadcasted_iota(jnp.int32, (A_CHUNK, A_CHUNK), 0)
    ci = lax.broadcasted_iota(jnp.int32, (A_CHUNK, A_CHUNK), 1)
    masks = (ri >= ci, ci >= ri)

    for j in range(hb):
        for d in range(2):
            if has_s0:
                st_ref[j, d] = s0_ref[d, j].T
            else:
                st_ref[j, d] = jnp.zeros((HEAD_DIM, HEAD_DIM), F32)

    def chunk(j, d, n, accumulate):
        cols = slice(j * HEAD_DIM, (j + 1) * HEAD_DIM)
        rows = pl.ds(pl.multiple_of(n * A_CHUNK, A_CHUNK), A_CHUNK)
        q = _silu(q_ref[rows, cols].astype(F32))
        v = v_ref[rows, cols]
        fr = (fb_ref if d else ff_ref)[rows, cols].astype(F32)
        lb = lb_ref[d:d + 1, cols]
        f = lb + (1.0 - lb) * jax.nn.sigmoid(fr)
        k = 1.0 - f
        b = _cumsum_rows(jnp.log(f), reverse=bool(d))
        btot = b[0:1, :] if d else b[A_CHUNK - 1:A_CHUNK, :]
        dec = jnp.exp(btot)
        qd = (q * jnp.exp(b)).astype(BF16)
        kinv = k * jnp.exp(-b)
        ki = kinv.astype(BF16)
        ke = (kinv * dec).astype(BF16)
        att = lax.dot_general(qd, ki, _NT, preferred_element_type=F32)
        att = jnp.where(masks[d], att, 0.0).astype(BF16)
        st = st_ref[j, d]
        o = (jnp.dot(att, v, preferred_element_type=F32)
             + lax.dot_general(qd, st.astype(BF16), _NT, preferred_element_type=F32))
        st_ref[j, d] = st * dec + lax.dot_general(v, ke, _TN, preferred_element_type=F32)
        if accumulate:
            oacc_ref[rows, cols] += o
        else:
            oacc_ref[rows, cols] = o

    def body(n, carry, accumulate):
        for j in range(hb):
            chunk(j, 0, n, accumulate)
            chunk(j, 1, nc - 1 - n, accumulate)
        return carry

    lax.fori_loop(0, nc // 2, functools.partial(body, accumulate=False), 0, unroll=unroll)
    lax.fori_loop(nc // 2, nc, functools.partial(body, accumulate=True), 0, unroll=unroll)

    rc = 128
    on = on_ref[...]

    def epilogue(r, carry):
        rows = pl.ds(pl.multiple_of(r * rc, rc), rc)
        for j in range(hb):
            cols = slice(j * HEAD_DIM, (j + 1) * HEAD_DIM)
            y = _rms(oacc_ref[rows, cols], on)
            o_ref[rows, cols] = (y * _silu(g_ref[rows, cols].astype(F32))).astype(o_ref.dtype)
        return carry

    lax.fori_loop(0, seq // rc, epilogue, 0)

    if emit_state:
        for j in range(hb):
            for d in range(2):
                so_ref[d, j] = st_ref[j, d].T


def _hgrn(u3, lb, onorm, s0, hb, unroll, emit_state):
    bsz, seq, _ = u3.shape
    bw = hb * HEAD_DIM
    nh = A_HEADS // hb

    def seg(s):
        return pl.BlockSpec((None, seq, bw), lambda b, h: (b, 0, s * nh + h))

    in_specs = [seg(0), seg(1), seg(2), seg(3), seg(4),
                pl.BlockSpec((2, bw), lambda b, h: (0, h)),
                pl.BlockSpec((1, HEAD_DIM), lambda b, h: (0, 0))]
    args = [u3, u3, u3, u3, u3, lb, onorm.reshape(1, HEAD_DIM)]
    st_spec = pl.BlockSpec((None, 2, hb, HEAD_DIM, HEAD_DIM), lambda b, h: (b, 0, h, 0, 0))
    if s0 is not None:
        in_specs.append(st_spec)
        args.append(s0)
    out_specs = [pl.BlockSpec((None, seq, bw), lambda b, h: (b, 0, h))]
    out_shape = [jax.ShapeDtypeStruct((bsz, seq, A_HEADS * HEAD_DIM), BF16)]
    if emit_state:
        out_specs.append(st_spec)
        out_shape.append(jax.ShapeDtypeStruct((bsz, 2, A_HEADS, HEAD_DIM, HEAD_DIM), F32))
    return pl.pallas_call(
        functools.partial(_hgrn_kernel, seq=seq, hb=hb, unroll=unroll, has_s0=s0 is not None,
                          emit_state=emit_state),
        grid=(bsz, nh),
        in_specs=in_specs,
        out_specs=out_specs,
        out_shape=out_shape,
        scratch_shapes=[pltpu.VMEM((seq, bw), F32),
                        pltpu.VMEM((hb, 2, HEAD_DIM, HEAD_DIM), F32)],
        compiler_params=_params("parallel", "parallel"),
        name="hgrn",
    )(*args)


def _norm_rope(xb, g_cos, gp_sin, rot_ref, out_scale):
    x = xb.astype(F32)
    sq = x * x
    hi = sq.astype(BF16)
    lo = (sq - hi.astype(F32)).astype(BF16)
    mean_mat = jnp.full((HEAD_DIM, HEAD_DIM), 1.0 / HEAD_DIM, BF16)
    ms = (jnp.dot(hi, mean_mat, preferred_element_type=F32)
          + jnp.dot(lo, mean_mat, preferred_element_type=F32))
    r = lax.rsqrt(ms + NORM_EPS)
    if out_scale != 1.0:
        r = r * out_scale
    y = x * g_cos
    if gp_sin is not None:
        y = y + jnp.dot(xb, rot_ref[...], preferred_element_type=F32) * gp_sin
    return y * r


def _attn_kernel(*refs, seq, tq, nqb, ctx_len, rope, window, sink, emit_kv):
    refs = list(refs)
    q_ref, gate_ref, k_ref, v_ref = refs[:4]
    pos = 4
    kc_ref = vc_ref = cos_ref = sin_ref = rot_ref = bias_ref = sink_ref = ko_ref = vo_ref = None
    if ctx_len:
        kc_ref, vc_ref = refs[pos:pos + 2]
        pos += 2
    if rope:
        cos_ref, sin_ref, rot_ref = refs[pos:pos + 3]
        pos += 3
    if window:
        bias_ref = refs[pos]
        pos += 1
    qn_ref, kn_ref = refs[pos:pos + 2]
    pos += 2
    if sink:
        sink_ref = refs[pos]
        pos += 1
    o_ref = refs[pos]
    pos += 1
    if emit_kv:
        ko_ref, vo_ref = refs[pos:pos + 2]
        pos += 2
    ks_ref, vs_ref = refs[pos:pos + 2]
    pos += 2
    qc_ref = qs_ref = None
    if rope:
        qc_ref, qs_ref = refs[pos:]

    kvh = pl.program_id(1)
    step = pl.program_id(2)
    nq = seq // tq
    pad = WINDOW if window else 0
    ctx_off = seq + 2 * pad
    qscale = LOG2E / math.sqrt(HEAD_DIM)

    @pl.when(step == 0)
    def _():
        if window:
            zeros = jnp.zeros((pad, HEAD_DIM), BF16)
            for ref in (ks_ref, vs_ref):
                ref[0:pad, 0:HEAD_DIM] = zeros
                ref[pad + seq:ctx_off, 0:HEAD_DIM] = zeros
        rc = 256
        kg = kn_ref[0:1, :]
        kgp = kn_ref[1:2, :]
        ones = jnp.ones((rc, HEAD_DIM), BF16)

        def body(r, carry):
            rows = pl.ds(pl.multiple_of(r * rc, rc), rc)
            dst = pl.ds(pl.multiple_of(pad + r * rc, HEAD_DIM), rc)
            if rope:
                cos = cos_ref[rows, :]
                sin = sin_ref[rows, :]
                qc_ref[rows, :] = cos * qn_ref[0:1, :]
                qs_ref[rows, :] = sin * qn_ref[1:2, :]
                k = _norm_rope(k_ref[rows, :], cos * kg, sin * kgp, rot_ref, 1.0)
            else:
                k = _norm_rope(k_ref[rows, :], kg, None, None, 1.0)
            if emit_kv:
                ko_ref[rows, :] = k
                vo_ref[rows, :] = v_ref[rows, :].astype(F32)
            ks_ref[dst, :] = k.astype(BF16)
            vs_ref[dst, 0:HEAD_DIM] = v_ref[rows, :]
            return carry

        lax.fori_loop(0, seq // rc, body, 0)
        if ctx_len:
            ks_ref[ctx_off:ctx_off + ctx_len, :] = kc_ref[...].astype(BF16)
            vs_ref[ctx_off:ctx_off + ctx_len, 0:HEAD_DIM] = vc_ref[...].astype(BF16)

        def fill(r, carry):
            vs_ref[pl.ds(pl.multiple_of(r * rc, rc), rc), HEAD_DIM:2 * HEAD_DIM] = ones
            return carry

        lax.fori_loop(0, vs_ref.shape[0] // rc, fill, 0)

    sink_col = None
    if sink:
        sink_col = jnp.concatenate(
            [jnp.full((tq, 1), sink_ref[kvh * GROUP + h] * LOG2E, F32) for h in range(GROUP)], axis=0)

    def prep(t):
        qi = step * nqb + t
        qrows = pl.ds(pl.multiple_of(qi * tq, tq), tq)
        xs = []
        for h in range(GROUP):
            xb = q_ref[t * tq:(t + 1) * tq, h * HEAD_DIM:(h + 1) * HEAD_DIM]
            if rope:
                x = _norm_rope(xb, qc_ref[qrows, :], qs_ref[qrows, :], rot_ref, qscale)
            else:
                x = _norm_rope(xb, qn_ref[0:1, :], None, None, qscale)
            xs.append(x.astype(BF16))
        return jnp.concatenate(xs, axis=0)

    def key_parts(t):
        qi = step * nqb + t
        if not window:
            return [(slice(None), None)]
        start = pl.multiple_of(qi * tq, HEAD_DIM)
        edge = jnp.where(qi == 0, 0, jnp.where(qi == nq - 1, 2, 1))
        parts = [(pl.ds(start, 3 * WINDOW), bias_ref[edge])]
        if ctx_len:
            parts.append((slice(ctx_off, ctx_off + ctx_len), None))
        return parts

    def logits(q, parts):
        ss = []
        for rows, bias in parts:
            s = lax.dot_general(q, ks_ref[rows, :], _NT, preferred_element_type=F32)
            ss.append(s if bias is None else s + bias)
        return ss

    def row_max(ss):
        slabs = [s[:, c:c + HEAD_DIM] for s in ss for c in range(0, s.shape[1], HEAD_DIM)]
        return jnp.max(functools.reduce(jnp.maximum, slabs), axis=-1, keepdims=True)

    def weighted(ss, m, parts):
        acc = None
        for s, (rows, _) in zip(ss, parts):
            pv = jnp.dot(jnp.exp2(s - m).astype(BF16), vs_ref[rows, :], preferred_element_type=F32)
            acc = pv if acc is None else acc + pv
        return acc

    def finish(t, acc, m):
        l = acc[:, HEAD_DIM:]
        if sink:
            l = l + jnp.exp2(sink_col - m)
        for h in range(GROUP):
            cols = slice(h * HEAD_DIM, (h + 1) * HEAD_DIM)
            rows = slice(h * tq, (h + 1) * tq)
            gate = gate_ref[t * tq:(t + 1) * tq, cols].astype(F32)
            inv = 1.0 / (l[rows, :] * (1.0 + jnp.exp(-gate)))
            o_ref[t * tq:(t + 1) * tq, cols] = (acc[rows, :HEAD_DIM] * gate * inv).astype(o_ref.dtype)

    blocks = range(nqb)
    parts = [key_parts(t) for t in blocks]
    qs = [prep(t) for t in blocks]
    ss = {0: logits(qs[0], parts[0])}
    ms = {}
    accs = {}
    for t in blocks:
        if t + 1 < nqb:
            ss[t + 1] = logits(qs[t + 1], parts[t + 1])
        ms[t] = row_max(ss[t])
        if t >= 1:
            accs[t - 1] = weighted(ss[t - 1], ms[t - 1], parts[t - 1])
    accs[nqb - 1] = weighted(ss[nqb - 1], ms[nqb - 1], parts[nqb - 1])
    for t in blocks:
        finish(t, accs[t], ms[t])


def _window_bias(tq):
    r = (jnp.arange(GROUP * tq) % tq)[:, None]
    c = jnp.arange(3 * WINDOW)[None, :]
    band = (c >= r) & (c <= r + 2 * WINDOW)
    keep = jnp.stack([band & (c >= WINDOW), band, band & (c < 2 * WINDOW)])
    return jnp.where(keep, 0.0, NEG).astype(F32)


def _attn(u3, q_off, k_off, v_off, g_off, kv_heads, qnorm, knorm, *, tq, nqb, ctx=None, rope=None,
          window=False, sink=None, emit_kv=False):
    bsz, seq, _ = u3.shape
    gw = GROUP * HEAD_DIM
    heads = kv_heads * GROUP
    ctx_len = 0 if ctx is None else ctx[0].shape[1]
    pad = WINDOW if window else 0
    assert not window or (tq == WINDOW and seq // tq >= 2)
    assert seq % (tq * nqb) == 0

    def wide(off):
        return pl.BlockSpec((None, tq * nqb, gw), lambda b, h, i: (b, i, off // gw + h))

    def narrow(off, rows):
        return pl.BlockSpec((None, rows, HEAD_DIM), lambda b, h, i: (b, 0, off // HEAD_DIM + h))

    def whole(shape):
        return pl.BlockSpec(shape, lambda b, h, i: (0,) * len(shape))

    in_specs = [wide(q_off), wide(g_off), narrow(k_off, seq), narrow(v_off, seq)]
    args = [u3, u3, u3, u3]
    if ctx is not None:
        for cache in ctx:
            in_specs.append(narrow(0, ctx_len))
            args.append(cache.reshape(bsz, ctx_len, kv_heads * HEAD_DIM))
    if rope is not None:
        for t in rope:
            in_specs.append(whole(t.shape))
            args.append(t)
    if window:
        in_specs.append(whole((3, GROUP * tq, 3 * WINDOW)))
        args.append(_window_bias(tq))
    in_specs += [whole((2, HEAD_DIM)), whole((2, HEAD_DIM))]
    args += [jnp.stack([qnorm, qnorm[_ROT_PARTNER]]), jnp.stack([knorm, knorm[_ROT_PARTNER]])]
    if sink is not None:
        in_specs.append(pl.BlockSpec(memory_space=pltpu.SMEM))
        args.append(sink)
    out_specs = [pl.BlockSpec((None, tq * nqb, gw), lambda b, h, i: (b, i, h))]
    out_shape = [jax.ShapeDtypeStruct((bsz, seq, heads * HEAD_DIM), BF16)]
    if emit_kv:
        for _ in range(2):
            out_specs.append(narrow(0, seq))
            out_shape.append(jax.ShapeDtypeStruct((bsz, seq, kv_heads * HEAD_DIM), F32))
    rows = seq + 2 * pad + ctx_len
    kern = functools.partial(_attn_kernel, seq=seq, tq=tq, nqb=nqb, ctx_len=ctx_len,
                             rope=rope is not None, window=window, sink=sink is not None,
                             emit_kv=emit_kv)
    return pl.pallas_call(
        kern,
        grid=(bsz, kv_heads, seq // (tq * nqb)),
        in_specs=in_specs,
        out_specs=out_specs,
        out_shape=out_shape,
        scratch_shapes=[pltpu.VMEM((rows, HEAD_DIM), BF16),
                        pltpu.VMEM((rows, 2 * HEAD_DIM), BF16)]
        + ([pltpu.VMEM((seq, HEAD_DIM), F32)] * 2 if rope is not None else []),
        compiler_params=_params("parallel", "parallel", "arbitrary"),
        name="attn",
    )(*args)


def _rope_tables(n_tokens):
    rows = n_tokens // GRID_W
    row = jnp.repeat(jnp.arange(rows), GRID_W).astype(F32)
    col = (jnp.arange(rows * GRID_W) % GRID_W).astype(F32)
    inv = ROPE_THETA ** (-jnp.arange(ROPE_QUARTER, dtype=F32) / ROPE_QUARTER)
    ar = row[:, None] * inv
    ac = col[:, None] * inv
    ang = jnp.concatenate([ar, ar, ac, ac], axis=-1)
    rot = np.zeros((HEAD_DIM, HEAD_DIM), np.float32)
    rot[_ROT_PARTNER, np.arange(HEAD_DIM)] = np.where(_ROT_FIRST, -1.0, 1.0)
    return jnp.cos(ang), jnp.sin(ang), jnp.asarray(rot, BF16)


def _tile(m, pref):
    t = pref
    while m % t:
        t //= 2
    return t


def kernel(x_prompt, x_sample, state_l0_hgrn, cache_l0_k, cache_l0_v, cache_l1_k, cache_l1_v, c, c_ctx, lb_gamma, l0_norm, l0_w_mod, l0_b_mod, l0_w_in, l0_w_out, l0_a_onorm, l0_b_qnorm, l0_b_knorm, l1_norm, l1_w_mod, l1_b_mod, l1_w_in, l1_w_out, l1_c_qnorm, l1_c_knorm, l1_c_sink):
    pb, pl_, d = x_prompt.shape
    sb, sl, _ = x_sample.shape
    aw = A_HEADS * HEAD_DIM
    bkv = B_KV_HEADS * HEAD_DIM
    bw = B_KV_HEADS * GROUP * HEAD_DIM
    ckv = C_KV_HEADS * HEAD_DIM
    cw = C_KV_HEADS * GROUP * HEAD_DIM

    lb = jnp.cumsum(jax.nn.softmax(lb_gamma.astype(F32), axis=0), axis=0)[0]
    rope = _rope_tables(sl)

    nrow = -(-(sb + 1) // 8) * 8
    cond = jnp.zeros((nrow, d), F32).at[:sb].set(c).at[sb].set(c_ctx)
    xs = (x_prompt.reshape(pb * pl_, d), x_sample.reshape(sb * sl, d))
    tms = (_tile(pb * pl_, 1024), _tile(sl, 1024))
    rows_per_mod = (pb * pl_, sl)

    def mods(w_mod, b_mod):
        m = _adaln(cond, w_mod, b_mod)
        parts = [m[:, i * d:(i + 1) * d] for i in range(3)]
        return ([p[sb:sb + 1, None, :] for p in parts], [p[:sb, None, :] for p in parts])

    def tn_for(n):
        for t in (1280, 1024, 768, 512, 256, 128):
            if n % t == 0:
                return t
        return n

    mod_p, mod_s = mods(l0_w_mod, l0_b_mod)
    w_in = l0_w_in.astype(BF16)
    w_out = l0_w_out.astype(BF16)
    tn = tn_for(w_in.shape[1])
    u_p = _inproj(xs[0], l0_norm, mod_p[0], mod_p[1], w_in, rows_per_mod[0], tms[0], tn).reshape(pb, pl_, -1)
    u_s = _inproj(xs[1], l0_norm, mod_s[0], mod_s[1], w_in, rows_per_mod[1], tms[1], tn).reshape(sb, sl, -1)

    oa_p, new_state = _hgrn(u_p, lb, l0_a_onorm, None, hb=A_HEADS, unroll=1, emit_state=True)
    (oa_s,) = _hgrn(u_s, lb, l0_a_onorm, state_l0_hgrn, hb=4, unroll=2, emit_state=False)

    q_off = 5 * aw
    k_off = q_off + bw
    v_off = k_off + bkv
    g_off = v_off + bkv
    ob_p, k0, v0 = _attn(u_p, q_off, k_off, v_off, g_off, B_KV_HEADS, l0_b_qnorm, l0_b_knorm,
                         tq=128, nqb=2, emit_kv=True)
    (ob_s,) = _attn(u_s, q_off, k_off, v_off, g_off, B_KV_HEADS, l0_b_qnorm, l0_b_knorm,
                    tq=128, nqb=2, ctx=(cache_l0_k, cache_l0_v), rope=rope)

    otm = (_tile(pb * pl_, 512), _tile(sl, 512))
    y_p = _outproj([oa_p.reshape(pb * pl_, aw), ob_p.reshape(pb * pl_, bw)], w_out, xs[0], mod_p[2],
                   rows_per_mod[0], otm[0])
    y_s = _outproj([oa_s.reshape(sb * sl, aw), ob_s.reshape(sb * sl, bw)], w_out, xs[1], mod_s[2],
                   rows_per_mod[1], otm[1])

    mod_p, mod_s = mods(l1_w_mod, l1_b_mod)
    w_in = l1_w_in.astype(BF16)
    w_out = l1_w_out.astype(BF16)
    tn = tn_for(w_in.shape[1])
    u_p = _inproj(y_p, l1_norm, mod_p[0], mod_p[1], w_in, rows_per_mod[0], tms[0], tn).reshape(pb, pl_, -1)
    u_s = _inproj(y_s, l1_norm, mod_s[0], mod_s[1], w_in, rows_per_mod[1], tms[1], tn).reshape(sb, sl, -1)

    k_off = cw
    v_off = k_off + ckv
    g_off = v_off + ckv
    oc_p, k1, v1 = _attn(u_p, 0, k_off, v_off, g_off, C_KV_HEADS, l1_c_qnorm, l1_c_knorm,
                         tq=128, nqb=2, sink=l1_c_sink, emit_kv=True)
    (oc_s,) = _attn(u_s, 0, k_off, v_off, g_off, C_KV_HEADS, l1_c_qnorm, l1_c_knorm,
                    tq=WINDOW, nqb=4, ctx=(cache_l1_k, cache_l1_v), rope=rope, window=True, sink=l1_c_sink)

    z_p = _outproj([oc_p.reshape(pb * pl_, cw)], w_out, y_p, mod_p[2], rows_per_mod[0], otm[0])
    z_s = _outproj([oc_s.reshape(sb * sl, cw)], w_out, y_s, mod_s[2], rows_per_mod[1], otm[1])

    return (z_p.reshape(pb, pl_, d), z_s.reshape(sb, sl, d), new_state,
            k0.reshape(pb, pl_, B_KV_HEADS, HEAD_DIM), v0.reshape(pb, pl_, B_KV_HEADS, HEAD_DIM),
            k1.reshape(pb, pl_, C_KV_HEADS, HEAD_DIM), v1.reshape(pb, pl_, C_KV_HEADS, HEAD_DIM))
```

```python
import functools
import math

import jax
import jax.numpy as jnp
import numpy as np
from jax import lax
from jax.experimental import pallas as pl
from jax.experimental.pallas import tpu as pltpu

F32 = jnp.float32
BF16 = jnp.bfloat16

HEAD_DIM = 128
GRID_W = 64
ROPE_QUARTER = HEAD_DIM // 4
ROPE_THETA = 10000.0
NORM_EPS = 1e-6
A_HEADS = 8
A_CHUNK = 32
B_KV_HEADS = 2
C_KV_HEADS = 4
GROUP = 4
WINDOW = 128
NEG = -1e30
LOG2E = math.log2(math.e)
VMEM_LIMIT = 56 * 1024 * 1024

_ROT_FIRST = (np.arange(HEAD_DIM) % (2 * ROPE_QUARTER)) < ROPE_QUARTER
_ROT_PARTNER = np.where(_ROT_FIRST, np.arange(HEAD_DIM) + ROPE_QUARTER, np.arange(HEAD_DIM) - ROPE_QUARTER)

_NT = (((1,), (1,)), ((), ()))
_TN = (((0,), (0,)), ((), ()))


def _params(*sem):
    return pltpu.CompilerParams(dimension_semantics=sem, vmem_limit_bytes=VMEM_LIMIT)


def _silu(x):
    return x * jax.nn.sigmoid(x)


def _rms(x, g):
    ms = jnp.mean(x * x, axis=-1, keepdims=True)
    return x * lax.rsqrt(ms + NORM_EPS) * g


def _adaln_kernel(c_ref, w_ref, b_ref, o_ref):
    a = _silu(c_ref[...]).astype(BF16)
    o_ref[...] = jnp.dot(a, w_ref[...].astype(BF16), preferred_element_type=F32) + b_ref[...]


def _adaln(cond, w_mod, b_mod):
    r, d = cond.shape
    n = w_mod.shape[1]
    tn = _tile(n, 512)
    return pl.pallas_call(
        _adaln_kernel,
        grid=(n // tn,),
        in_specs=[pl.BlockSpec((r, d), lambda j: (0, 0)),
                  pl.BlockSpec((d, tn), lambda j: (0, j)),
                  pl.BlockSpec((1, tn), lambda j: (0, j))],
        out_specs=pl.BlockSpec((r, tn), lambda j: (0, j)),
        out_shape=jax.ShapeDtypeStruct((r, n), F32),
        compiler_params=_params("parallel"),
        name="adaln",
    )(cond, w_mod, b_mod.reshape(1, n))


def _inproj_kernel(x_ref, g_ref, sh_ref, sc_ref, w_ref, o_ref, h_ref, *, rc):
    @pl.when(pl.program_id(1) == 0)
    def _():
        g = g_ref[...]
        mul = 1.0 + sc_ref[...]
        sh = sh_ref[...]

        def body(r, carry):
            rows = pl.ds(pl.multiple_of(r * rc, rc), rc)
            h_ref[rows, :] = (_rms(x_ref[rows, :], g) * mul + sh).astype(BF16)
            return carry

        lax.fori_loop(0, x_ref.shape[0] // rc, body, 0, unroll=2)

    o_ref[...] = jnp.dot(h_ref[...], w_ref[...], preferred_element_type=F32).astype(o_ref.dtype)


def _inproj(x2d, norm_g, shift, scale, w, rows_per_mod, tm, tn):
    m, d = x2d.shape
    n = w.shape[1]
    mod_spec = pl.BlockSpec((None, 1, d), lambda i, j: ((i * tm) // rows_per_mod, 0, 0))
    return pl.pallas_call(
        functools.partial(_inproj_kernel, rc=32),
        grid=(m // tm, n // tn),
        in_specs=[pl.BlockSpec((tm, d), lambda i, j: (i, 0)),
                  pl.BlockSpec((1, d), lambda i, j: (0, 0)),
                  mod_spec, mod_spec,
                  pl.BlockSpec((d, tn), lambda i, j: (0, j))],
        out_specs=pl.BlockSpec((tm, tn), lambda i, j: (i, j)),
        out_shape=jax.ShapeDtypeStruct((m, n), BF16),
        scratch_shapes=[pltpu.VMEM((tm, d), BF16)],
        compiler_params=_params("parallel", "arbitrary"),
        name="inproj",
    )(x2d, norm_g.reshape(1, d), shift, scale, w)


def _outproj_kernel(*refs, widths):
    o_refs = refs[:len(widths)]
    w_ref, x_ref, gt_ref, y_ref = refs[len(widths):]
    acc = None
    start = 0
    for o_ref, wd in zip(o_refs, widths):
        part = jnp.dot(o_ref[...], w_ref[start:start + wd, :], preferred_element_type=F32)
        acc = part if acc is None else acc + part
        start += wd
    y_ref[...] = x_ref[...] + gt_ref[...] * acc


def _outproj(os, w, x2d, gate, rows_per_mod, tm):
    m, d = x2d.shape
    widths = tuple(o.shape[1] for o in os)
    k = w.shape[0]
    return pl.pallas_call(
        functools.partial(_outproj_kernel, widths=widths),
        grid=(m // tm,),
        in_specs=[pl.BlockSpec((tm, wd), lambda i: (i, 0)) for wd in widths] + [
            pl.BlockSpec((k, d), lambda i: (0, 0)),
            pl.BlockSpec((tm, d), lambda i: (i, 0)),
            pl.BlockSpec((None, 1, d), lambda i: ((i * tm) // rows_per_mod, 0, 0))],
        out_specs=pl.BlockSpec((tm, d), lambda i: (i, 0)),
        out_shape=jax.ShapeDtypeStruct((m, d), F32),
        compiler_params=_params("parallel"),
        name="outproj",
    )(*os, w, x2d, gate)


SUBLANES = 8
HGRN_BLOCK = 128


def _chunk_cumprod(f, reverse):
    n = f.shape[0]
    tiles = n // SUBLANES
    per_chunk = A_CHUNK // SUBLANES
    x = f.reshape(tiles, SUBLANES, HEAD_DIM)
    sub = lax.broadcasted_iota(jnp.int32, x.shape, 1)
    s = 1
    while s < SUBLANES:
        if reverse:
            x = x * jnp.where(sub < SUBLANES - s, pltpu.roll(x, SUBLANES - s, axis=1), 1.0)
        else:
            x = x * jnp.where(sub >= s, pltpu.roll(x, s, axis=1), 1.0)
        s *= 2
    edge = 0 if reverse else SUBLANES - 1
    out, tot = [], []
    for c in range(n // A_CHUNK):
        ts = [x[c * per_chunk + i] for i in range(per_chunk)]
        order = range(per_chunk - 1, -1, -1) if reverse else range(per_chunk)
        carry = None
        done = {}
        for i in order:
            t = ts[i] if carry is None else ts[i] * carry
            done[i] = t
            carry = t[edge:edge + 1, :]
        out += [done[i] for i in range(per_chunk)]
        tot += [jnp.broadcast_to(carry, (A_CHUNK, HEAD_DIM))]
    return jnp.concatenate(out, axis=0), jnp.concatenate(tot, axis=0)


def _hgrn_kernel(*refs, seq, hb, unroll, has_s0, emit_state):
    q_ref, ff_ref, fb_ref, v_ref, g_ref, lb_ref, on_ref = refs[:7]
    pos = 7
    s0_ref = None
    if has_s0:
        s0_ref = refs[pos]
        pos += 1
    o_ref = refs[pos]
    pos += 1
    so_ref = None
    if emit_state:
        so_ref = refs[pos]
        pos += 1
    oacc_ref, st_ref, qd_ref, ke_ref, dec_ref = refs[pos:]

    nc = seq // A_CHUNK
    rb = HGRN_BLOCK
    cpb = rb // A_CHUNK
    ri = lax.broadcasted_iota(jnp.int32, (rb, rb), 0)
    ci = lax.broadcasted_iota(jnp.int32, (rb, rb), 1)
    shift = A_CHUNK.bit_length() - 1
    same_chunk = jnp.right_shift(ri, shift) == jnp.right_shift(ci, shift)
    masks = (same_chunk & (ri >= ci), same_chunk & (ci >= ri))

    for j in range(hb):
        for d in range(2):
            if has_s0:
                st_ref[j, d] = s0_ref[d, j].T
            else:
                st_ref[j, d] = jnp.zeros((HEAD_DIM, HEAD_DIM), F32)

    def stage1(blk, carry):
        rows = pl.ds(pl.multiple_of(blk * rb, rb), rb)
        heads = range(hb)
        chains = [(j, d) for j in heads for d in range(2)]
        cols = [slice(j * HEAD_DIM, (j + 1) * HEAD_DIM) for j in heads]
        qs = [_silu(q_ref[rows, cols[j]].astype(F32)) for j in heads]
        ts, fs, scans, qds, kinvs, atts = {}, {}, {}, {}, {}, {}
        for j, d in chains:
            fr = (fb_ref if d else ff_ref)[rows, cols[j]].astype(F32)
            lb = lb_ref[d:d + 1, cols[j]]
            ts[j, d] = (1.0 - lb) * jax.nn.sigmoid(fr)
            fs[j, d] = lb + ts[j, d]
        for j, d in chains:
            scans[j, d] = _chunk_cumprod(fs[j, d], reverse=bool(d))
        for j, d in chains:
            eb, dec = scans[j, d]
            lb = lb_ref[d:d + 1, cols[j]]
            qds[j, d] = (qs[j] * eb).astype(BF16)
            kinvs[j, d] = ((1.0 - lb) - ts[j, d]) * (1.0 / eb)
            qd_ref[d, j, rows, :] = qds[j, d]
            ke_ref[d, j, rows, :] = (kinvs[j, d] * dec).astype(BF16)
            for c in range(cpb):
                dec_ref[d, j, pl.ds(blk * cpb + c, 1), :] = dec[c * A_CHUNK:c * A_CHUNK + 1, :]
        for j, d in chains:
            att = lax.dot_general(qds[j, d], kinvs[j, d].astype(BF16), _NT, preferred_element_type=F32)
            atts[j, d] = jnp.where(masks[d], att, 0.0).astype(BF16)
        for j in heads:
            v = v_ref[rows, cols[j]]
            oacc_ref[rows, cols[j]] = (jnp.dot(atts[j, 0], v, preferred_element_type=F32)
                                       + jnp.dot(atts[j, 1], v, preferred_element_type=F32))
        return carry

    lax.fori_loop(0, seq // rb, stage1, 0)

    def stage2(n, carry):
        for j in range(hb):
            cols = slice(j * HEAD_DIM, (j + 1) * HEAD_DIM)
            for d in range(2):
                c = nc - 1 - n if d else n
                rows = pl.ds(pl.multiple_of(c * A_CHUNK, A_CHUNK), A_CHUNK)
                st = st_ref[j, d]
                o = lax.dot_general(qd_ref[d, j, rows, :], st.astype(BF16), _NT, preferred_element_type=F32)
                upd = lax.dot_general(v_ref[rows, cols], ke_ref[d, j, rows, :], _TN, preferred_element_type=F32)
                st_ref[j, d] = st * dec_ref[d, j, pl.ds(c, 1), :] + upd
                oacc_ref[rows, cols] += o
        return carry

    lax.fori_loop(0, nc, stage2, 0, unroll=unroll)

    rc = 128
    on = on_ref[...]

    def epilogue(r, carry):
        rows = pl.ds(pl.multiple_of(r * rc, rc), rc)
        for j in range(hb):
            cols = slice(j * HEAD_DIM, (j + 1) * HEAD_DIM)
            y = _rms(oacc_ref[rows, cols], on)
            o_ref[rows, cols] = (y * _silu(g_ref[rows, cols].astype(F32))).astype(o_ref.dtype)
        return carry

    lax.fori_loop(0, seq // rc, epilogue, 0)

    if emit_state:
        for j in range(hb):
            for d in range(2):
                so_ref[d, j] = st_ref[j, d].T


def _hgrn(u3, lb, onorm, s0, hb, unroll, emit_state):
    bsz, seq, _ = u3.shape
    bw = hb * HEAD_DIM
    nh = A_HEADS // hb

    def seg(s):
        return pl.BlockSpec((None, seq, bw), lambda b, h: (b, 0, s * nh + h))

    in_specs = [seg(0), seg(1), seg(2), seg(3), seg(4),
                pl.BlockSpec((2, bw), lambda b, h: (0, h)),
                pl.BlockSpec((1, HEAD_DIM), lambda b, h: (0, 0))]
    args = [u3, u3, u3, u3, u3, lb, onorm.reshape(1, HEAD_DIM)]
    st_spec = pl.BlockSpec((None, 2, hb, HEAD_DIM, HEAD_DIM), lambda b, h: (b, 0, h, 0, 0))
    if s0 is not None:
        in_specs.append(st_spec)
        args.append(s0)
    out_specs = [pl.BlockSpec((None, seq, bw), lambda b, h: (b, 0, h))]
    out_shape = [jax.ShapeDtypeStruct((bsz, seq, A_HEADS * HEAD_DIM), BF16)]
    if emit_state:
        out_specs.append(st_spec)
        out_shape.append(jax.ShapeDtypeStruct((bsz, 2, A_HEADS, HEAD_DIM, HEAD_DIM), F32))
    return pl.pallas_call(
        functools.partial(_hgrn_kernel, seq=seq, hb=hb, unroll=unroll, has_s0=s0 is not None,
                          emit_state=emit_state),
        grid=(bsz, nh),
        in_specs=in_specs,
        out_specs=out_specs,
        out_shape=out_shape,
        scratch_shapes=[pltpu.VMEM((seq, bw), F32),
                        pltpu.VMEM((hb, 2, HEAD_DIM, HEAD_DIM), F32),
                        pltpu.VMEM((2, hb, seq, HEAD_DIM), BF16),
                        pltpu.VMEM((2, hb, seq, HEAD_DIM), BF16),
                        pltpu.VMEM((2, hb, seq // A_CHUNK, HEAD_DIM), F32)],
        compiler_params=_params("parallel", "parallel"),
        name="hgrn",
    )(*args)


def _norm_rope(xb, g_cos, gp_sin, rot_ref, out_scale):
    x = xb.astype(F32)
    sq = x * x
    hi = sq.astype(BF16)
    lo = (sq - hi.astype(F32)).astype(BF16)
    mean_mat = jnp.full((HEAD_DIM, HEAD_DIM), 1.0 / HEAD_DIM, BF16)
    ms = (jnp.dot(hi, mean_mat, preferred_element_type=F32)
          + jnp.dot(lo, mean_mat, preferred_element_type=F32))
    r = lax.rsqrt(ms + NORM_EPS)
    if out_scale != 1.0:
        r = r * out_scale
    y = x * g_cos
    if gp_sin is not None:
        y = y + jnp.dot(xb, rot_ref[...], preferred_element_type=F32) * gp_sin
    return y * r


def _attn_kernel(*refs, seq, tq, nqb, ctx_len, rope, window, sink, emit_kv):
    refs = list(refs)
    q_ref, gate_ref, k_ref, v_ref = refs[:4]
    pos = 4
    kc_ref = vc_ref = cos_ref = sin_ref = rot_ref = bias_ref = sink_ref = ko_ref = vo_ref = None
    if ctx_len:
        kc_ref, vc_ref = refs[pos:pos + 2]
        pos += 2
    if rope:
        cos_ref, sin_ref, rot_ref = refs[pos:pos + 3]
        pos += 3
    if window:
        bias_ref = refs[pos]
        pos += 1
    qn_ref, kn_ref = refs[pos:pos + 2]
    pos += 2
    if sink:
        sink_ref = refs[pos]
        pos += 1
    o_ref = refs[pos]
    pos += 1
    if emit_kv:
        ko_ref, vo_ref = refs[pos:pos + 2]
        pos += 2
    ks_ref, vs_ref = refs[pos:pos + 2]
    pos += 2
    qc_ref = qs_ref = None
    if rope:
        qc_ref, qs_ref = refs[pos:]

    kvh = pl.program_id(1)
    step = pl.program_id(2)
    nq = seq // tq
    pad = WINDOW if window else 0
    ctx_off = seq + 2 * pad
    qscale = LOG2E / math.sqrt(HEAD_DIM)

    @pl.when(step == 0)
    def _():
        if window:
            zeros = jnp.zeros((pad, HEAD_DIM), BF16)
            for ref in (ks_ref, vs_ref):
                ref[0:pad, 0:HEAD_DIM] = zeros
                ref[pad + seq:ctx_off, 0:HEAD_DIM] = zeros
        rc = 256
        kg = kn_ref[0:1, :]
        kgp = kn_ref[1:2, :]
        ones = jnp.ones((rc, HEAD_DIM), BF16)

        def body(r, carry):
            rows = pl.ds(pl.multiple_of(r * rc, rc), rc)
            dst = pl.ds(pl.multiple_of(pad + r * rc, HEAD_DIM), rc)
            if rope:
                cos = cos_ref[rows, :]
                sin = sin_ref[rows, :]
                qc_ref[rows, :] = cos * qn_ref[0:1, :]
                qs_ref[rows, :] = sin * qn_ref[1:2, :]
                k = _norm_rope(k_ref[rows, :], cos * kg, sin * kgp, rot_ref, 1.0)
            else:
                k = _norm_rope(k_ref[rows, :], kg, None, None, 1.0)
            if emit_kv:
                ko_ref[rows, :] = k
                vo_ref[rows, :] = v_ref[rows, :].astype(F32)
            ks_ref[dst, :] = k.astype(BF16)
            vs_ref[dst, 0:HEAD_DIM] = v_ref[rows, :]
            return carry

        lax.fori_loop(0, seq // rc, body, 0)
        if ctx_len:
            ks_ref[ctx_off:ctx_off + ctx_len, :] = kc_ref[...].astype(BF16)
            vs_ref[ctx_off:ctx_off + ctx_len, 0:HEAD_DIM] = vc_ref[...].astype(BF16)

        def fill(r, carry):
            vs_ref[pl.ds(pl.multiple_of(r * rc, rc), rc), HEAD_DIM:2 * HEAD_DIM] = ones
            return carry

        lax.fori_loop(0, vs_ref.shape[0] // rc, fill, 0)

    sink_col = None
    if sink:
        sink_col = jnp.concatenate(
            [jnp.full((tq, 1), sink_ref[kvh * GROUP + h] * LOG2E, F32) for h in range(GROUP)], axis=0)

    def prep(t):
        qi = step * nqb + t
        qrows = pl.ds(pl.multiple_of(qi * tq, tq), tq)
        xs = []
        for h in range(GROUP):
            xb = q_ref[t * tq:(t + 1) * tq, h * HEAD_DIM:(h + 1) * HEAD_DIM]
            if rope:
                x = _norm_rope(xb, qc_ref[qrows, :], qs_ref[qrows, :], rot_ref, qscale)
            else:
                x = _norm_rope(xb, qn_ref[0:1, :], None, None, qscale)
            xs.append(x.astype(BF16))
        return jnp.concatenate(xs, axis=0)

    def key_parts(t):
        qi = step * nqb + t
        if not window:
            return [(slice(None), None)]
        start = pl.multiple_of(qi * tq, HEAD_DIM)
        edge = jnp.where(qi == 0, 0, jnp.where(qi == nq - 1, 2, 1))
        parts = [(pl.ds(start, 3 * WINDOW), bias_ref[edge])]
        if ctx_len:
            parts.append((slice(ctx_off, ctx_off + ctx_len), None))
        return parts

    def logits(q, parts):
        ss = []
        for rows, bias in parts:
            s = lax.dot_general(q, ks_ref[rows, :], _NT, preferred_element_type=F32)
            ss.append(s if bias is None else s + bias)
        return ss

    def row_max(ss):
        slabs = [s[:, c:c + HEAD_DIM] for s in ss for c in range(0, s.shape[1], HEAD_DIM)]
        return jnp.max(functools.reduce(jnp.maximum, slabs), axis=-1, keepdims=True)

    def weighted(ss, m, parts):
        acc = None
        for s, (rows, _) in zip(ss, parts):
            pv = jnp.dot(jnp.exp2(s - m).astype(BF16), vs_ref[rows, :], preferred_element_type=F32)
            acc = pv if acc is None else acc + pv
        return acc

    def finish(t, acc, m):
        l = acc[:, HEAD_DIM:]
        if sink:
            l = l + jnp.exp2(sink_col - m)
        for h in range(GROUP):
            cols = slice(h * HEAD_DIM, (h + 1) * HEAD_DIM)
            rows = slice(h * tq, (h + 1) * tq)
            gate = gate_ref[t * tq:(t + 1) * tq, cols].astype(F32)
            inv = 1.0 / (l[rows, :] * (1.0 + jnp.exp(-gate)))
            o_ref[t * tq:(t + 1) * tq, cols] = (acc[rows, :HEAD_DIM] * gate * inv).astype(o_ref.dtype)

    blocks = range(nqb)
    parts = [key_parts(t) for t in blocks]
    qs = [prep(t) for t in blocks]
    ss = {0: logits(qs[0], parts[0])}
    ms = {}
    accs = {}
    for t in blocks:
        if t + 1 < nqb:
            ss[t + 1] = logits(qs[t + 1], parts[t + 1])
        ms[t] = row_max(ss[t])
        if t >= 1:
            accs[t - 1] = weighted(ss[t - 1], ms[t - 1], parts[t - 1])
    accs[nqb - 1] = weighted(ss[nqb - 1], ms[nqb - 1], parts[nqb - 1])
    for t in blocks:
        finish(t, accs[t], ms[t])


def _window_bias(tq):
    r = (jnp.arange(GROUP * tq) % tq)[:, None]
    c = jnp.arange(3 * WINDOW)[None, :]
    band = (c >= r) & (c <= r + 2 * WINDOW)
    keep = jnp.stack([band & (c >= WINDOW), band, band & (c < 2 * WINDOW)])
    return jnp.where(keep, 0.0, NEG).astype(F32)


def _attn(u3, q_off, k_off, v_off, g_off, kv_heads, qnorm, knorm, *, tq, nqb, ctx=None, rope=None,
          window=False, sink=None, emit_kv=False):
    bsz, seq, _ = u3.shape
    gw = GROUP * HEAD_DIM
    heads = kv_heads * GROUP
    ctx_len = 0 if ctx is None else ctx[0].shape[1]
    pad = WINDOW if window else 0
    assert not window or (tq == WINDOW and seq // tq >= 2)
    assert seq % (tq * nqb) == 0

    def wide(off):
        return pl.BlockSpec((None, tq * nqb, gw), lambda b, h, i: (b, i, off // gw + h))

    def narrow(off, rows):
        return pl.BlockSpec((None, rows, HEAD_DIM), lambda b, h, i: (b, 0, off // HEAD_DIM + h))

    def whole(shape):
        return pl.BlockSpec(shape, lambda b, h, i: (0,) * len(shape))

    in_specs = [wide(q_off), wide(g_off), narrow(k_off, seq), narrow(v_off, seq)]
    args = [u3, u3, u3, u3]
    if ctx is not None:
        for cache in ctx:
            in_specs.append(narrow(0, ctx_len))
            args.append(cache.reshape(bsz, ctx_len, kv_heads * HEAD_DIM))
    if rope is not None:
        for t in rope:
            in_specs.append(whole(t.shape))
            args.append(t)
    if window:
        in_specs.append(whole((3, GROUP * tq, 3 * WINDOW)))
        args.append(_window_bias(tq))
    in_specs += [whole((2, HEAD_DIM)), whole((2, HEAD_DIM))]
    args += [jnp.stack([qnorm, qnorm[_ROT_PARTNER]]), jnp.stack([knorm, knorm[_ROT_PARTNER]])]
    if sink is not None:
        in_specs.append(pl.BlockSpec(memory_space=pltpu.SMEM))
        args.append(sink)
    out_specs = [pl.BlockSpec((None, tq * nqb, gw), lambda b, h, i: (b, i, h))]
    out_shape = [jax.ShapeDtypeStruct((bsz, seq, heads * HEAD_DIM), BF16)]
    if emit_kv:
        for _ in range(2):
            out_specs.append(narrow(0, seq))
            out_shape.append(jax.ShapeDtypeStruct((bsz, seq, kv_heads * HEAD_DIM), F32))
    rows = seq + 2 * pad + ctx_len
    kern = functools.partial(_attn_kernel, seq=seq, tq=tq, nqb=nqb, ctx_len=ctx_len,
                             rope=rope is not None, window=window, sink=sink is not None,
                             emit_kv=emit_kv)
    return pl.pallas_call(
        kern,
        grid=(bsz, kv_heads, seq // (tq * nqb)),
        in_specs=in_specs,
        out_specs=out_specs,
        out_shape=out_shape,
        scratch_shapes=[pltpu.VMEM((rows, HEAD_DIM), BF16),
                        pltpu.VMEM((rows, 2 * HEAD_DIM), BF16)]
        + ([pltpu.VMEM((seq, HEAD_DIM), F32)] * 2 if rope is not None else []),
        compiler_params=_params("parallel", "parallel", "arbitrary"),
        name="attn",
    )(*args)


def _rope_tables(n_tokens):
    rows = n_tokens // GRID_W
    row = jnp.repeat(jnp.arange(rows), GRID_W).astype(F32)
    col = (jnp.arange(rows * GRID_W) % GRID_W).astype(F32)
    inv = ROPE_THETA ** (-jnp.arange(ROPE_QUARTER, dtype=F32) / ROPE_QUARTER)
    ar = row[:, None] * inv
    ac = col[:, None] * inv
    ang = jnp.concatenate([ar, ar, ac, ac], axis=-1)
    rot = np.zeros((HEAD_DIM, HEAD_DIM), np.float32)
    rot[_ROT_PARTNER, np.arange(HEAD_DIM)] = np.where(_ROT_FIRST, -1.0, 1.0)
    return jnp.cos(ang), jnp.sin(ang), jnp.asarray(rot, BF16)


def _tile(m, pref):
    t = pref
    while m % t:
        t //= 2
    return t


def kernel(x_prompt, x_sample, state_l0_hgrn, cache_l0_k, cache_l0_v, cache_l1_k, cache_l1_v, c, c_ctx, lb_gamma, l0_norm, l0_w_mod, l0_b_mod, l0_w_in, l0_w_out, l0_a_onorm, l0_b_qnorm, l0_b_knorm, l1_norm, l1_w_mod, l1_b_mod, l1_w_in, l1_w_out, l1_c_qnorm, l1_c_knorm, l1_c_sink):
    pb, pl_, d = x_prompt.shape
    sb, sl, _ = x_sample.shape
    aw = A_HEADS * HEAD_DIM
    bkv = B_KV_HEADS * HEAD_DIM
    bw = B_KV_HEADS * GROUP * HEAD_DIM
    ckv = C_KV_HEADS * HEAD_DIM
    cw = C_KV_HEADS * GROUP * HEAD_DIM

    lb = jnp.cumsum(jax.nn.softmax(lb_gamma.astype(F32), axis=0), axis=0)[0]
    rope = _rope_tables(sl)

    nrow = -(-(sb + 1) // 8) * 8
    cond = jnp.zeros((nrow, d), F32).at[:sb].set(c).at[sb].set(c_ctx)
    xs = (x_prompt.reshape(pb * pl_, d), x_sample.reshape(sb * sl, d))
    tms = (_tile(pb * pl_, 1024), _tile(sl, 1024))
    rows_per_mod = (pb * pl_, sl)

    def mods(w_mod, b_mod):
        m = _adaln(cond, w_mod, b_mod)
        parts = [m[:, i * d:(i + 1) * d] for i in range(3)]
        return ([p[sb:sb + 1, None, :] for p in parts], [p[:sb, None, :] for p in parts])

    def tn_for(n):
        for t in (1280, 1024, 768, 512, 256, 128):
            if n % t == 0:
                return t
        return n

    mod_p, mod_s = mods(l0_w_mod, l0_b_mod)
    w_in = l0_w_in.astype(BF16)
    w_out = l0_w_out.astype(BF16)
    tn = tn_for(w_in.shape[1])
    u_p = _inproj(xs[0], l0_norm, mod_p[0], mod_p[1], w_in, rows_per_mod[0], tms[0], tn).reshape(pb, pl_, -1)
    u_s = _inproj(xs[1], l0_norm, mod_s[0], mod_s[1], w_in, rows_per_mod[1], tms[1], tn).reshape(sb, sl, -1)

    oa_p, new_state = _hgrn(u_p, lb, l0_a_onorm, None, hb=A_HEADS, unroll=2, emit_state=True)
    (oa_s,) = _hgrn(u_s, lb, l0_a_onorm, state_l0_hgrn, hb=4, unroll=4, emit_state=False)

    q_off = 5 * aw
    k_off = q_off + bw
    v_off = k_off + bkv
    g_off = v_off + bkv
    ob_p, k0, v0 = _attn(u_p, q_off, k_off, v_off, g_off, B_KV_HEADS, l0_b_qnorm, l0_b_knorm,
                         tq=128, nqb=2, emit_kv=True)
    (ob_s,) = _attn(u_s, q_off, k_off, v_off, g_off, B_KV_HEADS, l0_b_qnorm, l0_b_knorm,
                    tq=128, nqb=2, ctx=(cache_l0_k, cache_l0_v), rope=rope)

    otm = (_tile(pb * pl_, 512), _tile(sl, 512))
    y_p = _outproj([oa_p.reshape(pb * pl_, aw), ob_p.reshape(pb * pl_, bw)], w_out, xs[0], mod_p[2],
                   rows_per_mod[0], otm[0])
    y_s = _outproj([oa_s.reshape(sb * sl, aw), ob_s.reshape(sb * sl, bw)], w_out, xs[1], mod_s[2],
                   rows_per_mod[1], otm[1])

    mod_p, mod_s = mods(l1_w_mod, l1_b_mod)
    w_in = l1_w_in.astype(BF16)
    w_out = l1_w_out.astype(BF16)
    tn = tn_for(w_in.shape[1])
    u_p = _inproj(y_p, l1_norm, mod_p[0], mod_p[1], w_in, rows_per_mod[0], tms[0], tn).reshape(pb, pl_, -1)
    u_s = _inproj(y_s, l1_norm, mod_s[0], mod_s[1], w_in, rows_per_mod[1], tms[1], tn).reshape(sb, sl, -1)

    k_off = cw
    v_off = k_off + ckv
    g_off = v_off + ckv
    oc_p, k1, v1 = _attn(u_p, 0, k_off, v_off, g_off, C_KV_HEADS, l1_c_qnorm, l1_c_knorm,
                         tq=128, nqb=2, sink=l1_c_sink, emit_kv=True)
    (oc_s,) = _attn(u_s, 0, k_off, v_off, g_off, C_KV_HEADS, l1_c_qnorm, l1_c_knorm,
                    tq=WINDOW, nqb=4, ctx=(cache_l1_k, cache_l1_v), rope=rope, window=True, sink=l1_c_sink)

    z_p = _outproj([oc_p.reshape(pb * pl_, cw)], w_out, y_p, mod_p[2], rows_per_mod[0], otm[0])
    z_s = _outproj([oc_s.reshape(sb * sl, cw)], w_out, y_s, mod_s[2], rows_per_mod[1], otm[1])

    return (z_p.reshape(pb, pl_, d), z_s.reshape(sb, sl, d), new_state,
            k0.reshape(pb, pl_, B_KV_HEADS, HEAD_DIM), v0.reshape(pb, pl_, B_KV_HEADS, HEAD_DIM),
            k1.reshape(pb, pl_, C_KV_HEADS, HEAD_DIM), v1.reshape(pb, pl_, C_KV_HEADS, HEAD_DIM))
```

```python
import functools
import math

import jax
import jax.numpy as jnp
import numpy as np
from jax import lax
from jax.experimental import pallas as pl
from jax.experimental.pallas import tpu as pltpu

F32 = jnp.float32
BF16 = jnp.bfloat16

HEAD_DIM = 128
GRID_W = 64
ROPE_QUARTER = HEAD_DIM // 4
ROPE_THETA = 10000.0
NORM_EPS = 1e-6
A_HEADS = 8
A_CHUNK = 32
B_KV_HEADS = 2
C_KV_HEADS = 4
GROUP = 4
WINDOW = 128
NEG = -1e30
LOG2E = math.log2(math.e)
VMEM_LIMIT = 56 * 1024 * 1024

_ROT_FIRST = (np.arange(HEAD_DIM) % (2 * ROPE_QUARTER)) < ROPE_QUARTER
_ROT_PARTNER = np.where(_ROT_FIRST, np.arange(HEAD_DIM) + ROPE_QUARTER, np.arange(HEAD_DIM) - ROPE_QUARTER)

_NT = (((1,), (1,)), ((), ()))
_TN = (((0,), (0,)), ((), ()))


def _params(*sem):
    return pltpu.CompilerParams(dimension_semantics=sem, vmem_limit_bytes=VMEM_LIMIT)


def _silu(x):
    return x * jax.nn.sigmoid(x)


def _rms(x, g):
    ms = jnp.mean(x * x, axis=-1, keepdims=True)
    return x * lax.rsqrt(ms + NORM_EPS) * g


def _adaln_kernel(c_ref, w_ref, b_ref, o_ref):
    a = _silu(c_ref[...]).astype(BF16)
    o_ref[...] = jnp.dot(a, w_ref[...].astype(BF16), preferred_element_type=F32) + b_ref[...]


def _adaln(cond, w_mod, b_mod):
    r, d = cond.shape
    n = w_mod.shape[1]
    tn = _tile(n, 512)
    return pl.pallas_call(
        _adaln_kernel,
        grid=(n // tn,),
        in_specs=[pl.BlockSpec((r, d), lambda j: (0, 0)),
                  pl.BlockSpec((d, tn), lambda j: (0, j)),
                  pl.BlockSpec((1, tn), lambda j: (0, j))],
        out_specs=pl.BlockSpec((r, tn), lambda j: (0, j)),
        out_shape=jax.ShapeDtypeStruct((r, n), F32),
        compiler_params=_params("parallel"),
        name="adaln",
    )(cond, w_mod, b_mod.reshape(1, n))


def _inproj_kernel(x_ref, g_ref, sh_ref, sc_ref, w_ref, o_ref, h_ref, *, rc):
    @pl.when(pl.program_id(1) == 0)
    def _():
        g = g_ref[...]
        mul = 1.0 + sc_ref[...]
        sh = sh_ref[...]

        def body(r, carry):
            rows = pl.ds(pl.multiple_of(r * rc, rc), rc)
            h_ref[rows, :] = (_rms(x_ref[rows, :], g) * mul + sh).astype(BF16)
            return carry

        lax.fori_loop(0, x_ref.shape[0] // rc, body, 0, unroll=2)

    o_ref[...] = jnp.dot(h_ref[...], w_ref[...], preferred_element_type=F32).astype(o_ref.dtype)


def _inproj(x2d, norm_g, shift, scale, w, rows_per_mod, tm, tn):
    m, d = x2d.shape
    n = w.shape[1]
    mod_spec = pl.BlockSpec((None, 1, d), lambda i, j: ((i * tm) // rows_per_mod, 0, 0))
    return pl.pallas_call(
        functools.partial(_inproj_kernel, rc=32),
        grid=(m // tm, n // tn),
        in_specs=[pl.BlockSpec((tm, d), lambda i, j: (i, 0)),
                  pl.BlockSpec((1, d), lambda i, j: (0, 0)),
                  mod_spec, mod_spec,
                  pl.BlockSpec((d, tn), lambda i, j: (0, j))],
        out_specs=pl.BlockSpec((tm, tn), lambda i, j: (i, j)),
        out_shape=jax.ShapeDtypeStruct((m, n), BF16),
        scratch_shapes=[pltpu.VMEM((tm, d), BF16)],
        compiler_params=_params("parallel", "arbitrary"),
        name="inproj",
    )(x2d, norm_g.reshape(1, d), shift, scale, w)


def _outproj_kernel(*refs, widths):
    o_refs = refs[:len(widths)]
    w_ref, x_ref, gt_ref, y_ref = refs[len(widths):]
    acc = None
    start = 0
    for o_ref, wd in zip(o_refs, widths):
        part = jnp.dot(o_ref[...], w_ref[start:start + wd, :], preferred_element_type=F32)
        acc = part if acc is None else acc + part
        start += wd
    y_ref[...] = x_ref[...] + gt_ref[...] * acc


def _outproj(os, w, x2d, gate, rows_per_mod, tm):
    m, d = x2d.shape
    widths = tuple(o.shape[1] for o in os)
    k = w.shape[0]
    return pl.pallas_call(
        functools.partial(_outproj_kernel, widths=widths),
        grid=(m // tm,),
        in_specs=[pl.BlockSpec((tm, wd), lambda i: (i, 0)) for wd in widths] + [
            pl.BlockSpec((k, d), lambda i: (0, 0)),
            pl.BlockSpec((tm, d), lambda i: (i, 0)),
            pl.BlockSpec((None, 1, d), lambda i: ((i * tm) // rows_per_mod, 0, 0))],
        out_specs=pl.BlockSpec((tm, d), lambda i: (i, 0)),
        out_shape=jax.ShapeDtypeStruct((m, d), F32),
        compiler_params=_params("parallel"),
        name="outproj",
    )(*os, w, x2d, gate)


SUBLANES = 8
HGRN_BLOCK = 128


def _chunk_cumprod(f, reverse):
    n = f.shape[0]
    tiles = n // SUBLANES
    per_chunk = A_CHUNK // SUBLANES
    x = f.reshape(tiles, SUBLANES, HEAD_DIM)
    sub = lax.broadcasted_iota(jnp.int32, x.shape, 1)
    s = 1
    while s < SUBLANES:
        if reverse:
            x = x * jnp.where(sub < SUBLANES - s, pltpu.roll(x, SUBLANES - s, axis=1), 1.0)
        else:
            x = x * jnp.where(sub >= s, pltpu.roll(x, s, axis=1), 1.0)
        s *= 2
    edge = 0 if reverse else SUBLANES - 1
    out, tot = [], []
    for c in range(n // A_CHUNK):
        ts = [x[c * per_chunk + i] for i in range(per_chunk)]
        order = range(per_chunk - 1, -1, -1) if reverse else range(per_chunk)
        carry = None
        done = {}
        for i in order:
            t = ts[i] if carry is None else ts[i] * carry
            done[i] = t
            carry = t[edge:edge + 1, :]
        out += [done[i] for i in range(per_chunk)]
        tot += [jnp.broadcast_to(carry, (A_CHUNK, HEAD_DIM))]
    return jnp.concatenate(out, axis=0), jnp.concatenate(tot, axis=0)


def _hgrn_kernel(*refs, seq, hb, unroll, has_s0, emit_state):
    q_ref, ff_ref, fb_ref, v_ref, g_ref, lb_ref, on_ref = refs[:7]
    pos = 7
    s0_ref = None
    if has_s0:
        s0_ref = refs[pos]
        pos += 1
    o_ref = refs[pos]
    pos += 1
    so_ref = None
    if emit_state:
        so_ref = refs[pos]
        pos += 1
    oacc_ref, st_ref, qd_ref, ke_ref, dec_ref = refs[pos:]

    nc = seq // A_CHUNK
    rb = HGRN_BLOCK
    cpb = rb // A_CHUNK
    ri = lax.broadcasted_iota(jnp.int32, (rb, rb), 0)
    ci = lax.broadcasted_iota(jnp.int32, (rb, rb), 1)
    shift = A_CHUNK.bit_length() - 1
    same_chunk = jnp.right_shift(ri, shift) == jnp.right_shift(ci, shift)
    masks = (same_chunk & (ri >= ci), same_chunk & (ci >= ri))

    for j in range(hb):
        for d in range(2):
            lanes = slice(d * HEAD_DIM, (d + 1) * HEAD_DIM)
            if has_s0:
                st_ref[j, :, lanes] = s0_ref[d, j].T
            else:
                st_ref[j, :, lanes] = jnp.zeros((HEAD_DIM, HEAD_DIM), F32)

    def stage1(blk, carry):
        rows = pl.ds(pl.multiple_of(blk * rb, rb), rb)
        heads = range(hb)
        chains = [(j, d) for j in heads for d in range(2)]
        cols = [slice(j * HEAD_DIM, (j + 1) * HEAD_DIM) for j in heads]
        qs = [_silu(q_ref[rows, cols[j]].astype(F32)) for j in heads]
        ts, fs, scans, qds, kinvs, atts = {}, {}, {}, {}, {}, {}
        for j, d in chains:
            fr = (fb_ref if d else ff_ref)[rows, cols[j]].astype(F32)
            lb = lb_ref[d:d + 1, cols[j]]
            ts[j, d] = (1.0 - lb) * jax.nn.sigmoid(fr)
            fs[j, d] = lb + ts[j, d]
        for j, d in chains:
            scans[j, d] = _chunk_cumprod(fs[j, d], reverse=bool(d))
        for j, d in chains:
            eb, dec = scans[j, d]
            lb = lb_ref[d:d + 1, cols[j]]
            qds[j, d] = (qs[j] * eb).astype(BF16)
            kinvs[j, d] = ((1.0 - lb) - ts[j, d]) * (1.0 / eb)
            qd_ref[d, j, rows, :] = qds[j, d]
            ke_ref[d, j, rows, :] = (kinvs[j, d] * dec).astype(BF16)
            for c in range(cpb):
                dec_ref[d, j, pl.ds(blk * cpb + c, 1), :] = dec[c * A_CHUNK:c * A_CHUNK + 1, :]
        for j, d in chains:
            att = lax.dot_general(qds[j, d], kinvs[j, d].astype(BF16), _NT, preferred_element_type=F32)
            atts[j, d] = jnp.where(masks[d], att, 0.0).astype(BF16)
        for j in heads:
            v = v_ref[rows, cols[j]]
            oacc_ref[rows, cols[j]] = (jnp.dot(atts[j, 0], v, preferred_element_type=F32)
                                       + jnp.dot(atts[j, 1], v, preferred_element_type=F32))
        return carry

    lax.fori_loop(0, seq // rb, stage1, 0)

    zero = jnp.zeros((A_CHUNK, HEAD_DIM), BF16)

    def blockdiag(a, b):
        return jnp.concatenate([jnp.concatenate([a, zero], axis=1), jnp.concatenate([zero, b], axis=1)], axis=0)

    def stage2(n, carry):
        cf, cb = n, nc - 1 - n
        rows_f = pl.ds(pl.multiple_of(cf * A_CHUNK, A_CHUNK), A_CHUNK)
        rows_b = pl.ds(pl.multiple_of(cb * A_CHUNK, A_CHUNK), A_CHUNK)
        for j in range(hb):
            cols = slice(j * HEAD_DIM, (j + 1) * HEAD_DIM)
            st = st_ref[j]
            qd = blockdiag(qd_ref[0, j, rows_f, :], qd_ref[1, j, rows_b, :])
            ke = blockdiag(ke_ref[0, j, rows_f, :], ke_ref[1, j, rows_b, :])
            v = jnp.concatenate([v_ref[rows_f, cols], v_ref[rows_b, cols]], axis=0)
            dec = jnp.concatenate([dec_ref[0, j, pl.ds(cf, 1), :], dec_ref[1, j, pl.ds(cb, 1), :]], axis=1)
            o = lax.dot_general(qd, st.astype(BF16), _NT, preferred_element_type=F32)
            st_ref[j] = st * dec + lax.dot_general(v, ke, _TN, preferred_element_type=F32)
            oacc_ref[rows_f, cols] += o[:A_CHUNK, :]
            oacc_ref[rows_b, cols] += o[A_CHUNK:, :]
        return carry

    lax.fori_loop(0, nc, stage2, 0, unroll=unroll)

    rc = 128
    on = on_ref[...]

    def epilogue(r, carry):
        rows = pl.ds(pl.multiple_of(r * rc, rc), rc)
        for j in range(hb):
            cols = slice(j * HEAD_DIM, (j + 1) * HEAD_DIM)
            y = _rms(oacc_ref[rows, cols], on)
            o_ref[rows, cols] = (y * _silu(g_ref[rows, cols].astype(F32))).astype(o_ref.dtype)
        return carry

    lax.fori_loop(0, seq // rc, epilogue, 0)

    if emit_state:
        for j in range(hb):
            for d in range(2):
                so_ref[d, j] = st_ref[j, :, d * HEAD_DIM:(d + 1) * HEAD_DIM].T


def _hgrn(u3, lb, onorm, s0, hb, unroll, emit_state):
    bsz, seq, _ = u3.shape
    bw = hb * HEAD_DIM
    nh = A_HEADS // hb

    def seg(s):
        return pl.BlockSpec((None, seq, bw), lambda b, h: (b, 0, s * nh + h))

    in_specs = [seg(0), seg(1), seg(2), seg(3), seg(4),
                pl.BlockSpec((2, bw), lambda b, h: (0, h)),
                pl.BlockSpec((1, HEAD_DIM), lambda b, h: (0, 0))]
    args = [u3, u3, u3, u3, u3, lb, onorm.reshape(1, HEAD_DIM)]
    st_spec = pl.BlockSpec((None, 2, hb, HEAD_DIM, HEAD_DIM), lambda b, h: (b, 0, h, 0, 0))
    if s0 is not None:
        in_specs.append(st_spec)
        args.append(s0)
    out_specs = [pl.BlockSpec((None, seq, bw), lambda b, h: (b, 0, h))]
    out_shape = [jax.ShapeDtypeStruct((bsz, seq, A_HEADS * HEAD_DIM), BF16)]
    if emit_state:
        out_specs.append(st_spec)
        out_shape.append(jax.ShapeDtypeStruct((bsz, 2, A_HEADS, HEAD_DIM, HEAD_DIM), F32))
    return pl.pallas_call(
        functools.partial(_hgrn_kernel, seq=seq, hb=hb, unroll=unroll, has_s0=s0 is not None,
                          emit_state=emit_state),
        grid=(bsz, nh),
        in_specs=in_specs,
        out_specs=out_specs,
        out_shape=out_shape,
        scratch_shapes=[pltpu.VMEM((seq, bw), F32),
                        pltpu.VMEM((hb, HEAD_DIM, 2 * HEAD_DIM), F32),
                        pltpu.VMEM((2, hb, seq, HEAD_DIM), BF16),
                        pltpu.VMEM((2, hb, seq, HEAD_DIM), BF16),
                        pltpu.VMEM((2, hb, seq // A_CHUNK, HEAD_DIM), F32)],
        compiler_params=_params("parallel", "parallel"),
        name="hgrn",
    )(*args)


def _norm_rope(xb, g_cos, gp_sin, rot_ref, out_scale):
    x = xb.astype(F32)
    sq = (x * x).astype(BF16)
    y = x * g_cos
    if gp_sin is not None:
        both = jnp.dot(jnp.concatenate([xb, sq], axis=1), rot_ref[...], preferred_element_type=F32)
        y = y + both[:, :HEAD_DIM] * gp_sin
        ms = both[:, HEAD_DIM:]
    else:
        mean_mat = jnp.full((HEAD_DIM, HEAD_DIM), 1.0 / HEAD_DIM, BF16)
        ms = jnp.dot(sq, mean_mat, preferred_element_type=F32)
    r = lax.rsqrt(ms + NORM_EPS)
    if out_scale != 1.0:
        r = r * out_scale
    return y * r


def _attn_kernel(*refs, seq, tq, nqb, ctx_len, rope, window, sink, emit_kv):
    refs = list(refs)
    q_ref, gate_ref, k_ref, v_ref = refs[:4]
    pos = 4
    kc_ref = vc_ref = cos_ref = sin_ref = rot_ref = bias_ref = sink_ref = ko_ref = vo_ref = None
    if ctx_len:
        kc_ref, vc_ref = refs[pos:pos + 2]
        pos += 2
    if rope:
        cos_ref, sin_ref, rot_ref = refs[pos:pos + 3]
        pos += 3
    if window:
        bias_ref = refs[pos]
        pos += 1
    qn_ref, kn_ref = refs[pos:pos + 2]
    pos += 2
    if sink:
        sink_ref = refs[pos]
        pos += 1
    o_ref = refs[pos]
    pos += 1
    if emit_kv:
        ko_ref, vo_ref = refs[pos:pos + 2]
        pos += 2
    ks_ref, vs_ref = refs[pos:pos + 2]
    pos += 2
    qc_ref = qs_ref = None
    if rope:
        qc_ref, qs_ref = refs[pos:]

    kvh = pl.program_id(1)
    step = pl.program_id(2)
    nq = seq // tq
    pad = WINDOW if window else 0
    ctx_off = seq + 2 * pad
    qscale = LOG2E / math.sqrt(HEAD_DIM)

    @pl.when(step == 0)
    def _():
        if window:
            zeros = jnp.zeros((pad, HEAD_DIM), BF16)
            for ref in (ks_ref, vs_ref):
                ref[0:pad, 0:HEAD_DIM] = zeros
                ref[pad + seq:ctx_off, 0:HEAD_DIM] = zeros
        rc = 256
        kg = kn_ref[0:1, :]
        kgp = kn_ref[1:2, :]
        ones = jnp.ones((rc, HEAD_DIM), BF16)

        def body(r, carry):
            rows = pl.ds(pl.multiple_of(r * rc, rc), rc)
            dst = pl.ds(pl.multiple_of(pad + r * rc, HEAD_DIM), rc)
            if rope:
                cos = cos_ref[rows, :]
                sin = sin_ref[rows, :]
                qc_ref[rows, :] = cos * qn_ref[0:1, :]
                qs_ref[rows, :] = sin * qn_ref[1:2, :]
                k = _norm_rope(k_ref[rows, :], cos * kg, sin * kgp, rot_ref, 1.0)
            else:
                k = _norm_rope(k_ref[rows, :], kg, None, None, 1.0)
            if emit_kv:
                ko_ref[rows, :] = k
                vo_ref[rows, :] = v_ref[rows, :].astype(F32)
            ks_ref[dst, :] = k.astype(BF16)
            vs_ref[dst, 0:HEAD_DIM] = v_ref[rows, :]
            return carry

        lax.fori_loop(0, seq // rc, body, 0)
        if ctx_len:
            ks_ref[ctx_off:ctx_off + ctx_len, :] = kc_ref[...].astype(BF16)
            vs_ref[ctx_off:ctx_off + ctx_len, 0:HEAD_DIM] = vc_ref[...].astype(BF16)

        def fill(r, carry):
            vs_ref[pl.ds(pl.multiple_of(r * rc, rc), rc), HEAD_DIM:2 * HEAD_DIM] = ones
            return carry

        lax.fori_loop(0, vs_ref.shape[0] // rc, fill, 0)

    sink_col = None
    if sink:
        sink_col = jnp.concatenate(
            [jnp.full((tq, 1), sink_ref[kvh * GROUP + h] * LOG2E, F32) for h in range(GROUP)], axis=0)

    def prep(t):
        qi = step * nqb + t
        qrows = pl.ds(pl.multiple_of(qi * tq, tq), tq)
        xs = []
        for h in range(GROUP):
            xb = q_ref[t * tq:(t + 1) * tq, h * HEAD_DIM:(h + 1) * HEAD_DIM]
            if rope:
                x = _norm_rope(xb, qc_ref[qrows, :], qs_ref[qrows, :], rot_ref, qscale)
            else:
                x = _norm_rope(xb, qn_ref[0:1, :], None, None, qscale)
            xs.append(x.astype(BF16))
        return jnp.concatenate(xs, axis=0)

    def key_parts(t):
        qi = step * nqb + t
        if not window:
            return [(slice(None), None)]
        start = pl.multiple_of(qi * tq, HEAD_DIM)
        edge = jnp.where(qi == 0, 0, jnp.where(qi == nq - 1, 2, 1))
        parts = [(pl.ds(start, 3 * WINDOW), bias_ref[edge])]
        if ctx_len:
            parts.append((slice(ctx_off, ctx_off + ctx_len), None))
        return parts

    def logits(q, parts):
        ss = []
        for rows, bias in parts:
            s = lax.dot_general(q, ks_ref[rows, :], _NT, preferred_element_type=F32)
            ss.append(s if bias is None else s + bias)
        return ss

    def row_max(ss):
        slabs = [s[:, c:c + HEAD_DIM] for s in ss for c in range(0, s.shape[1], HEAD_DIM)]
        return jnp.max(functools.reduce(jnp.maximum, slabs), axis=-1, keepdims=True)

    def weighted(ss, m, parts):
        acc = None
        for s, (rows, _) in zip(ss, parts):
            pv = jnp.dot(jnp.exp2(s - m).astype(BF16), vs_ref[rows, :], preferred_element_type=F32)
            acc = pv if acc is None else acc + pv
        return acc

    def finish(t, acc, m):
        l = acc[:, HEAD_DIM:]
        if sink:
            l = l + jnp.exp2(sink_col - m)
        for h in range(GROUP):
            cols = slice(h * HEAD_DIM, (h + 1) * HEAD_DIM)
            rows = slice(h * tq, (h + 1) * tq)
            gate = gate_ref[t * tq:(t + 1) * tq, cols].astype(F32)
            inv = 1.0 / (l[rows, :] * (1.0 + jnp.exp(-gate)))
            o_ref[t * tq:(t + 1) * tq, cols] = (acc[rows, :HEAD_DIM] * gate * inv).astype(o_ref.dtype)

    blocks = range(nqb)
    parts = [key_parts(t) for t in blocks]
    qs = [prep(t) for t in blocks]
    ss = {0: logits(qs[0], parts[0])}
    ms = {}
    accs = {}
    for t in blocks:
        if t + 1 < nqb:
            ss[t + 1] = logits(qs[t + 1], parts[t + 1])
        ms[t] = row_max(ss[t])
        if t >= 1:
            accs[t - 1] = weighted(ss[t - 1], ms[t - 1], parts[t - 1])
    accs[nqb - 1] = weighted(ss[nqb - 1], ms[nqb - 1], parts[nqb - 1])
    for t in blocks:
        finish(t, accs[t], ms[t])


def _window_bias(tq):
    r = (jnp.arange(GROUP * tq) % tq)[:, None]
    c = jnp.arange(3 * WINDOW)[None, :]
    band = (c >= r) & (c <= r + 2 * WINDOW)
    keep = jnp.stack([band & (c >= WINDOW), band, band & (c < 2 * WINDOW)])
    return jnp.where(keep, 0.0, NEG).astype(F32)


def _attn(u3, q_off, k_off, v_off, g_off, kv_heads, qnorm, knorm, *, tq, nqb, ctx=None, rope=None,
          window=False, sink=None, emit_kv=False):
    bsz, seq, _ = u3.shape
    gw = GROUP * HEAD_DIM
    heads = kv_heads * GROUP
    ctx_len = 0 if ctx is None else ctx[0].shape[1]
    pad = WINDOW if window else 0
    assert not window or (tq == WINDOW and seq // tq >= 2)
    assert seq % (tq * nqb) == 0

    def wide(off):
        return pl.BlockSpec((None, tq * nqb, gw), lambda b, h, i: (b, i, off // gw + h))

    def narrow(off, rows):
        return pl.BlockSpec((None, rows, HEAD_DIM), lambda b, h, i: (b, 0, off // HEAD_DIM + h))

    def whole(shape):
        return pl.BlockSpec(shape, lambda b, h, i: (0,) * len(shape))

    in_specs = [wide(q_off), wide(g_off), narrow(k_off, seq), narrow(v_off, seq)]
    args = [u3, u3, u3, u3]
    if ctx is not None:
        for cache in ctx:
            in_specs.append(narrow(0, ctx_len))
            args.append(cache.reshape(bsz, ctx_len, kv_heads * HEAD_DIM))
    if rope is not None:
        for t in rope:
            in_specs.append(whole(t.shape))
            args.append(t)
    if window:
        in_specs.append(whole((3, GROUP * tq, 3 * WINDOW)))
        args.append(_window_bias(tq))
    in_specs += [whole((2, HEAD_DIM)), whole((2, HEAD_DIM))]
    args += [jnp.stack([qnorm, qnorm[_ROT_PARTNER]]), jnp.stack([knorm, knorm[_ROT_PARTNER]])]
    if sink is not None:
        in_specs.append(pl.BlockSpec(memory_space=pltpu.SMEM))
        args.append(sink)
    out_specs = [pl.BlockSpec((None, tq * nqb, gw), lambda b, h, i: (b, i, h))]
    out_shape = [jax.ShapeDtypeStruct((bsz, seq, heads * HEAD_DIM), BF16)]
    if emit_kv:
        for _ in range(2):
            out_specs.append(narrow(0, seq))
            out_shape.append(jax.ShapeDtypeStruct((bsz, seq, kv_heads * HEAD_DIM), F32))
    rows = seq + 2 * pad + ctx_len
    kern = functools.partial(_attn_kernel, seq=seq, tq=tq, nqb=nqb, ctx_len=ctx_len,
                             rope=rope is not None, window=window, sink=sink is not None,
                             emit_kv=emit_kv)
    return pl.pallas_call(
        kern,
        grid=(bsz, kv_heads, seq // (tq * nqb)),
        in_specs=in_specs,
        out_specs=out_specs,
        out_shape=out_shape,
        scratch_shapes=[pltpu.VMEM((rows, HEAD_DIM), BF16),
                        pltpu.VMEM((rows, 2 * HEAD_DIM), BF16)]
        + ([pltpu.VMEM((seq, HEAD_DIM), F32)] * 2 if rope is not None else []),
        compiler_params=_params("parallel", "parallel", "arbitrary"),
        name="attn",
    )(*args)


def _rope_tables(n_tokens):
    rows = n_tokens // GRID_W
    row = jnp.repeat(jnp.arange(rows), GRID_W).astype(F32)
    col = (jnp.arange(rows * GRID_W) % GRID_W).astype(F32)
    inv = ROPE_THETA ** (-jnp.arange(ROPE_QUARTER, dtype=F32) / ROPE_QUARTER)
    ar = row[:, None] * inv
    ac = col[:, None] * inv
    ang = jnp.concatenate([ar, ar, ac, ac], axis=-1)
    rot = np.zeros((2 * HEAD_DIM, 2 * HEAD_DIM), np.float32)
    rot[_ROT_PARTNER, np.arange(HEAD_DIM)] = np.where(_ROT_FIRST, -1.0, 1.0)
    rot[HEAD_DIM:, HEAD_DIM:] = 1.0 / HEAD_DIM
    return jnp.cos(ang), jnp.sin(ang), jnp.asarray(rot, BF16)


def _tile(m, pref):
    t = pref
    while m % t:
        t //= 2
    return t


def kernel(x_prompt, x_sample, state_l0_hgrn, cache_l0_k, cache_l0_v, cache_l1_k, cache_l1_v, c, c_ctx, lb_gamma, l0_norm, l0_w_mod, l0_b_mod, l0_w_in, l0_w_out, l0_a_onorm, l0_b_qnorm, l0_b_knorm, l1_norm, l1_w_mod, l1_b_mod, l1_w_in, l1_w_out, l1_c_qnorm, l1_c_knorm, l1_c_sink):
    pb, pl_, d = x_prompt.shape
    sb, sl, _ = x_sample.shape
    aw = A_HEADS * HEAD_DIM
    bkv = B_KV_HEADS * HEAD_DIM
    bw = B_KV_HEADS * GROUP * HEAD_DIM
    ckv = C_KV_HEADS * HEAD_DIM
    cw = C_KV_HEADS * GROUP * HEAD_DIM

    lb = jnp.cumsum(jax.nn.softmax(lb_gamma.astype(F32), axis=0), axis=0)[0]
    rope = _rope_tables(sl)

    nrow = -(-(sb + 1) // 8) * 8
    cond = jnp.zeros((nrow, d), F32).at[:sb].set(c).at[sb].set(c_ctx)
    xs = (x_prompt.reshape(pb * pl_, d), x_sample.reshape(sb * sl, d))
    tms = (_tile(pb * pl_, 1024), _tile(sl, 1024))
    rows_per_mod = (pb * pl_, sl)

    def mods(w_mod, b_mod):
        m = _adaln(cond, w_mod, b_mod)
        parts = [m[:, i * d:(i + 1) * d] for i in range(3)]
        return ([p[sb:sb + 1, None, :] for p in parts], [p[:sb, None, :] for p in parts])

    def tn_for(n):
        for t in (1280, 1024, 768, 512, 256, 128):
            if n % t == 0:
                return t
        return n

    mod_p, mod_s = mods(l0_w_mod, l0_b_mod)
    w_in = l0_w_in.astype(BF16)
    w_out = l0_w_out.astype(BF16)
    tn = tn_for(w_in.shape[1])
    u_p = _inproj(xs[0], l0_norm, mod_p[0], mod_p[1], w_in, rows_per_mod[0], tms[0], tn).reshape(pb, pl_, -1)
    u_s = _inproj(xs[1], l0_norm, mod_s[0], mod_s[1], w_in, rows_per_mod[1], tms[1], tn).reshape(sb, sl, -1)

    oa_p, new_state = _hgrn(u_p, lb, l0_a_onorm, None, hb=A_HEADS, unroll=2, emit_state=True)
    (oa_s,) = _hgrn(u_s, lb, l0_a_onorm, state_l0_hgrn, hb=4, unroll=4, emit_state=False)

    q_off = 5 * aw
    k_off = q_off + bw
    v_off = k_off + bkv
    g_off = v_off + bkv
    ob_p, k0, v0 = _attn(u_p, q_off, k_off, v_off, g_off, B_KV_HEADS, l0_b_qnorm, l0_b_knorm,
                         tq=128, nqb=2, emit_kv=True)
    (ob_s,) = _attn(u_s, q_off, k_off, v_off, g_off, B_KV_HEADS, l0_b_qnorm, l0_b_knorm,
                    tq=128, nqb=_tile(sl // 128, 4), ctx=(cache_l0_k, cache_l0_v), rope=rope)

    otm = (_tile(pb * pl_, 512), _tile(sl, 512))
    y_p = _outproj([oa_p.reshape(pb * pl_, aw), ob_p.reshape(pb * pl_, bw)], w_out, xs[0], mod_p[2],
                   rows_per_mod[0], otm[0])
    y_s = _outproj([oa_s.reshape(sb * sl, aw), ob_s.reshape(sb * sl, bw)], w_out, xs[1], mod_s[2],
                   rows_per_mod[1], otm[1])

    mod_p, mod_s = mods(l1_w_mod, l1_b_mod)
    w_in = l1_w_in.astype(BF16)
    w_out = l1_w_out.astype(BF16)
    tn = tn_for(w_in.shape[1])
    u_p = _inproj(y_p, l1_norm, mod_p[0], mod_p[1], w_in, rows_per_mod[0], tms[0], tn).reshape(pb, pl_, -1)
    u_s = _inproj(y_s, l1_norm, mod_s[0], mod_s[1], w_in, rows_per_mod[1], tms[1], tn).reshape(sb, sl, -1)

    k_off = cw
    v_off = k_off + ckv
    g_off = v_off + ckv
    oc_p, k1, v1 = _attn(u_p, 0, k_off, v_off, g_off, C_KV_HEADS, l1_c_qnorm, l1_c_knorm,
                         tq=128, nqb=2, sink=l1_c_sink, emit_kv=True)
    (oc_s,) = _attn(u_s, 0, k_off, v_off, g_off, C_KV_HEADS, l1_c_qnorm, l1_c_knorm,
                    tq=WINDOW, nqb=_tile(sl // WINDOW, 8), ctx=(cache_l1_k, cache_l1_v), rope=rope, window=True, sink=l1_c_sink)

    z_p = _outproj([oc_p.reshape(pb * pl_, cw)], w_out, y_p, mod_p[2], rows_per_mod[0], otm[0])
    z_s = _outproj([oc_s.reshape(sb * sl, cw)], w_out, y_s, mod_s[2], rows_per_mod[1], otm[1])

    return (z_p.reshape(pb, pl_, d), z_s.reshape(sb, sl, d), new_state,
            k0.reshape(pb, pl_, B_KV_HEADS, HEAD_DIM), v0.reshape(pb, pl_, B_KV_HEADS, HEAD_DIM),
            k1.reshape(pb, pl_, C_KV_HEADS, HEAD_DIM), v1.reshape(pb, pl_, C_KV_HEADS, HEAD_DIM))
```

```python
import functools
import math

import jax
import jax.numpy as jnp
import numpy as np
from jax import lax
from jax.experimental import pallas as pl
from jax.experimental.pallas import tpu as pltpu

F32 = jnp.float32
BF16 = jnp.bfloat16

HEAD_DIM = 128
GRID_W = 64
ROPE_QUARTER = HEAD_DIM // 4
ROPE_THETA = 10000.0
NORM_EPS = 1e-6
A_HEADS = 8
A_CHUNK = 32
B_KV_HEADS = 2
C_KV_HEADS = 4
GROUP = 4
WINDOW = 128
NEG = -1e30
LOG2E = math.log2(math.e)
VMEM_LIMIT = 56 * 1024 * 1024

_ROT_FIRST = (np.arange(HEAD_DIM) % (2 * ROPE_QUARTER)) < ROPE_QUARTER
_ROT_PARTNER = np.where(_ROT_FIRST, np.arange(HEAD_DIM) + ROPE_QUARTER, np.arange(HEAD_DIM) - ROPE_QUARTER)

_NT = (((1,), (1,)), ((), ()))
_TN = (((0,), (0,)), ((), ()))


def _params(*sem):
    return pltpu.CompilerParams(dimension_semantics=sem, vmem_limit_bytes=VMEM_LIMIT)


def _silu(x):
    return x * jax.nn.sigmoid(x)


def _rms(x, g):
    ms = jnp.mean(x * x, axis=-1, keepdims=True)
    return x * lax.rsqrt(ms + NORM_EPS) * g


def _adaln_kernel(c_ref, w_ref, b_ref, o_ref):
    a = _silu(c_ref[...]).astype(BF16)
    o_ref[...] = jnp.dot(a, w_ref[...].astype(BF16), preferred_element_type=F32) + b_ref[...]


def _adaln(cond, w_mod, b_mod):
    r, d = cond.shape
    n = w_mod.shape[1]
    tn = _tile(n, 512)
    return pl.pallas_call(
        _adaln_kernel,
        grid=(n // tn,),
        in_specs=[pl.BlockSpec((r, d), lambda j: (0, 0)),
                  pl.BlockSpec((d, tn), lambda j: (0, j)),
                  pl.BlockSpec((1, tn), lambda j: (0, j))],
        out_specs=pl.BlockSpec((r, tn), lambda j: (0, j)),
        out_shape=jax.ShapeDtypeStruct((r, n), F32),
        compiler_params=_params("parallel"),
        name="adaln",
    )(cond, w_mod, b_mod.reshape(1, n))


def _inproj_kernel(x_ref, g_ref, sh_ref, sc_ref, w_ref, o_ref, h_ref, *, rc, n_tiles, norm_steps):
    i = pl.program_id(0)
    j = pl.program_id(1)
    slice_rows = x_ref.shape[0] // norm_steps

    def matmul():
        o_ref[...] = jnp.dot(h_ref[(i + 1) % 2], w_ref[...], preferred_element_type=F32).astype(o_ref.dtype)

    def norm_slice():
        g = g_ref[...]
        mul = 1.0 + sc_ref[...]
        sh = sh_ref[...]
        for r in range(slice_rows // rc):
            rows = pl.ds(pl.multiple_of(j * slice_rows + r * rc, rc), rc)
            h_ref[i % 2, rows, :] = (_rms(x_ref[rows, :], g) * mul + sh).astype(BF16)

    has_mm = i >= 1
    has_norm = (i < n_tiles) & (j < norm_steps)

    @pl.when(has_mm & has_norm)
    def _():
        matmul()
        norm_slice()

    @pl.when(has_mm & jnp.logical_not(has_norm))
    def _():
        matmul()

    @pl.when(jnp.logical_not(has_mm) & has_norm)
    def _():
        norm_slice()


def _inproj(x2d, norm_g, shift, scale, w, rows_per_mod, tm, tn):
    m, d = x2d.shape
    n = w.shape[1]
    n_tiles = m // tm
    n_cols = n // tn
    norm_steps = max(s for s in (1, 2, 4, 8) if s <= n_cols and (tm // s) % 32 == 0)

    def tile(i):
        return jnp.minimum(i, n_tiles - 1)

    def col(i, j):
        return jnp.where(i == 0, 0, j)

    mod_spec = pl.BlockSpec((None, 1, d), lambda i, j: ((tile(i) * tm) // rows_per_mod, 0, 0))
    return pl.pallas_call(
        functools.partial(_inproj_kernel, rc=32, n_tiles=n_tiles, norm_steps=norm_steps),
        grid=(n_tiles + 1, n_cols),
        in_specs=[pl.BlockSpec((tm, d), lambda i, j: (tile(i), 0)),
                  pl.BlockSpec((1, d), lambda i, j: (0, 0)),
                  mod_spec, mod_spec,
                  pl.BlockSpec((d, tn), lambda i, j: (0, col(i, j)))],
        out_specs=pl.BlockSpec((tm, tn), lambda i, j: (jnp.maximum(i - 1, 0), col(i, j))),
        out_shape=jax.ShapeDtypeStruct((m, n), BF16),
        scratch_shapes=[pltpu.VMEM((2, tm, d), BF16)],
        compiler_params=_params("arbitrary", "arbitrary"),
        name="inproj",
    )(x2d, norm_g.reshape(1, d), shift, scale, w)


def _outproj_kernel(*refs, widths):
    o_refs = refs[:len(widths)]
    w_ref, x_ref, gt_ref, y_ref = refs[len(widths):]
    acc = None
    start = 0
    for o_ref, wd in zip(o_refs, widths):
        part = jnp.dot(o_ref[...], w_ref[start:start + wd, :], preferred_element_type=F32)
        acc = part if acc is None else acc + part
        start += wd
    y_ref[...] = x_ref[...] + gt_ref[...] * acc


def _outproj(os, w, x2d, gate, rows_per_mod, tm):
    m, d = x2d.shape
    widths = tuple(o.shape[1] for o in os)
    k = w.shape[0]
    return pl.pallas_call(
        functools.partial(_outproj_kernel, widths=widths),
        grid=(m // tm,),
        in_specs=[pl.BlockSpec((tm, wd), lambda i: (i, 0)) for wd in widths] + [
            pl.BlockSpec((k, d), lambda i: (0, 0)),
            pl.BlockSpec((tm, d), lambda i: (i, 0)),
            pl.BlockSpec((None, 1, d), lambda i: ((i * tm) // rows_per_mod, 0, 0))],
        out_specs=pl.BlockSpec((tm, d), lambda i: (i, 0)),
        out_shape=jax.ShapeDtypeStruct((m, d), F32),
        compiler_params=_params("parallel"),
        name="outproj",
    )(*os, w, x2d, gate)


SUBLANES = 8
HGRN_BLOCK = 128


def _chunk_cumprod(f, reverse):
    n = f.shape[0]
    tiles = n // SUBLANES
    per_chunk = A_CHUNK // SUBLANES
    x = f.reshape(tiles, SUBLANES, HEAD_DIM)
    sub = lax.broadcasted_iota(jnp.int32, x.shape, 1)
    s = 1
    while s < SUBLANES:
        if reverse:
            x = x * jnp.where(sub < SUBLANES - s, pltpu.roll(x, SUBLANES - s, axis=1), 1.0)
        else:
            x = x * jnp.where(sub >= s, pltpu.roll(x, s, axis=1), 1.0)
        s *= 2
    edge = 0 if reverse else SUBLANES - 1
    out, tot = [], []
    for c in range(n // A_CHUNK):
        ts = [x[c * per_chunk + i] for i in range(per_chunk)]
        order = range(per_chunk - 1, -1, -1) if reverse else range(per_chunk)
        carry = None
        done = {}
        for i in order:
            t = ts[i] if carry is None else ts[i] * carry
            done[i] = t
            carry = t[edge:edge + 1, :]
        out += [done[i] for i in range(per_chunk)]
        tot += [jnp.broadcast_to(carry, (A_CHUNK, HEAD_DIM))]
    return jnp.concatenate(out, axis=0), jnp.concatenate(tot, axis=0)


def _hgrn_kernel(*refs, seq, hb, unroll, has_s0, emit_state):
    q_ref, ff_ref, fb_ref, v_ref, g_ref, lb_ref, on_ref = refs[:7]
    pos = 7
    s0_ref = None
    if has_s0:
        s0_ref = refs[pos]
        pos += 1
    o_ref = refs[pos]
    pos += 1
    so_ref = None
    if emit_state:
        so_ref = refs[pos]
        pos += 1
    oacc_ref, st_ref, qd_ref, ke_ref, dec_ref = refs[pos:]

    nc = seq // A_CHUNK
    rb = HGRN_BLOCK
    cpb = rb // A_CHUNK
    ri = lax.broadcasted_iota(jnp.int32, (rb, rb), 0)
    ci = lax.broadcasted_iota(jnp.int32, (rb, rb), 1)
    shift = A_CHUNK.bit_length() - 1
    same_chunk = jnp.right_shift(ri, shift) == jnp.right_shift(ci, shift)
    masks = (same_chunk & (ri >= ci), same_chunk & (ci >= ri))

    for j in range(hb):
        for d in range(2):
            lanes = slice(d * HEAD_DIM, (d + 1) * HEAD_DIM)
            if has_s0:
                st_ref[j, :, lanes] = s0_ref[d, j].T
            else:
                st_ref[j, :, lanes] = jnp.zeros((HEAD_DIM, HEAD_DIM), F32)

    def stage1(blk, carry):
        rows = pl.ds(pl.multiple_of(blk * rb, rb), rb)
        heads = range(hb)
        chains = [(j, d) for j in heads for d in range(2)]
        cols = [slice(j * HEAD_DIM, (j + 1) * HEAD_DIM) for j in heads]
        qs = [_silu(q_ref[rows, cols[j]].astype(F32)) for j in heads]
        ts, fs, scans, qds, kinvs, atts = {}, {}, {}, {}, {}, {}
        for j, d in chains:
            fr = (fb_ref if d else ff_ref)[rows, cols[j]].astype(F32)
            lb = lb_ref[d:d + 1, cols[j]]
            ts[j, d] = (1.0 - lb) * jax.nn.sigmoid(fr)
            fs[j, d] = lb + ts[j, d]
        for j, d in chains:
            scans[j, d] = _chunk_cumprod(fs[j, d], reverse=bool(d))
        for j, d in chains:
            eb, dec = scans[j, d]
            lb = lb_ref[d:d + 1, cols[j]]
            qds[j, d] = (qs[j] * eb).astype(BF16)
            kinvs[j, d] = ((1.0 - lb) - ts[j, d]) * (1.0 / eb)
            qd_ref[d, j, rows, :] = qds[j, d]
            ke_ref[d, j, rows, :] = (kinvs[j, d] * dec).astype(BF16)
            for c in range(cpb):
                dec_ref[d, j, pl.ds(blk * cpb + c, 1), :] = dec[c * A_CHUNK:c * A_CHUNK + 1, :]
        for j, d in chains:
            att = lax.dot_general(qds[j, d], kinvs[j, d].astype(BF16), _NT, preferred_element_type=F32)
            atts[j, d] = jnp.where(masks[d], att, 0.0).astype(BF16)
        for j in heads:
            v = v_ref[rows, cols[j]]
            oacc_ref[rows, cols[j]] = (jnp.dot(atts[j, 0], v, preferred_element_type=F32)
                                       + jnp.dot(atts[j, 1], v, preferred_element_type=F32))
        return carry

    lax.fori_loop(0, seq // rb, stage1, 0)

    zero = jnp.zeros((A_CHUNK, HEAD_DIM), BF16)

    def blockdiag(a, b):
        return jnp.concatenate([jnp.concatenate([a, zero], axis=1), jnp.concatenate([zero, b], axis=1)], axis=0)

    def stage2(n, carry):
        cf, cb = n, nc - 1 - n
        rows_f = pl.ds(pl.multiple_of(cf * A_CHUNK, A_CHUNK), A_CHUNK)
        rows_b = pl.ds(pl.multiple_of(cb * A_CHUNK, A_CHUNK), A_CHUNK)
        for j in range(hb):
            cols = slice(j * HEAD_DIM, (j + 1) * HEAD_DIM)
            st = st_ref[j]
            qd = blockdiag(qd_ref[0, j, rows_f, :], qd_ref[1, j, rows_b, :])
            ke = blockdiag(ke_ref[0, j, rows_f, :], ke_ref[1, j, rows_b, :])
            v = jnp.concatenate([v_ref[rows_f, cols], v_ref[rows_b, cols]], axis=0)
            dec = jnp.concatenate([dec_ref[0, j, pl.ds(cf, 1), :], dec_ref[1, j, pl.ds(cb, 1), :]], axis=1)
            o = lax.dot_general(qd, st.astype(BF16), _NT, preferred_element_type=F32)
            st_ref[j] = st * dec + lax.dot_general(v, ke, _TN, preferred_element_type=F32)
            oacc_ref[rows_f, cols] += o[:A_CHUNK, :]
            oacc_ref[rows_b, cols] += o[A_CHUNK:, :]
        return carry

    lax.fori_loop(0, nc, stage2, 0, unroll=unroll)

    rc = 128
    on = on_ref[...]

    def epilogue(r, carry):
        rows = pl.ds(pl.multiple_of(r * rc, rc), rc)
        for j in range(hb):
            cols = slice(j * HEAD_DIM, (j + 1) * HEAD_DIM)
            y = _rms(oacc_ref[rows, cols], on)
            o_ref[rows, cols] = (y * _silu(g_ref[rows, cols].astype(F32))).astype(o_ref.dtype)
        return carry

    lax.fori_loop(0, seq // rc, epilogue, 0)

    if emit_state:
        for j in range(hb):
            for d in range(2):
                so_ref[d, j] = st_ref[j, :, d * HEAD_DIM:(d + 1) * HEAD_DIM].T


def _hgrn(u3, lb, onorm, s0, hb, unroll, emit_state):
    bsz, seq, _ = u3.shape
    bw = hb * HEAD_DIM
    nh = A_HEADS // hb

    def seg(s):
        return pl.BlockSpec((None, seq, bw), lambda b, h: (b, 0, s * nh + h))

    in_specs = [seg(0), seg(1), seg(2), seg(3), seg(4),
                pl.BlockSpec((2, bw), lambda b, h: (0, h)),
                pl.BlockSpec((1, HEAD_DIM), lambda b, h: (0, 0))]
    args = [u3, u3, u3, u3, u3, lb, onorm.reshape(1, HEAD_DIM)]
    st_spec = pl.BlockSpec((None, 2, hb, HEAD_DIM, HEAD_DIM), lambda b, h: (b, 0, h, 0, 0))
    if s0 is not None:
        in_specs.append(st_spec)
        args.append(s0)
    out_specs = [pl.BlockSpec((None, seq, bw), lambda b, h: (b, 0, h))]
    out_shape = [jax.ShapeDtypeStruct((bsz, seq, A_HEADS * HEAD_DIM), BF16)]
    if emit_state:
        out_specs.append(st_spec)
        out_shape.append(jax.ShapeDtypeStruct((bsz, 2, A_HEADS, HEAD_DIM, HEAD_DIM), F32))
    return pl.pallas_call(
        functools.partial(_hgrn_kernel, seq=seq, hb=hb, unroll=unroll, has_s0=s0 is not None,
                          emit_state=emit_state),
        grid=(bsz, nh),
        in_specs=in_specs,
        out_specs=out_specs,
        out_shape=out_shape,
        scratch_shapes=[pltpu.VMEM((seq, bw), F32),
                        pltpu.VMEM((hb, HEAD_DIM, 2 * HEAD_DIM), F32),
                        pltpu.VMEM((2, hb, seq, HEAD_DIM), BF16),
                        pltpu.VMEM((2, hb, seq, HEAD_DIM), BF16),
                        pltpu.VMEM((2, hb, seq // A_CHUNK, HEAD_DIM), F32)],
        compiler_params=_params("parallel", "parallel"),
        name="hgrn",
    )(*args)


def _norm_rope(xb, g_cos, gp_sin, rot_ref, out_scale):
    x = xb.astype(F32)
    sq = (x * x).astype(BF16)
    y = x * g_cos
    if gp_sin is not None:
        both = jnp.dot(jnp.concatenate([xb, sq], axis=1), rot_ref[...], preferred_element_type=F32)
        y = y + both[:, :HEAD_DIM] * gp_sin
        ms = both[:, HEAD_DIM:]
    else:
        mean_mat = jnp.full((HEAD_DIM, HEAD_DIM), 1.0 / HEAD_DIM, BF16)
        ms = jnp.dot(sq, mean_mat, preferred_element_type=F32)
    r = lax.rsqrt(ms + NORM_EPS)
    if out_scale != 1.0:
        r = r * out_scale
    return y * r


def _attn_kernel(*refs, seq, tq, nqb, ctx_len, rope, window, sink, emit_kv):
    refs = list(refs)
    q_ref, gate_ref, k_ref, v_ref = refs[:4]
    pos = 4
    kc_ref = vc_ref = cos_ref = sin_ref = rot_ref = bias_ref = sink_ref = ko_ref = vo_ref = None
    if ctx_len:
        kc_ref, vc_ref = refs[pos:pos + 2]
        pos += 2
    if rope:
        cos_ref, sin_ref, rot_ref = refs[pos:pos + 3]
        pos += 3
    if window:
        bias_ref = refs[pos]
        pos += 1
    qn_ref, kn_ref = refs[pos:pos + 2]
    pos += 2
    if sink:
        sink_ref = refs[pos]
        pos += 1
    o_ref = refs[pos]
    pos += 1
    if emit_kv:
        ko_ref, vo_ref = refs[pos:pos + 2]
        pos += 2
    ks_ref, vs_ref = refs[pos:pos + 2]
    pos += 2
    qc_ref = qs_ref = None
    if rope:
        qc_ref, qs_ref = refs[pos:]

    kvh = pl.program_id(1)
    step = pl.program_id(2)
    nq = seq // tq
    pad = WINDOW if window else 0
    ctx_off = seq + 2 * pad
    qscale = LOG2E / math.sqrt(HEAD_DIM)

    @pl.when(step == 0)
    def _():
        if window:
            zeros = jnp.zeros((pad, HEAD_DIM), BF16)
            for ref in (ks_ref, vs_ref):
                ref[0:pad, 0:HEAD_DIM] = zeros
                ref[pad + seq:ctx_off, 0:HEAD_DIM] = zeros
        rc = 256
        kg = kn_ref[0:1, :]
        kgp = kn_ref[1:2, :]
        ones = jnp.ones((rc, HEAD_DIM), BF16)

        def body(r, carry):
            rows = pl.ds(pl.multiple_of(r * rc, rc), rc)
            dst = pl.ds(pl.multiple_of(pad + r * rc, HEAD_DIM), rc)
            if rope:
                cos = cos_ref[rows, :]
                sin = sin_ref[rows, :]
                qc_ref[rows, :] = cos * qn_ref[0:1, :]
                qs_ref[rows, :] = sin * qn_ref[1:2, :]
                k = _norm_rope(k_ref[rows, :], cos * kg, sin * kgp, rot_ref, 1.0)
            else:
                k = _norm_rope(k_ref[rows, :], kg, None, None, 1.0)
            if emit_kv:
                ko_ref[rows, :] = k
                vo_ref[rows, :] = v_ref[rows, :].astype(F32)
            ks_ref[dst, :] = k.astype(BF16)
            vs_ref[dst, 0:HEAD_DIM] = v_ref[rows, :]
            return carry

        lax.fori_loop(0, seq // rc, body, 0)
        if ctx_len:
            ks_ref[ctx_off:ctx_off + ctx_len, :] = kc_ref[...].astype(BF16)
            vs_ref[ctx_off:ctx_off + ctx_len, 0:HEAD_DIM] = vc_ref[...].astype(BF16)

        def fill(r, carry):
            vs_ref[pl.ds(pl.multiple_of(r * rc, rc), rc), HEAD_DIM:2 * HEAD_DIM] = ones
            return carry

        lax.fori_loop(0, vs_ref.shape[0] // rc, fill, 0)

    sink_col = None
    if sink:
        sink_col = jnp.concatenate(
            [jnp.full((tq, 1), sink_ref[kvh * GROUP + h] * LOG2E, F32) for h in range(GROUP)], axis=0)

    def prep(t):
        qi = step * nqb + t
        qrows = pl.ds(pl.multiple_of(qi * tq, tq), tq)
        xs = []
        for h in range(GROUP):
            xb = q_ref[t * tq:(t + 1) * tq, h * HEAD_DIM:(h + 1) * HEAD_DIM]
            if rope:
                x = _norm_rope(xb, qc_ref[qrows, :], qs_ref[qrows, :], rot_ref, qscale)
            else:
                x = _norm_rope(xb, qn_ref[0:1, :], None, None, qscale)
            xs.append(x.astype(BF16))
        return jnp.concatenate(xs, axis=0)

    def key_parts(t):
        qi = step * nqb + t
        if not window:
            return [(slice(None), None)]
        start = pl.multiple_of(qi * tq, HEAD_DIM)
        edge = jnp.where(qi == 0, 0, jnp.where(qi == nq - 1, 2, 1))
        parts = [(pl.ds(start, 3 * WINDOW), bias_ref[edge])]
        if ctx_len:
            parts.append((slice(ctx_off, ctx_off + ctx_len), None))
        return parts

    def logits(q, parts):
        ss = []
        for rows, bias in parts:
            s = lax.dot_general(q, ks_ref[rows, :], _NT, preferred_element_type=F32)
            ss.append(s if bias is None else s + bias)
        return ss

    def row_max(ss):
        slabs = [s[:, c:c + HEAD_DIM] for s in ss for c in range(0, s.shape[1], HEAD_DIM)]
        return jnp.max(functools.reduce(jnp.maximum, slabs), axis=-1, keepdims=True)

    def weighted(ss, m, parts):
        acc = None
        for s, (rows, _) in zip(ss, parts):
            pv = jnp.dot(jnp.exp2(s - m).astype(BF16), vs_ref[rows, :], preferred_element_type=F32)
            acc = pv if acc is None else acc + pv
        return acc

    def finish(t, acc, m):
        l = acc[:, HEAD_DIM:]
        if sink:
            l = l + jnp.exp2(sink_col - m)
        for h in range(GROUP):
            cols = slice(h * HEAD_DIM, (h + 1) * HEAD_DIM)
            rows = slice(h * tq, (h + 1) * tq)
            gate = gate_ref[t * tq:(t + 1) * tq, cols].astype(F32)
            inv = 1.0 / (l[rows, :] * (1.0 + jnp.exp(-gate)))
            o_ref[t * tq:(t + 1) * tq, cols] = (acc[rows, :HEAD_DIM] * gate * inv).astype(o_ref.dtype)

    blocks = range(nqb)
    parts = [key_parts(t) for t in blocks]
    qs = [prep(t) for t in blocks]
    ss = {0: logits(qs[0], parts[0])}
    ms = {}
    accs = {}
    for t in blocks:
        if t + 1 < nqb:
            ss[t + 1] = logits(qs[t + 1], parts[t + 1])
        ms[t] = row_max(ss[t])
        if t >= 1:
            accs[t - 1] = weighted(ss[t - 1], ms[t - 1], parts[t - 1])
    accs[nqb - 1] = weighted(ss[nqb - 1], ms[nqb - 1], parts[nqb - 1])
    for t in blocks:
        finish(t, accs[t], ms[t])


def _window_bias(tq):
    r = (jnp.arange(GROUP * tq) % tq)[:, None]
    c = jnp.arange(3 * WINDOW)[None, :]
    band = (c >= r) & (c <= r + 2 * WINDOW)
    keep = jnp.stack([band & (c >= WINDOW), band, band & (c < 2 * WINDOW)])
    return jnp.where(keep, 0.0, NEG).astype(F32)


def _attn(u3, q_off, k_off, v_off, g_off, kv_heads, qnorm, knorm, *, tq, nqb, ctx=None, rope=None,
          window=False, sink=None, emit_kv=False):
    bsz, seq, _ = u3.shape
    gw = GROUP * HEAD_DIM
    heads = kv_heads * GROUP
    ctx_len = 0 if ctx is None else ctx[0].shape[1]
    pad = WINDOW if window else 0
    assert not window or (tq == WINDOW and seq // tq >= 2)
    assert seq % (tq * nqb) == 0

    def wide(off):
        return pl.BlockSpec((None, tq * nqb, gw), lambda b, h, i: (b, i, off // gw + h))

    def narrow(off, rows):
        return pl.BlockSpec((None, rows, HEAD_DIM), lambda b, h, i: (b, 0, off // HEAD_DIM + h))

    def whole(shape):
        return pl.BlockSpec(shape, lambda b, h, i: (0,) * len(shape))

    in_specs = [wide(q_off), wide(g_off), narrow(k_off, seq), narrow(v_off, seq)]
    args = [u3, u3, u3, u3]
    if ctx is not None:
        for cache in ctx:
            in_specs.append(narrow(0, ctx_len))
            args.append(cache.reshape(bsz, ctx_len, kv_heads * HEAD_DIM))
    if rope is not None:
        for t in rope:
            in_specs.append(whole(t.shape))
            args.append(t)
    if window:
        in_specs.append(whole((3, GROUP * tq, 3 * WINDOW)))
        args.append(_window_bias(tq))
    in_specs += [whole((2, HEAD_DIM)), whole((2, HEAD_DIM))]
    args += [jnp.stack([qnorm, qnorm[_ROT_PARTNER]]), jnp.stack([knorm, knorm[_ROT_PARTNER]])]
    if sink is not None:
        in_specs.append(pl.BlockSpec(memory_space=pltpu.SMEM))
        args.append(sink)
    out_specs = [pl.BlockSpec((None, tq * nqb, gw), lambda b, h, i: (b, i, h))]
    out_shape = [jax.ShapeDtypeStruct((bsz, seq, heads * HEAD_DIM), BF16)]
    if emit_kv:
        for _ in range(2):
            out_specs.append(narrow(0, seq))
            out_shape.append(jax.ShapeDtypeStruct((bsz, seq, kv_heads * HEAD_DIM), F32))
    rows = seq + 2 * pad + ctx_len
    kern = functools.partial(_attn_kernel, seq=seq, tq=tq, nqb=nqb, ctx_len=ctx_len,
                             rope=rope is not None, window=window, sink=sink is not None,
                             emit_kv=emit_kv)
    return pl.pallas_call(
        kern,
        grid=(bsz, kv_heads, seq // (tq * nqb)),
        in_specs=in_specs,
        out_specs=out_specs,
        out_shape=out_shape,
        scratch_shapes=[pltpu.VMEM((rows, HEAD_DIM), BF16),
                        pltpu.VMEM((rows, 2 * HEAD_DIM), BF16)]
        + ([pltpu.VMEM((seq, HEAD_DIM), F32)] * 2 if rope is not None else []),
        compiler_params=_params("parallel", "parallel", "arbitrary"),
        name="attn",
    )(*args)


def _rope_tables(n_tokens):
    rows = n_tokens // GRID_W
    row = jnp.repeat(jnp.arange(rows), GRID_W).astype(F32)
    col = (jnp.arange(rows * GRID_W) % GRID_W).astype(F32)
    inv = ROPE_THETA ** (-jnp.arange(ROPE_QUARTER, dtype=F32) / ROPE_QUARTER)
    ar = row[:, None] * inv
    ac = col[:, None] * inv
    ang = jnp.concatenate([ar, ar, ac, ac], axis=-1)
    rot = np.zeros((2 * HEAD_DIM, 2 * HEAD_DIM), np.float32)
    rot[_ROT_PARTNER, np.arange(HEAD_DIM)] = np.where(_ROT_FIRST, -1.0, 1.0)
    rot[HEAD_DIM:, HEAD_DIM:] = 1.0 / HEAD_DIM
    return jnp.cos(ang), jnp.sin(ang), jnp.asarray(rot, BF16)


def _tile(m, pref):
    t = pref
    while m % t:
        t //= 2
    return t


def kernel(x_prompt, x_sample, state_l0_hgrn, cache_l0_k, cache_l0_v, cache_l1_k, cache_l1_v, c, c_ctx, lb_gamma, l0_norm, l0_w_mod, l0_b_mod, l0_w_in, l0_w_out, l0_a_onorm, l0_b_qnorm, l0_b_knorm, l1_norm, l1_w_mod, l1_b_mod, l1_w_in, l1_w_out, l1_c_qnorm, l1_c_knorm, l1_c_sink):
    pb, pl_, d = x_prompt.shape
    sb, sl, _ = x_sample.shape
    aw = A_HEADS * HEAD_DIM
    bkv = B_KV_HEADS * HEAD_DIM
    bw = B_KV_HEADS * GROUP * HEAD_DIM
    ckv = C_KV_HEADS * HEAD_DIM
    cw = C_KV_HEADS * GROUP * HEAD_DIM

    lb = jnp.cumsum(jax.nn.softmax(lb_gamma.astype(F32), axis=0), axis=0)[0]
    rope = _rope_tables(sl)

    nrow = -(-(sb + 1) // 8) * 8
    cond = jnp.zeros((nrow, d), F32).at[:sb].set(c).at[sb].set(c_ctx)
    xs = (x_prompt.reshape(pb * pl_, d), x_sample.reshape(sb * sl, d))
    tms = (_tile(pb * pl_, 1024), _tile(sl, 1024))
    rows_per_mod = (pb * pl_, sl)

    def mods(w_mod, b_mod):
        m = _adaln(cond, w_mod, b_mod)
        parts = [m[:, i * d:(i + 1) * d] for i in range(3)]
        return ([p[sb:sb + 1, None, :] for p in parts], [p[:sb, None, :] for p in parts])

    def tn_for(n):
        for t in (1280, 1024, 768, 512, 256, 128):
            if n % t == 0:
                return t
        return n

    mod_p, mod_s = mods(l0_w_mod, l0_b_mod)
    w_in = l0_w_in.astype(BF16)
    w_out = l0_w_out.astype(BF16)
    tn = tn_for(w_in.shape[1])
    u_p = _inproj(xs[0], l0_norm, mod_p[0], mod_p[1], w_in, rows_per_mod[0], tms[0], tn).reshape(pb, pl_, -1)
    u_s = _inproj(xs[1], l0_norm, mod_s[0], mod_s[1], w_in, rows_per_mod[1], tms[1], tn).reshape(sb, sl, -1)

    oa_p, new_state = _hgrn(u_p, lb, l0_a_onorm, None, hb=A_HEADS, unroll=2, emit_state=True)
    (oa_s,) = _hgrn(u_s, lb, l0_a_onorm, state_l0_hgrn, hb=4, unroll=8, emit_state=False)

    q_off = 5 * aw
    k_off = q_off + bw
    v_off = k_off + bkv
    g_off = v_off + bkv
    ob_p, k0, v0 = _attn(u_p, q_off, k_off, v_off, g_off, B_KV_HEADS, l0_b_qnorm, l0_b_knorm,
                         tq=128, nqb=2, emit_kv=True)
    (ob_s,) = _attn(u_s, q_off, k_off, v_off, g_off, B_KV_HEADS, l0_b_qnorm, l0_b_knorm,
                    tq=128, nqb=_tile(sl // 128, 8), ctx=(cache_l0_k, cache_l0_v), rope=rope)

    otm = (_tile(pb * pl_, 512), _tile(sl, 512))
    y_p = _outproj([oa_p.reshape(pb * pl_, aw), ob_p.reshape(pb * pl_, bw)], w_out, xs[0], mod_p[2],
                   rows_per_mod[0], otm[0])
    y_s = _outproj([oa_s.reshape(sb * sl, aw), ob_s.reshape(sb * sl, bw)], w_out, xs[1], mod_s[2],
                   rows_per_mod[1], otm[1])

    mod_p, mod_s = mods(l1_w_mod, l1_b_mod)
    w_in = l1_w_in.astype(BF16)
    w_out = l1_w_out.astype(BF16)
    tn = tn_for(w_in.shape[1])
    u_p = _inproj(y_p, l1_norm, mod_p[0], mod_p[1], w_in, rows_per_mod[0], tms[0], tn).reshape(pb, pl_, -1)
    u_s = _inproj(y_s, l1_norm, mod_s[0], mod_s[1], w_in, rows_per_mod[1], tms[1], tn).reshape(sb, sl, -1)

    k_off = cw
    v_off = k_off + ckv
    g_off = v_off + ckv
    oc_p, k1, v1 = _attn(u_p, 0, k_off, v_off, g_off, C_KV_HEADS, l1_c_qnorm, l1_c_knorm,
                         tq=128, nqb=2, sink=l1_c_sink, emit_kv=True)
    (oc_s,) = _attn(u_s, 0, k_off, v_off, g_off, C_KV_HEADS, l1_c_qnorm, l1_c_knorm,
                    tq=WINDOW, nqb=_tile(sl // WINDOW, 16), ctx=(cache_l1_k, cache_l1_v), rope=rope, window=True, sink=l1_c_sink)

    z_p = _outproj([oc_p.reshape(pb * pl_, cw)], w_out, y_p, mod_p[2], rows_per_mod[0], otm[0])
    z_s = _outproj([oc_s.reshape(sb * sl, cw)], w_out, y_s, mod_s[2], rows_per_mod[1], otm[1])

    return (z_p.reshape(pb, pl_, d), z_s.reshape(sb, sl, d), new_state,
            k0.reshape(pb, pl_, B_KV_HEADS, HEAD_DIM), v0.reshape(pb, pl_, B_KV_HEADS, HEAD_DIM),
            k1.reshape(pb, pl_, C_KV_HEADS, HEAD_DIM), v1.reshape(pb, pl_, C_KV_HEADS, HEAD_DIM))
```

```python
import functools
import math

import jax
import jax.numpy as jnp
import numpy as np
from jax import lax
from jax.experimental import pallas as pl
from jax.experimental.pallas import tpu as pltpu

F32 = jnp.float32
BF16 = jnp.bfloat16

HEAD_DIM = 128
GRID_W = 64
ROPE_QUARTER = HEAD_DIM // 4
ROPE_THETA = 10000.0
NORM_EPS = 1e-6
A_HEADS = 8
A_CHUNK = 32
B_KV_HEADS = 2
C_KV_HEADS = 4
GROUP = 4
WINDOW = 128
NEG = -1e30
LOG2E = math.log2(math.e)
VMEM_LIMIT = 56 * 1024 * 1024

_ROT_FIRST = (np.arange(HEAD_DIM) % (2 * ROPE_QUARTER)) < ROPE_QUARTER
_ROT_PARTNER = np.where(_ROT_FIRST, np.arange(HEAD_DIM) + ROPE_QUARTER, np.arange(HEAD_DIM) - ROPE_QUARTER)

_NT = (((1,), (1,)), ((), ()))
_TN = (((0,), (0,)), ((), ()))


def _params(*sem):
    return pltpu.CompilerParams(dimension_semantics=sem, vmem_limit_bytes=VMEM_LIMIT)


def _silu(x):
    return x * jax.nn.sigmoid(x)


def _rms(x, g):
    ms = jnp.mean(x * x, axis=-1, keepdims=True)
    return x * lax.rsqrt(ms + NORM_EPS) * g


def _adaln_kernel(c_ref, w_ref, b_ref, o_ref):
    a = _silu(c_ref[...]).astype(BF16)
    o_ref[...] = jnp.dot(a, w_ref[...].astype(BF16), preferred_element_type=F32) + b_ref[...]


def _adaln(cond, w_mod, b_mod):
    r, d = cond.shape
    n = w_mod.shape[1]
    tn = _tile(n, 512)
    return pl.pallas_call(
        _adaln_kernel,
        grid=(n // tn,),
        in_specs=[pl.BlockSpec((r, d), lambda j: (0, 0)),
                  pl.BlockSpec((d, tn), lambda j: (0, j)),
                  pl.BlockSpec((1, tn), lambda j: (0, j))],
        out_specs=pl.BlockSpec((r, tn), lambda j: (0, j)),
        out_shape=jax.ShapeDtypeStruct((r, n), F32),
        compiler_params=_params("parallel"),
        name="adaln",
    )(cond, w_mod, b_mod.reshape(1, n))


def _inproj_kernel(x_ref, g_ref, sh_ref, sc_ref, w_ref, o_ref, h_ref, *, rc, n_tiles, norm_steps):
    i = pl.program_id(0)
    j = pl.program_id(1)
    slice_rows = x_ref.shape[0] // norm_steps

    def matmul():
        o_ref[...] = jnp.dot(h_ref[(i + 1) % 2], w_ref[...], preferred_element_type=F32).astype(o_ref.dtype)

    def norm_slice():
        g = g_ref[...]
        mul = 1.0 + sc_ref[...]
        sh = sh_ref[...]
        for r in range(slice_rows // rc):
            rows = pl.ds(pl.multiple_of(j * slice_rows + r * rc, rc), rc)
            h_ref[i % 2, rows, :] = (_rms(x_ref[rows, :], g) * mul + sh).astype(BF16)

    has_mm = i >= 1
    has_norm = (i < n_tiles) & (j < norm_steps)

    @pl.when(has_mm & has_norm)
    def _():
        matmul()
        norm_slice()

    @pl.when(has_mm & jnp.logical_not(has_norm))
    def _():
        matmul()

    @pl.when(jnp.logical_not(has_mm) & has_norm)
    def _():
        norm_slice()


def _inproj(x2d, norm_g, shift, scale, w, rows_per_mod, tm, tn):
    m, d = x2d.shape
    n = w.shape[1]
    n_tiles = m // tm
    n_cols = n // tn
    norm_steps = max(s for s in (1, 2, 4, 8) if s <= n_cols and (tm // s) % 32 == 0)

    def tile(i):
        return jnp.minimum(i, n_tiles - 1)

    def col(i, j):
        return jnp.where(i == 0, 0, j)

    mod_spec = pl.BlockSpec((None, 1, d), lambda i, j: ((tile(i) * tm) // rows_per_mod, 0, 0))
    return pl.pallas_call(
        functools.partial(_inproj_kernel, rc=32, n_tiles=n_tiles, norm_steps=norm_steps),
        grid=(n_tiles + 1, n_cols),
        in_specs=[pl.BlockSpec((tm, d), lambda i, j: (tile(i), 0)),
                  pl.BlockSpec((1, d), lambda i, j: (0, 0)),
                  mod_spec, mod_spec,
                  pl.BlockSpec((d, tn), lambda i, j: (0, col(i, j)))],
        out_specs=pl.BlockSpec((tm, tn), lambda i, j: (jnp.maximum(i - 1, 0), col(i, j))),
        out_shape=jax.ShapeDtypeStruct((m, n), BF16),
        scratch_shapes=[pltpu.VMEM((2, tm, d), BF16)],
        compiler_params=_params("arbitrary", "arbitrary"),
        name="inproj",
    )(x2d, norm_g.reshape(1, d), shift, scale, w)


def _outproj_kernel(*refs, widths):
    o_refs = refs[:len(widths)]
    w_ref, x_ref, gt_ref, y_ref = refs[len(widths):]
    acc = None
    start = 0
    for o_ref, wd in zip(o_refs, widths):
        part = jnp.dot(o_ref[...], w_ref[start:start + wd, :], preferred_element_type=F32)
        acc = part if acc is None else acc + part
        start += wd
    y_ref[...] = x_ref[...] + gt_ref[...] * acc


def _outproj(os, w, x2d, gate, rows_per_mod, tm):
    m, d = x2d.shape
    widths = tuple(o.shape[1] for o in os)
    k = w.shape[0]
    return pl.pallas_call(
        functools.partial(_outproj_kernel, widths=widths),
        grid=(m // tm,),
        in_specs=[pl.BlockSpec((tm, wd), lambda i: (i, 0)) for wd in widths] + [
            pl.BlockSpec((k, d), lambda i: (0, 0)),
            pl.BlockSpec((tm, d), lambda i: (i, 0)),
            pl.BlockSpec((None, 1, d), lambda i: ((i * tm) // rows_per_mod, 0, 0))],
        out_specs=pl.BlockSpec((tm, d), lambda i: (i, 0)),
        out_shape=jax.ShapeDtypeStruct((m, d), F32),
        compiler_params=_params("parallel"),
        name="outproj",
    )(*os, w, x2d, gate)


SUBLANES = 8
HGRN_BLOCK = 128


def _chunk_cumprod(f, reverse):
    n = f.shape[0]
    tiles = n // SUBLANES
    per_chunk = A_CHUNK // SUBLANES
    x = f.reshape(tiles, SUBLANES, HEAD_DIM)
    sub = lax.broadcasted_iota(jnp.int32, x.shape, 1)
    s = 1
    while s < SUBLANES:
        if reverse:
            x = x * jnp.where(sub < SUBLANES - s, pltpu.roll(x, SUBLANES - s, axis=1), 1.0)
        else:
            x = x * jnp.where(sub >= s, pltpu.roll(x, s, axis=1), 1.0)
        s *= 2
    edge = 0 if reverse else SUBLANES - 1
    out, tot = [], []
    for c in range(n // A_CHUNK):
        ts = [x[c * per_chunk + i] for i in range(per_chunk)]
        order = range(per_chunk - 1, -1, -1) if reverse else range(per_chunk)
        carry = None
        done = {}
        for i in order:
            t = ts[i] if carry is None else ts[i] * carry
            done[i] = t
            carry = t[edge:edge + 1, :]
        out += [done[i] for i in range(per_chunk)]
        tot += [jnp.broadcast_to(carry, (A_CHUNK, HEAD_DIM))]
    return jnp.concatenate(out, axis=0), jnp.concatenate(tot, axis=0)


def _hgrn_kernel(*refs, seq, hb, unroll, has_s0, emit_state):
    q_ref, ff_ref, fb_ref, v_ref, g_ref, lb_ref, on_ref = refs[:7]
    pos = 7
    s0_ref = None
    if has_s0:
        s0_ref = refs[pos]
        pos += 1
    o_ref = refs[pos]
    pos += 1
    so_ref = None
    if emit_state:
        so_ref = refs[pos]
        pos += 1
    oacc_ref, st_ref, qd_ref, ke_ref, dec_ref = refs[pos:]

    nc = seq // A_CHUNK
    rb = HGRN_BLOCK
    cpb = rb // A_CHUNK
    ri = lax.broadcasted_iota(jnp.int32, (rb, rb), 0)
    ci = lax.broadcasted_iota(jnp.int32, (rb, rb), 1)
    shift = A_CHUNK.bit_length() - 1
    same_chunk = jnp.right_shift(ri, shift) == jnp.right_shift(ci, shift)
    masks = (same_chunk & (ri >= ci), same_chunk & (ci >= ri))

    for j in range(hb):
        for d in range(2):
            lanes = slice(d * HEAD_DIM, (d + 1) * HEAD_DIM)
            if has_s0:
                st_ref[j, :, lanes] = s0_ref[d, j].T
            else:
                st_ref[j, :, lanes] = jnp.zeros((HEAD_DIM, HEAD_DIM), F32)

    def stage1(blk, carry):
        rows = pl.ds(pl.multiple_of(blk * rb, rb), rb)
        heads = range(hb)
        chains = [(j, d) for j in heads for d in range(2)]
        cols = [slice(j * HEAD_DIM, (j + 1) * HEAD_DIM) for j in heads]
        qs = [_silu(q_ref[rows, cols[j]].astype(F32)) for j in heads]
        ts, fs, scans, qds, kinvs, atts = {}, {}, {}, {}, {}, {}
        for j, d in chains:
            fr = (fb_ref if d else ff_ref)[rows, cols[j]].astype(F32)
            lb = lb_ref[d:d + 1, cols[j]]
            ts[j, d] = (1.0 - lb) * jax.nn.sigmoid(fr)
            fs[j, d] = lb + ts[j, d]
        for j, d in chains:
            scans[j, d] = _chunk_cumprod(fs[j, d], reverse=bool(d))
        for j, d in chains:
            eb, dec = scans[j, d]
            lb = lb_ref[d:d + 1, cols[j]]
            qds[j, d] = (qs[j] * eb).astype(BF16)
            kinvs[j, d] = ((1.0 - lb) - ts[j, d]) * (1.0 / eb)
            qd_ref[d, j, rows, :] = qds[j, d]
            ke_ref[d, j, rows, :] = (kinvs[j, d] * dec).astype(BF16)
            for c in range(cpb):
                dec_ref[d, j, pl.ds(blk * cpb + c, 1), :] = dec[c * A_CHUNK:c * A_CHUNK + 1, :]
        for j, d in chains:
            att = lax.dot_general(qds[j, d], kinvs[j, d].astype(BF16), _NT, preferred_element_type=F32)
            atts[j, d] = jnp.where(masks[d], att, 0.0).astype(BF16)
        for j in heads:
            v = v_ref[rows, cols[j]]
            oacc_ref[rows, cols[j]] = (jnp.dot(atts[j, 0], v, preferred_element_type=F32)
                                       + jnp.dot(atts[j, 1], v, preferred_element_type=F32))
        return carry

    lax.fori_loop(0, seq // rb, stage1, 0)

    zero = jnp.zeros((A_CHUNK, HEAD_DIM), BF16)

    def blockdiag(a, b):
        return jnp.concatenate([jnp.concatenate([a, zero], axis=1), jnp.concatenate([zero, b], axis=1)], axis=0)

    def stage2(n, carry):
        cf, cb = n, nc - 1 - n
        rows_f = pl.ds(pl.multiple_of(cf * A_CHUNK, A_CHUNK), A_CHUNK)
        rows_b = pl.ds(pl.multiple_of(cb * A_CHUNK, A_CHUNK), A_CHUNK)
        for j in range(hb):
            cols = slice(j * HEAD_DIM, (j + 1) * HEAD_DIM)
            st = st_ref[j]
            qd = blockdiag(qd_ref[0, j, rows_f, :], qd_ref[1, j, rows_b, :])
            ke = blockdiag(ke_ref[0, j, rows_f, :], ke_ref[1, j, rows_b, :])
            v = jnp.concatenate([v_ref[rows_f, cols], v_ref[rows_b, cols]], axis=0)
            dec = jnp.concatenate([dec_ref[0, j, pl.ds(cf, 1), :], dec_ref[1, j, pl.ds(cb, 1), :]], axis=1)
            o = lax.dot_general(qd, st.astype(BF16), _NT, preferred_element_type=F32)
            st_ref[j] = st * dec + lax.dot_general(v, ke, _TN, preferred_element_type=F32)
            oacc_ref[rows_f, cols] += o[:A_CHUNK, :]
            oacc_ref[rows_b, cols] += o[A_CHUNK:, :]
        return carry

    lax.fori_loop(0, nc, stage2, 0, unroll=unroll)

    rc = 128
    on = on_ref[...]

    def epilogue(r, carry):
        rows = pl.ds(pl.multiple_of(r * rc, rc), rc)
        for j in range(hb):
            cols = slice(j * HEAD_DIM, (j + 1) * HEAD_DIM)
            y = _rms(oacc_ref[rows, cols], on)
            o_ref[rows, cols] = (y * _silu(g_ref[rows, cols].astype(F32))).astype(o_ref.dtype)
        return carry

    lax.fori_loop(0, seq // rc, epilogue, 0)

    if emit_state:
        for j in range(hb):
            for d in range(2):
                so_ref[d, j] = st_ref[j, :, d * HEAD_DIM:(d + 1) * HEAD_DIM].T


def _hgrn(u3, lb, onorm, s0, hb, unroll, emit_state):
    bsz, seq, _ = u3.shape
    bw = hb * HEAD_DIM
    nh = A_HEADS // hb

    def seg(s):
        return pl.BlockSpec((None, seq, bw), lambda b, h: (b, 0, s * nh + h))

    in_specs = [seg(0), seg(1), seg(2), seg(3), seg(4),
                pl.BlockSpec((2, bw), lambda b, h: (0, h)),
                pl.BlockSpec((1, HEAD_DIM), lambda b, h: (0, 0))]
    args = [u3, u3, u3, u3, u3, lb, onorm.reshape(1, HEAD_DIM)]
    st_spec = pl.BlockSpec((None, 2, hb, HEAD_DIM, HEAD_DIM), lambda b, h: (b, 0, h, 0, 0))
    if s0 is not None:
        in_specs.append(st_spec)
        args.append(s0)
    out_specs = [pl.BlockSpec((None, seq, bw), lambda b, h: (b, 0, h))]
    out_shape = [jax.ShapeDtypeStruct((bsz, seq, A_HEADS * HEAD_DIM), BF16)]
    if emit_state:
        out_specs.append(st_spec)
        out_shape.append(jax.ShapeDtypeStruct((bsz, 2, A_HEADS, HEAD_DIM, HEAD_DIM), F32))
    return pl.pallas_call(
        functools.partial(_hgrn_kernel, seq=seq, hb=hb, unroll=unroll, has_s0=s0 is not None,
                          emit_state=emit_state),
        grid=(bsz, nh),
        in_specs=in_specs,
        out_specs=out_specs,
        out_shape=out_shape,
        scratch_shapes=[pltpu.VMEM((seq, bw), F32),
                        pltpu.VMEM((hb, HEAD_DIM, 2 * HEAD_DIM), F32),
                        pltpu.VMEM((2, hb, seq, HEAD_DIM), BF16),
                        pltpu.VMEM((2, hb, seq, HEAD_DIM), BF16),
                        pltpu.VMEM((2, hb, seq // A_CHUNK, HEAD_DIM), F32)],
        compiler_params=_params("parallel", "parallel"),
        name="hgrn",
    )(*args)


def _norm_rope(xb, g_cos, gp_sin, rot_ref, out_scale):
    x = xb.astype(F32)
    sq = (x * x).astype(BF16)
    y = x * g_cos
    if gp_sin is not None:
        both = jnp.dot(jnp.concatenate([xb, sq], axis=1), rot_ref[...], preferred_element_type=F32)
        y = y + both[:, :HEAD_DIM] * gp_sin
        ms = both[:, HEAD_DIM:]
    else:
        mean_mat = jnp.full((HEAD_DIM, HEAD_DIM), 1.0 / HEAD_DIM, BF16)
        ms = jnp.dot(sq, mean_mat, preferred_element_type=F32)
    r = lax.rsqrt(ms + NORM_EPS)
    if out_scale != 1.0:
        r = r * out_scale
    return y * r


def _attn_kernel(*refs, seq, tq, nqb, ctx_len, rope, window, sink, emit_kv):
    refs = list(refs)
    q_ref, gate_ref, k_ref, v_ref = refs[:4]
    pos = 4
    kc_ref = vc_ref = cos_ref = sin_ref = rot_ref = bias_ref = sink_ref = ko_ref = vo_ref = None
    if ctx_len:
        kc_ref, vc_ref = refs[pos:pos + 2]
        pos += 2
    if rope:
        cos_ref, sin_ref, rot_ref = refs[pos:pos + 3]
        pos += 3
    if window:
        bias_ref = refs[pos]
        pos += 1
    qn_ref, kn_ref = refs[pos:pos + 2]
    pos += 2
    if sink:
        sink_ref = refs[pos]
        pos += 1
    o_ref = refs[pos]
    pos += 1
    if emit_kv:
        ko_ref, vo_ref = refs[pos:pos + 2]
        pos += 2
    ks_ref, vs_ref = refs[pos:pos + 2]
    pos += 2
    qc_ref = qs_ref = None
    if rope:
        qc_ref, qs_ref = refs[pos:]

    kvh = pl.program_id(1)
    step = pl.program_id(2)
    nq = seq // tq
    pad = WINDOW if window else 0
    ctx_off = seq + 2 * pad
    qscale = LOG2E / math.sqrt(HEAD_DIM)

    @pl.when(step == 0)
    def _():
        if window:
            zeros = jnp.zeros((pad, HEAD_DIM), BF16)
            for ref in (ks_ref, vs_ref):
                ref[0:pad, 0:HEAD_DIM] = zeros
                ref[pad + seq:ctx_off, 0:HEAD_DIM] = zeros
        rc = 256
        kg = kn_ref[0:1, :]
        kgp = kn_ref[1:2, :]
        ones = jnp.ones((rc, HEAD_DIM), BF16)

        def body(r, carry):
            rows = pl.ds(pl.multiple_of(r * rc, rc), rc)
            dst = pl.ds(pl.multiple_of(pad + r * rc, HEAD_DIM), rc)
            if rope:
                cos = cos_ref[rows, :]
                sin = sin_ref[rows, :]
                qc_ref[rows, :] = cos * qn_ref[0:1, :]
                qs_ref[rows, :] = sin * qn_ref[1:2, :]
                k = _norm_rope(k_ref[rows, :], cos * kg, sin * kgp, rot_ref, 1.0)
            else:
                k = _norm_rope(k_ref[rows, :], kg, None, None, 1.0)
            if emit_kv:
                ko_ref[rows, :] = k
                vo_ref[rows, :] = v_ref[rows, :].astype(F32)
            ks_ref[dst, :] = k.astype(BF16)
            vs_ref[dst, 0:HEAD_DIM] = v_ref[rows, :]
            return carry

        lax.fori_loop(0, seq // rc, body, 0)
        if ctx_len:
            ks_ref[ctx_off:ctx_off + ctx_len, :] = kc_ref[...].astype(BF16)
            vs_ref[ctx_off:ctx_off + ctx_len, 0:HEAD_DIM] = vc_ref[...].astype(BF16)

        def fill(r, carry):
            vs_ref[pl.ds(pl.multiple_of(r * rc, rc), rc), HEAD_DIM:2 * HEAD_DIM] = ones
            return carry

        lax.fori_loop(0, vs_ref.shape[0] // rc, fill, 0)

    sink_col = None
    if sink:
        sink_col = jnp.concatenate(
            [jnp.full((tq, 1), sink_ref[kvh * GROUP + h] * LOG2E, F32) for h in range(GROUP)], axis=0)

    def prep(t):
        qi = step * nqb + t
        qrows = pl.ds(pl.multiple_of(qi * tq, tq), tq)
        xs = []
        for h in range(GROUP):
            xb = q_ref[t * tq:(t + 1) * tq, h * HEAD_DIM:(h + 1) * HEAD_DIM]
            if rope:
                x = _norm_rope(xb, qc_ref[qrows, :], qs_ref[qrows, :], rot_ref, qscale)
            else:
                x = _norm_rope(xb, qn_ref[0:1, :], None, None, qscale)
            xs.append(x.astype(BF16))
        return jnp.concatenate(xs, axis=0)

    def key_parts(t):
        qi = step * nqb + t
        if not window:
            return [(slice(None), None)]
        start = pl.multiple_of(qi * tq, HEAD_DIM)
        edge = jnp.where(qi == 0, 0, jnp.where(qi == nq - 1, 2, 1))
        parts = [(pl.ds(start, 3 * WINDOW), bias_ref[edge])]
        if ctx_len:
            parts.append((slice(ctx_off, ctx_off + ctx_len), None))
        return parts

    def logits(q, parts):
        ss = []
        for rows, bias in parts:
            s = lax.dot_general(q, ks_ref[rows, :], _NT, preferred_element_type=F32)
            ss.append(s if bias is None else s + bias)
        return ss

    def row_max(ss):
        slabs = [s[:, c:c + HEAD_DIM] for s in ss for c in range(0, s.shape[1], HEAD_DIM)]
        return jnp.max(functools.reduce(jnp.maximum, slabs), axis=-1, keepdims=True)

    def weighted(ss, m, parts):
        acc = None
        for s, (rows, _) in zip(ss, parts):
            pv = jnp.dot(jnp.exp2(s - m).astype(BF16), vs_ref[rows, :], preferred_element_type=F32)
            acc = pv if acc is None else acc + pv
        return acc

    def finish(t, acc, m):
        l = acc[:, HEAD_DIM:]
        if sink:
            l = l + jnp.exp2(sink_col - m)
        for h in range(GROUP):
            cols = slice(h * HEAD_DIM, (h + 1) * HEAD_DIM)
            rows = slice(h * tq, (h + 1) * tq)
            gate = gate_ref[t * tq:(t + 1) * tq, cols].astype(F32)
            inv = 1.0 / (l[rows, :] * (1.0 + jnp.exp(-gate)))
            o_ref[t * tq:(t + 1) * tq, cols] = (acc[rows, :HEAD_DIM] * gate * inv).astype(o_ref.dtype)

    blocks = range(nqb)
    parts = [key_parts(t) for t in blocks]
    ss, ms = {}, {}
    q_next = prep(0)
    for t in blocks:
        q = q_next
        if t + 1 < nqb:
            q_next = prep(t + 1)
        ss[t] = logits(q, parts[t])
        ms[t] = row_max(ss[t])
        if t >= 1:
            finish(t - 1, weighted(ss.pop(t - 1), ms[t - 1], parts[t - 1]), ms.pop(t - 1))
    finish(nqb - 1, weighted(ss[nqb - 1], ms[nqb - 1], parts[nqb - 1]), ms[nqb - 1])


def _window_bias(tq):
    r = (jnp.arange(GROUP * tq) % tq)[:, None]
    c = jnp.arange(3 * WINDOW)[None, :]
    band = (c >= r) & (c <= r + 2 * WINDOW)
    keep = jnp.stack([band & (c >= WINDOW), band, band & (c < 2 * WINDOW)])
    return jnp.where(keep, 0.0, NEG).astype(F32)


def _attn(u3, q_off, k_off, v_off, g_off, kv_heads, qnorm, knorm, *, tq, nqb, ctx=None, rope=None,
          window=False, sink=None, emit_kv=False):
    bsz, seq, _ = u3.shape
    gw = GROUP * HEAD_DIM
    heads = kv_heads * GROUP
    ctx_len = 0 if ctx is None else ctx[0].shape[1]
    pad = WINDOW if window else 0
    assert not window or (tq == WINDOW and seq // tq >= 2)
    assert seq % (tq * nqb) == 0

    def wide(off):
        return pl.BlockSpec((None, tq * nqb, gw), lambda b, h, i: (b, i, off // gw + h))

    def narrow(off, rows):
        return pl.BlockSpec((None, rows, HEAD_DIM), lambda b, h, i: (b, 0, off // HEAD_DIM + h))

    def whole(shape):
        return pl.BlockSpec(shape, lambda b, h, i: (0,) * len(shape))

    in_specs = [wide(q_off), wide(g_off), narrow(k_off, seq), narrow(v_off, seq)]
    args = [u3, u3, u3, u3]
    if ctx is not None:
        for cache in ctx:
            in_specs.append(narrow(0, ctx_len))
            args.append(cache.reshape(bsz, ctx_len, kv_heads * HEAD_DIM))
    if rope is not None:
        for t in rope:
            in_specs.append(whole(t.shape))
            args.append(t)
    if window:
        in_specs.append(whole((3, GROUP * tq, 3 * WINDOW)))
        args.append(_window_bias(tq))
    in_specs += [whole((2, HEAD_DIM)), whole((2, HEAD_DIM))]
    args += [jnp.stack([qnorm, qnorm[_ROT_PARTNER]]), jnp.stack([knorm, knorm[_ROT_PARTNER]])]
    if sink is not None:
        in_specs.append(pl.BlockSpec(memory_space=pltpu.SMEM))
        args.append(sink)
    out_specs = [pl.BlockSpec((None, tq * nqb, gw), lambda b, h, i: (b, i, h))]
    out_shape = [jax.ShapeDtypeStruct((bsz, seq, heads * HEAD_DIM), BF16)]
    if emit_kv:
        for _ in range(2):
            out_specs.append(narrow(0, seq))
            out_shape.append(jax.ShapeDtypeStruct((bsz, seq, kv_heads * HEAD_DIM), F32))
    rows = seq + 2 * pad + ctx_len
    kern = functools.partial(_attn_kernel, seq=seq, tq=tq, nqb=nqb, ctx_len=ctx_len,
                             rope=rope is not None, window=window, sink=sink is not None,
                             emit_kv=emit_kv)
    return pl.pallas_call(
        kern,
        grid=(bsz, kv_heads, seq // (tq * nqb)),
        in_specs=in_specs,
        out_specs=out_specs,
        out_shape=out_shape,
        scratch_shapes=[pltpu.VMEM((rows, HEAD_DIM), BF16),
                        pltpu.VMEM((rows, 2 * HEAD_DIM), BF16)]
        + ([pltpu.VMEM((seq, HEAD_DIM), F32)] * 2 if rope is not None else []),
        compiler_params=_params("parallel", "parallel", "arbitrary"),
        name="attn",
    )(*args)


def _rope_tables(n_tokens):
    rows = n_tokens // GRID_W
    row = jnp.repeat(jnp.arange(rows), GRID_W).astype(F32)
    col = (jnp.arange(rows * GRID_W) % GRID_W).astype(F32)
    inv = ROPE_THETA ** (-jnp.arange(ROPE_QUARTER, dtype=F32) / ROPE_QUARTER)
    ar = row[:, None] * inv
    ac = col[:, None] * inv
    ang = jnp.concatenate([ar, ar, ac, ac], axis=-1)
    rot = np.zeros((2 * HEAD_DIM, 2 * HEAD_DIM), np.float32)
    rot[_ROT_PARTNER, np.arange(HEAD_DIM)] = np.where(_ROT_FIRST, -1.0, 1.0)
    rot[HEAD_DIM:, HEAD_DIM:] = 1.0 / HEAD_DIM
    return jnp.cos(ang), jnp.sin(ang), jnp.asarray(rot, BF16)


def _tile(m, pref):
    t = pref
    while m % t:
        t //= 2
    return t


def kernel(x_prompt, x_sample, state_l0_hgrn, cache_l0_k, cache_l0_v, cache_l1_k, cache_l1_v, c, c_ctx, lb_gamma, l0_norm, l0_w_mod, l0_b_mod, l0_w_in, l0_w_out, l0_a_onorm, l0_b_qnorm, l0_b_knorm, l1_norm, l1_w_mod, l1_b_mod, l1_w_in, l1_w_out, l1_c_qnorm, l1_c_knorm, l1_c_sink):
    pb, pl_, d = x_prompt.shape
    sb, sl, _ = x_sample.shape
    aw = A_HEADS * HEAD_DIM
    bkv = B_KV_HEADS * HEAD_DIM
    bw = B_KV_HEADS * GROUP * HEAD_DIM
    ckv = C_KV_HEADS * HEAD_DIM
    cw = C_KV_HEADS * GROUP * HEAD_DIM

    lb = jnp.cumsum(jax.nn.softmax(lb_gamma.astype(F32), axis=0), axis=0)[0]
    rope = _rope_tables(sl)

    nrow = -(-(sb + 1) // 8) * 8
    cond = jnp.zeros((nrow, d), F32).at[:sb].set(c).at[sb].set(c_ctx)
    xs = (x_prompt.reshape(pb * pl_, d), x_sample.reshape(sb * sl, d))
    tms = (_tile(pb * pl_, 1024), _tile(sl, 1024))
    rows_per_mod = (pb * pl_, sl)

    def mods(w_mod, b_mod):
        m = _adaln(cond, w_mod, b_mod)
        parts = [m[:, i * d:(i + 1) * d] for i in range(3)]
        return ([p[sb:sb + 1, None, :] for p in parts], [p[:sb, None, :] for p in parts])

    def tn_for(n):
        for t in (1536, 1280, 1024, 768, 512, 256, 128):
            if n % t == 0:
                return t
        return n

    mod_p, mod_s = mods(l0_w_mod, l0_b_mod)
    w_in = l0_w_in.astype(BF16)
    w_out = l0_w_out.astype(BF16)
    tn = tn_for(w_in.shape[1])
    u_p = _inproj(xs[0], l0_norm, mod_p[0], mod_p[1], w_in, rows_per_mod[0], tms[0], tn).reshape(pb, pl_, -1)
    u_s = _inproj(xs[1], l0_norm, mod_s[0], mod_s[1], w_in, rows_per_mod[1], tms[1], tn).reshape(sb, sl, -1)

    oa_p, new_state = _hgrn(u_p, lb, l0_a_onorm, None, hb=A_HEADS, unroll=4, emit_state=True)
    (oa_s,) = _hgrn(u_s, lb, l0_a_onorm, state_l0_hgrn, hb=4, unroll=8, emit_state=False)

    q_off = 5 * aw
    k_off = q_off + bw
    v_off = k_off + bkv
    g_off = v_off + bkv
    ob_p, k0, v0 = _attn(u_p, q_off, k_off, v_off, g_off, B_KV_HEADS, l0_b_qnorm, l0_b_knorm,
                         tq=128, nqb=2, emit_kv=True)
    (ob_s,) = _attn(u_s, q_off, k_off, v_off, g_off, B_KV_HEADS, l0_b_qnorm, l0_b_knorm,
                    tq=128, nqb=_tile(sl // 128, 8), ctx=(cache_l0_k, cache_l0_v), rope=rope)

    otm = (_tile(pb * pl_, 512), _tile(sl, 512))
    y_p = _outproj([oa_p.reshape(pb * pl_, aw), ob_p.reshape(pb * pl_, bw)], w_out, xs[0], mod_p[2],
                   rows_per_mod[0], otm[0])
    y_s = _outproj([oa_s.reshape(sb * sl, aw), ob_s.reshape(sb * sl, bw)], w_out, xs[1], mod_s[2],
                   rows_per_mod[1], otm[1])

    mod_p, mod_s = mods(l1_w_mod, l1_b_mod)
    w_in = l1_w_in.astype(BF16)
    w_out = l1_w_out.astype(BF16)
    tn = tn_for(w_in.shape[1])
    u_p = _inproj(y_p, l1_norm, mod_p[0], mod_p[1], w_in, rows_per_mod[0], tms[0], tn).reshape(pb, pl_, -1)
    u_s = _inproj(y_s, l1_norm, mod_s[0], mod_s[1], w_in, rows_per_mod[1], tms[1], tn).reshape(sb, sl, -1)

    k_off = cw
    v_off = k_off + ckv
    g_off = v_off + ckv
    oc_p, k1, v1 = _attn(u_p, 0, k_off, v_off, g_off, C_KV_HEADS, l1_c_qnorm, l1_c_knorm,
                         tq=128, nqb=2, sink=l1_c_sink, emit_kv=True)
    (oc_s,) = _attn(u_s, 0, k_off, v_off, g_off, C_KV_HEADS, l1_c_qnorm, l1_c_knorm,
                    tq=WINDOW, nqb=_tile(sl // WINDOW, 16), ctx=(cache_l1_k, cache_l1_v), rope=rope, window=True, sink=l1_c_sink)

    z_p = _outproj([oc_p.reshape(pb * pl_, cw)], w_out, y_p, mod_p[2], rows_per_mod[0], otm[0])
    z_s = _outproj([oc_s.reshape(sb * sl, cw)], w_out, y_s, mod_s[2], rows_per_mod[1], otm[1])

    return (z_p.reshape(pb, pl_, d), z_s.reshape(sb, sl, d), new_state,
            k0.reshape(pb, pl_, B_KV_HEADS, HEAD_DIM), v0.reshape(pb, pl_, B_KV_HEADS, HEAD_DIM),
            k1.reshape(pb, pl_, C_KV_HEADS, HEAD_DIM), v1.reshape(pb, pl_, C_KV_HEADS, HEAD_DIM))
```

```python
import functools
import math

import jax
import jax.numpy as jnp
import numpy as np
from jax import lax
from jax.experimental import pallas as pl
from jax.experimental.pallas import tpu as pltpu

F32 = jnp.float32
BF16 = jnp.bfloat16

HEAD_DIM = 128
GRID_W = 64
ROPE_QUARTER = HEAD_DIM // 4
ROPE_THETA = 10000.0
NORM_EPS = 1e-6
A_HEADS = 8
A_CHUNK = 32
B_KV_HEADS = 2
C_KV_HEADS = 4
GROUP = 4
WINDOW = 128
NEG = -1e30
LOG2E = math.log2(math.e)
VMEM_LIMIT = 56 * 1024 * 1024

_ROT_FIRST = (np.arange(HEAD_DIM) % (2 * ROPE_QUARTER)) < ROPE_QUARTER
_ROT_PARTNER = np.where(_ROT_FIRST, np.arange(HEAD_DIM) + ROPE_QUARTER, np.arange(HEAD_DIM) - ROPE_QUARTER)

_NT = (((1,), (1,)), ((), ()))
_TN = (((0,), (0,)), ((), ()))


def _params(*sem):
    return pltpu.CompilerParams(dimension_semantics=sem, vmem_limit_bytes=VMEM_LIMIT)


def _silu(x):
    return x * jax.nn.sigmoid(x)


def _rms(x, g):
    ms = jnp.mean(x * x, axis=-1, keepdims=True)
    return x * lax.rsqrt(ms + NORM_EPS) * g


def _adaln_kernel(c_ref, w_ref, b_ref, o_ref):
    a = _silu(c_ref[...]).astype(BF16)
    o_ref[...] = jnp.dot(a, w_ref[...].astype(BF16), preferred_element_type=F32) + b_ref[...]


def _adaln(cond, w_mod, b_mod):
    r, d = cond.shape
    n = w_mod.shape[1]
    tn = _tile(n, 512)
    return pl.pallas_call(
        _adaln_kernel,
        grid=(n // tn,),
        in_specs=[pl.BlockSpec((r, d), lambda j: (0, 0)),
                  pl.BlockSpec((d, tn), lambda j: (0, j)),
                  pl.BlockSpec((1, tn), lambda j: (0, j))],
        out_specs=pl.BlockSpec((r, tn), lambda j: (0, j)),
        out_shape=jax.ShapeDtypeStruct((r, n), F32),
        compiler_params=_params("parallel"),
        name="adaln",
    )(cond, w_mod, b_mod.reshape(1, n))


def _inproj_kernel(x_ref, g_ref, sh_ref, sc_ref, w_ref, o_ref, h_ref, *, rc, n_tiles, norm_steps):
    i = pl.program_id(0)
    j = pl.program_id(1)
    slice_rows = x_ref.shape[0] // norm_steps

    def matmul():
        o_ref[...] = jnp.dot(h_ref[(i + 1) % 2], w_ref[...], preferred_element_type=F32).astype(o_ref.dtype)

    def norm_slice():
        g = g_ref[...]
        mul = 1.0 + sc_ref[...]
        sh = sh_ref[...]
        for r in range(slice_rows // rc):
            rows = pl.ds(pl.multiple_of(j * slice_rows + r * rc, rc), rc)
            h_ref[i % 2, rows, :] = (_rms(x_ref[rows, :], g) * mul + sh).astype(BF16)

    has_mm = i >= 1
    has_norm = (i < n_tiles) & (j < norm_steps)

    @pl.when(has_mm & has_norm)
    def _():
        matmul()
        norm_slice()

    @pl.when(has_mm & jnp.logical_not(has_norm))
    def _():
        matmul()

    @pl.when(jnp.logical_not(has_mm) & has_norm)
    def _():
        norm_slice()


def _inproj(x2d, norm_g, shift, scale, w, rows_per_mod, tm, tn):
    m, d = x2d.shape
    n = w.shape[1]
    n_tiles = m // tm
    n_cols = n // tn
    norm_steps = max(s for s in (1, 2, 4, 8) if s <= n_cols and (tm // s) % 32 == 0)

    def tile(i):
        return jnp.minimum(i, n_tiles - 1)

    def col(i, j):
        return jnp.where(i == 0, 0, j)

    mod_spec = pl.BlockSpec((None, 1, d), lambda i, j: ((tile(i) * tm) // rows_per_mod, 0, 0))
    return pl.pallas_call(
        functools.partial(_inproj_kernel, rc=32, n_tiles=n_tiles, norm_steps=norm_steps),
        grid=(n_tiles + 1, n_cols),
        in_specs=[pl.BlockSpec((tm, d), lambda i, j: (tile(i), 0)),
                  pl.BlockSpec((1, d), lambda i, j: (0, 0)),
                  mod_spec, mod_spec,
                  pl.BlockSpec((d, tn), lambda i, j: (0, col(i, j)))],
        out_specs=pl.BlockSpec((tm, tn), lambda i, j: (jnp.maximum(i - 1, 0), col(i, j))),
        out_shape=jax.ShapeDtypeStruct((m, n), BF16),
        scratch_shapes=[pltpu.VMEM((2, tm, d), BF16)],
        compiler_params=_params("arbitrary", "arbitrary"),
        name="inproj",
    )(x2d, norm_g.reshape(1, d), shift, scale, w)


def _outproj_kernel(*refs, widths):
    o_refs = refs[:len(widths)]
    w_ref, x_ref, gt_ref, y_ref = refs[len(widths):]
    acc = None
    start = 0
    for o_ref, wd in zip(o_refs, widths):
        part = jnp.dot(o_ref[...], w_ref[start:start + wd, :], preferred_element_type=F32)
        acc = part if acc is None else acc + part
        start += wd
    y_ref[...] = x_ref[...] + gt_ref[...] * acc


def _outproj(os, w, x2d, gate, rows_per_mod, tm):
    m, d = x2d.shape
    widths = tuple(o.shape[1] for o in os)
    k = w.shape[0]
    return pl.pallas_call(
        functools.partial(_outproj_kernel, widths=widths),
        grid=(m // tm,),
        in_specs=[pl.BlockSpec((tm, wd), lambda i: (i, 0)) for wd in widths] + [
            pl.BlockSpec((k, d), lambda i: (0, 0)),
            pl.BlockSpec((tm, d), lambda i: (i, 0)),
            pl.BlockSpec((None, 1, d), lambda i: ((i * tm) // rows_per_mod, 0, 0))],
        out_specs=pl.BlockSpec((tm, d), lambda i: (i, 0)),
        out_shape=jax.ShapeDtypeStruct((m, d), F32),
        compiler_params=_params("parallel"),
        name="outproj",
    )(*os, w, x2d, gate)


SUBLANES = 8
HGRN_BLOCK = 128


def _chunk_cumprod(f, reverse):
    n = f.shape[0]
    tiles = n // SUBLANES
    per_chunk = A_CHUNK // SUBLANES
    x = f.reshape(tiles, SUBLANES, HEAD_DIM)
    sub = lax.broadcasted_iota(jnp.int32, x.shape, 1)
    s = 1
    while s < SUBLANES:
        if reverse:
            x = x * jnp.where(sub < SUBLANES - s, pltpu.roll(x, SUBLANES - s, axis=1), 1.0)
        else:
            x = x * jnp.where(sub >= s, pltpu.roll(x, s, axis=1), 1.0)
        s *= 2
    edge = 0 if reverse else SUBLANES - 1
    out, tot = [], []
    for c in range(n // A_CHUNK):
        ts = [x[c * per_chunk + i] for i in range(per_chunk)]
        order = range(per_chunk - 1, -1, -1) if reverse else range(per_chunk)
        carry = None
        done = {}
        for i in order:
            t = ts[i] if carry is None else ts[i] * carry
            done[i] = t
            carry = t[edge:edge + 1, :]
        out += [done[i] for i in range(per_chunk)]
        tot += [jnp.broadcast_to(carry, (A_CHUNK, HEAD_DIM))]
    return jnp.concatenate(out, axis=0), jnp.concatenate(tot, axis=0)


def _hgrn_kernel(*refs, seq, hb, has_s0, emit_state):
    q_ref, ff_ref, fb_ref, v_ref, g_ref, lb_ref, on_ref = refs[:7]
    pos = 7
    s0_ref = None
    if has_s0:
        s0_ref = refs[pos]
        pos += 1
    o_ref = refs[pos]
    pos += 1
    so_ref = None
    if emit_state:
        so_ref = refs[pos]
        pos += 1
    oacc_ref, st_ref, qd_ref, ke_ref, dec_ref = refs[pos:]

    nc = seq // A_CHUNK
    rb = HGRN_BLOCK
    cpb = rb // A_CHUNK
    ri = lax.broadcasted_iota(jnp.int32, (rb, rb), 0)
    ci = lax.broadcasted_iota(jnp.int32, (rb, rb), 1)
    shift = A_CHUNK.bit_length() - 1
    same_chunk = jnp.right_shift(ri, shift) == jnp.right_shift(ci, shift)
    masks = (same_chunk & (ri >= ci), same_chunk & (ci >= ri))

    for j in range(hb):
        for d in range(2):
            lanes = slice(d * HEAD_DIM, (d + 1) * HEAD_DIM)
            if has_s0:
                st_ref[j, :, lanes] = s0_ref[d, j].T
            else:
                st_ref[j, :, lanes] = jnp.zeros((HEAD_DIM, HEAD_DIM), F32)

    def stage1(blk, carry):
        rows = pl.ds(pl.multiple_of(blk * rb, rb), rb)
        heads = range(hb)
        chains = [(j, d) for j in heads for d in range(2)]
        cols = [slice(j * HEAD_DIM, (j + 1) * HEAD_DIM) for j in heads]
        qs = [_silu(q_ref[rows, cols[j]].astype(F32)) for j in heads]
        ts, fs, scans, qds, kinvs, atts = {}, {}, {}, {}, {}, {}
        for j, d in chains:
            fr = (fb_ref if d else ff_ref)[rows, cols[j]].astype(F32)
            lb = lb_ref[d:d + 1, cols[j]]
            ts[j, d] = (1.0 - lb) * jax.nn.sigmoid(fr)
            fs[j, d] = lb + ts[j, d]
        for j, d in chains:
            scans[j, d] = _chunk_cumprod(fs[j, d], reverse=bool(d))
        for j, d in chains:
            eb, dec = scans[j, d]
            lb = lb_ref[d:d + 1, cols[j]]
            qds[j, d] = (qs[j] * eb).astype(BF16)
            kinvs[j, d] = ((1.0 - lb) - ts[j, d]) * (1.0 / eb)
            qd_ref[d, j, rows, :] = qds[j, d]
            ke_ref[d, j, rows, :] = (kinvs[j, d] * dec).astype(BF16)
            for c in range(cpb):
                dec_ref[d, j, pl.ds(blk * cpb + c, 1), :] = dec[c * A_CHUNK:c * A_CHUNK + 1, :]
        for j, d in chains:
            att = lax.dot_general(qds[j, d], kinvs[j, d].astype(BF16), _NT, preferred_element_type=F32)
            atts[j, d] = jnp.where(masks[d], att, 0.0).astype(BF16)
        for j in heads:
            v = v_ref[rows, cols[j]]
            oacc_ref[rows, cols[j]] = (jnp.dot(atts[j, 0], v, preferred_element_type=F32)
                                       + jnp.dot(atts[j, 1], v, preferred_element_type=F32))
        return carry

    zero = jnp.zeros((A_CHUNK, HEAD_DIM), BF16)

    def blockdiag(a, b):
        return jnp.concatenate([jnp.concatenate([a, zero], axis=1), jnp.concatenate([zero, b], axis=1)], axis=0)

    def stage2(n, carry):
        cf, cb = n, nc - 1 - n
        rows_f = pl.ds(pl.multiple_of(cf * A_CHUNK, A_CHUNK), A_CHUNK)
        rows_b = pl.ds(pl.multiple_of(cb * A_CHUNK, A_CHUNK), A_CHUNK)
        for j in range(hb):
            cols = slice(j * HEAD_DIM, (j + 1) * HEAD_DIM)
            st = st_ref[j]
            qd = blockdiag(qd_ref[0, j, rows_f, :], qd_ref[1, j, rows_b, :])
            ke = blockdiag(ke_ref[0, j, rows_f, :], ke_ref[1, j, rows_b, :])
            v = jnp.concatenate([v_ref[rows_f, cols], v_ref[rows_b, cols]], axis=0)
            dec = jnp.concatenate([dec_ref[0, j, pl.ds(cf, 1), :], dec_ref[1, j, pl.ds(cb, 1), :]], axis=1)
            o = lax.dot_general(qd, st.astype(BF16), _NT, preferred_element_type=F32)
            st_ref[j] = st * dec + lax.dot_general(v, ke, _TN, preferred_element_type=F32)
            oacc_ref[rows_f, cols] += o[:A_CHUNK, :]
            oacc_ref[rows_b, cols] += o[A_CHUNK:, :]
        return carry

    nb = seq // rb
    mid = nb // 2
    on = on_ref[...]

    def merged(p, carry):
        stage1(p, carry)
        stage1(nb - 1 - p, carry)
        for i in range(cpb):
            stage2(p * cpb + i, carry)
        return carry

    def finish_block(r):
        rows = pl.ds(pl.multiple_of(r * rb, rb), rb)
        for j in range(hb):
            cols = slice(j * HEAD_DIM, (j + 1) * HEAD_DIM)
            y = _rms(oacc_ref[rows, cols], on)
            o_ref[rows, cols] = (y * _silu(g_ref[rows, cols].astype(F32))).astype(o_ref.dtype)

    def tail(g, carry):
        finish_block(mid - g)
        finish_block(mid + g - 1)
        for i in range(cpb):
            stage2(nc // 2 + g * cpb + i, carry)
        return carry

    lax.fori_loop(0, mid, merged, 0)
    for i in range(cpb):
        stage2(nc // 2 + i, 0)
    lax.fori_loop(1, mid, tail, 0)
    finish_block(0)
    finish_block(nb - 1)

    if emit_state:
        for j in range(hb):
            for d in range(2):
                so_ref[d, j] = st_ref[j, :, d * HEAD_DIM:(d + 1) * HEAD_DIM].T


def _hgrn(u3, lb, onorm, s0, hb, emit_state):
    bsz, seq, _ = u3.shape
    bw = hb * HEAD_DIM
    nh = A_HEADS // hb

    def seg(s):
        return pl.BlockSpec((None, seq, bw), lambda b, h: (b, 0, s * nh + h))

    in_specs = [seg(0), seg(1), seg(2), seg(3), seg(4),
                pl.BlockSpec((2, bw), lambda b, h: (0, h)),
                pl.BlockSpec((1, HEAD_DIM), lambda b, h: (0, 0))]
    args = [u3, u3, u3, u3, u3, lb, onorm.reshape(1, HEAD_DIM)]
    st_spec = pl.BlockSpec((None, 2, hb, HEAD_DIM, HEAD_DIM), lambda b, h: (b, 0, h, 0, 0))
    if s0 is not None:
        in_specs.append(st_spec)
        args.append(s0)
    out_specs = [pl.BlockSpec((None, seq, bw), lambda b, h: (b, 0, h))]
    out_shape = [jax.ShapeDtypeStruct((bsz, seq, A_HEADS * HEAD_DIM), BF16)]
    if emit_state:
        out_specs.append(st_spec)
        out_shape.append(jax.ShapeDtypeStruct((bsz, 2, A_HEADS, HEAD_DIM, HEAD_DIM), F32))
    return pl.pallas_call(
        functools.partial(_hgrn_kernel, seq=seq, hb=hb, has_s0=s0 is not None,
                          emit_state=emit_state),
        grid=(bsz, nh),
        in_specs=in_specs,
        out_specs=out_specs,
        out_shape=out_shape,
        scratch_shapes=[pltpu.VMEM((seq, bw), F32),
                        pltpu.VMEM((hb, HEAD_DIM, 2 * HEAD_DIM), F32),
                        pltpu.VMEM((2, hb, seq, HEAD_DIM), BF16),
                        pltpu.VMEM((2, hb, seq, HEAD_DIM), BF16),
                        pltpu.VMEM((2, hb, seq // A_CHUNK, HEAD_DIM), F32)],
        compiler_params=_params("parallel", "parallel"),
        name="hgrn",
    )(*args)


def _norm_rope(xb, g_cos, gp_sin, rot_ref, out_scale):
    x = xb.astype(F32)
    sq = (x * x).astype(BF16)
    y = x * g_cos
    if gp_sin is not None:
        both = jnp.dot(jnp.concatenate([xb, sq], axis=1), rot_ref[...], preferred_element_type=F32)
        y = y + both[:, :HEAD_DIM] * gp_sin
        ms = both[:, HEAD_DIM:]
    else:
        mean_mat = jnp.full((HEAD_DIM, HEAD_DIM), 1.0 / HEAD_DIM, BF16)
        ms = jnp.dot(sq, mean_mat, preferred_element_type=F32)
    r = lax.rsqrt(ms + NORM_EPS)
    if out_scale != 1.0:
        r = r * out_scale
    return y * r


def _attn_kernel(*refs, seq, tq, nqb, ctx_len, rope, window, sink, emit_kv):
    refs = list(refs)
    q_ref, gate_ref, k_ref, v_ref = refs[:4]
    pos = 4
    kc_ref = vc_ref = cos_ref = sin_ref = rot_ref = bias_ref = sink_ref = ko_ref = vo_ref = None
    if ctx_len:
        kc_ref, vc_ref = refs[pos:pos + 2]
        pos += 2
    if rope:
        cos_ref, sin_ref, rot_ref = refs[pos:pos + 3]
        pos += 3
    if window:
        bias_ref = refs[pos]
        pos += 1
    qn_ref, kn_ref = refs[pos:pos + 2]
    pos += 2
    if sink:
        sink_ref = refs[pos]
        pos += 1
    o_ref = refs[pos]
    pos += 1
    if emit_kv:
        ko_ref, vo_ref = refs[pos:pos + 2]
        pos += 2
    ks_ref, vs_ref = refs[pos:pos + 2]
    pos += 2
    qc_ref = qs_ref = None
    if rope:
        qc_ref, qs_ref = refs[pos:]

    kvh = pl.program_id(1)
    step = pl.program_id(2)
    nq = seq // tq
    pad = WINDOW if window else 0
    ctx_off = seq + 2 * pad
    qscale = LOG2E / math.sqrt(HEAD_DIM)

    @pl.when(step == 0)
    def _():
        if window:
            zeros = jnp.zeros((pad, HEAD_DIM), BF16)
            for ref in (ks_ref, vs_ref):
                ref[0:pad, 0:HEAD_DIM] = zeros
                ref[pad + seq:ctx_off, 0:HEAD_DIM] = zeros
        rc = 256
        kg = kn_ref[0:1, :]
        kgp = kn_ref[1:2, :]
        ones = jnp.ones((rc, HEAD_DIM), BF16)

        def body(r, carry):
            rows = pl.ds(pl.multiple_of(r * rc, rc), rc)
            dst = pl.ds(pl.multiple_of(pad + r * rc, HEAD_DIM), rc)
            if rope:
                cos = cos_ref[rows, :]
                sin = sin_ref[rows, :]
                qc_ref[rows, :] = cos * qn_ref[0:1, :]
                qs_ref[rows, :] = sin * qn_ref[1:2, :]
                k = _norm_rope(k_ref[rows, :], cos * kg, sin * kgp, rot_ref, 1.0)
            else:
                k = _norm_rope(k_ref[rows, :], kg, None, None, 1.0)
            if emit_kv:
                ko_ref[rows, :] = k
                vo_ref[rows, :] = v_ref[rows, :].astype(F32)
            ks_ref[dst, :] = k.astype(BF16)
            vs_ref[dst, 0:HEAD_DIM] = v_ref[rows, :]
            return carry

        lax.fori_loop(0, seq // rc, body, 0)
        if ctx_len:
            ks_ref[ctx_off:ctx_off + ctx_len, :] = kc_ref[...].astype(BF16)
            vs_ref[ctx_off:ctx_off + ctx_len, 0:HEAD_DIM] = vc_ref[...].astype(BF16)

        def fill(r, carry):
            vs_ref[pl.ds(pl.multiple_of(r * rc, rc), rc), HEAD_DIM:2 * HEAD_DIM] = ones
            return carry

        lax.fori_loop(0, vs_ref.shape[0] // rc, fill, 0)

    sink_col = None
    if sink:
        sink_col = jnp.concatenate(
            [jnp.full((tq, 1), sink_ref[kvh * GROUP + h] * LOG2E, F32) for h in range(GROUP)], axis=0)

    def prep(t):
        qi = step * nqb + t
        qrows = pl.ds(pl.multiple_of(qi * tq, tq), tq)
        xs = []
        for h in range(GROUP):
            xb = q_ref[t * tq:(t + 1) * tq, h * HEAD_DIM:(h + 1) * HEAD_DIM]
            if rope:
                x = _norm_rope(xb, qc_ref[qrows, :], qs_ref[qrows, :], rot_ref, qscale)
            else:
                x = _norm_rope(xb, qn_ref[0:1, :], None, None, qscale)
            xs.append(x.astype(BF16))
        return jnp.concatenate(xs, axis=0)

    def key_parts(t):
        qi = step * nqb + t
        if not window:
            return [(slice(None), None)]
        start = pl.multiple_of(qi * tq, HEAD_DIM)
        edge = jnp.where(qi == 0, 0, jnp.where(qi == nq - 1, 2, 1))
        parts = [(pl.ds(start, 3 * WINDOW), bias_ref[edge])]
        if ctx_len:
            parts.append((slice(ctx_off, ctx_off + ctx_len), None))
        return parts

    def logits(q, parts):
        ss = []
        for rows, bias in parts:
            s = lax.dot_general(q, ks_ref[rows, :], _NT, preferred_element_type=F32)
            ss.append(s if bias is None else s + bias)
        return ss

    def row_max(ss):
        slabs = [s[:, c:c + HEAD_DIM] for s in ss for c in range(0, s.shape[1], HEAD_DIM)]
        return jnp.max(functools.reduce(jnp.maximum, slabs), axis=-1, keepdims=True)

    def weighted(ss, m, parts):
        acc = None
        for s, (rows, _) in zip(ss, parts):
            pv = jnp.dot(jnp.exp2(s - m).astype(BF16), vs_ref[rows, :], preferred_element_type=F32)
            acc = pv if acc is None else acc + pv
        return acc

    def finish(t, acc, m):
        l = acc[:, HEAD_DIM:]
        if sink:
            l = l + jnp.exp2(sink_col - m)
        for h in range(GROUP):
            cols = slice(h * HEAD_DIM, (h + 1) * HEAD_DIM)
            rows = slice(h * tq, (h + 1) * tq)
            gate = gate_ref[t * tq:(t + 1) * tq, cols].astype(F32)
            inv = 1.0 / (l[rows, :] * (1.0 + jnp.exp(-gate)))
            o_ref[t * tq:(t + 1) * tq, cols] = (acc[rows, :HEAD_DIM] * gate * inv).astype(o_ref.dtype)

    blocks = range(nqb)
    parts = [key_parts(t) for t in blocks]
    ss, ms = {}, {}
    q_next = prep(0)
    for t in blocks:
        q = q_next
        if t + 1 < nqb:
            q_next = prep(t + 1)
        ss[t] = logits(q, parts[t])
        ms[t] = row_max(ss[t])
        if t >= 1:
            finish(t - 1, weighted(ss.pop(t - 1), ms[t - 1], parts[t - 1]), ms.pop(t - 1))
    finish(nqb - 1, weighted(ss[nqb - 1], ms[nqb - 1], parts[nqb - 1]), ms[nqb - 1])


def _window_bias(tq):
    r = (jnp.arange(GROUP * tq) % tq)[:, None]
    c = jnp.arange(3 * WINDOW)[None, :]
    band = (c >= r) & (c <= r + 2 * WINDOW)
    keep = jnp.stack([band & (c >= WINDOW), band, band & (c < 2 * WINDOW)])
    return jnp.where(keep, 0.0, NEG).astype(F32)


def _attn(u3, q_off, k_off, v_off, g_off, kv_heads, qnorm, knorm, *, tq, nqb, ctx=None, rope=None,
          window=False, sink=None, emit_kv=False):
    bsz, seq, _ = u3.shape
    gw = GROUP * HEAD_DIM
    heads = kv_heads * GROUP
    ctx_len = 0 if ctx is None else ctx[0].shape[1]
    pad = WINDOW if window else 0
    assert not window or (tq == WINDOW and seq // tq >= 2)
    assert seq % (tq * nqb) == 0

    def wide(off):
        return pl.BlockSpec((None, tq * nqb, gw), lambda b, h, i: (b, i, off // gw + h))

    def narrow(off, rows):
        return pl.BlockSpec((None, rows, HEAD_DIM), lambda b, h, i: (b, 0, off // HEAD_DIM + h))

    def whole(shape):
        return pl.BlockSpec(shape, lambda b, h, i: (0,) * len(shape))

    in_specs = [wide(q_off), wide(g_off), narrow(k_off, seq), narrow(v_off, seq)]
    args = [u3, u3, u3, u3]
    if ctx is not None:
        for cache in ctx:
            in_specs.append(narrow(0, ctx_len))
            args.append(cache.reshape(bsz, ctx_len, kv_heads * HEAD_DIM))
    if rope is not None:
        for t in rope:
            in_specs.append(whole(t.shape))
            args.append(t)
    if window:
        in_specs.append(whole((3, GROUP * tq, 3 * WINDOW)))
        args.append(_window_bias(tq))
    in_specs += [whole((2, HEAD_DIM)), whole((2, HEAD_DIM))]
    args += [jnp.stack([qnorm, qnorm[_ROT_PARTNER]]), jnp.stack([knorm, knorm[_ROT_PARTNER]])]
    if sink is not None:
        in_specs.append(pl.BlockSpec(memory_space=pltpu.SMEM))
        args.append(sink)
    out_specs = [pl.BlockSpec((None, tq * nqb, gw), lambda b, h, i: (b, i, h))]
    out_shape = [jax.ShapeDtypeStruct((bsz, seq, heads * HEAD_DIM), BF16)]
    if emit_kv:
        for _ in range(2):
            out_specs.append(narrow(0, seq))
            out_shape.append(jax.ShapeDtypeStruct((bsz, seq, kv_heads * HEAD_DIM), F32))
    rows = seq + 2 * pad + ctx_len
    kern = functools.partial(_attn_kernel, seq=seq, tq=tq, nqb=nqb, ctx_len=ctx_len,
                             rope=rope is not None, window=window, sink=sink is not None,
                             emit_kv=emit_kv)
    return pl.pallas_call(
        kern,
        grid=(bsz, kv_heads, seq // (tq * nqb)),
        in_specs=in_specs,
        out_specs=out_specs,
        out_shape=out_shape,
        scratch_shapes=[pltpu.VMEM((rows, HEAD_DIM), BF16),
                        pltpu.VMEM((rows, 2 * HEAD_DIM), BF16)]
        + ([pltpu.VMEM((seq, HEAD_DIM), F32)] * 2 if rope is not None else []),
        compiler_params=_params("parallel", "parallel", "arbitrary"),
        name="attn",
    )(*args)


def _rope_tables(n_tokens):
    rows = n_tokens // GRID_W
    row = jnp.repeat(jnp.arange(rows), GRID_W).astype(F32)
    col = (jnp.arange(rows * GRID_W) % GRID_W).astype(F32)
    inv = ROPE_THETA ** (-jnp.arange(ROPE_QUARTER, dtype=F32) / ROPE_QUARTER)
    ar = row[:, None] * inv
    ac = col[:, None] * inv
    ang = jnp.concatenate([ar, ar, ac, ac], axis=-1)
    rot = np.zeros((2 * HEAD_DIM, 2 * HEAD_DIM), np.float32)
    rot[_ROT_PARTNER, np.arange(HEAD_DIM)] = np.where(_ROT_FIRST, -1.0, 1.0)
    rot[HEAD_DIM:, HEAD_DIM:] = 1.0 / HEAD_DIM
    return jnp.cos(ang), jnp.sin(ang), jnp.asarray(rot, BF16)


def _tile(m, pref):
    t = pref
    while m % t:
        t //= 2
    return t


def kernel(x_prompt, x_sample, state_l0_hgrn, cache_l0_k, cache_l0_v, cache_l1_k, cache_l1_v, c, c_ctx, lb_gamma, l0_norm, l0_w_mod, l0_b_mod, l0_w_in, l0_w_out, l0_a_onorm, l0_b_qnorm, l0_b_knorm, l1_norm, l1_w_mod, l1_b_mod, l1_w_in, l1_w_out, l1_c_qnorm, l1_c_knorm, l1_c_sink):
    pb, pl_, d = x_prompt.shape
    sb, sl, _ = x_sample.shape
    aw = A_HEADS * HEAD_DIM
    bkv = B_KV_HEADS * HEAD_DIM
    bw = B_KV_HEADS * GROUP * HEAD_DIM
    ckv = C_KV_HEADS * HEAD_DIM
    cw = C_KV_HEADS * GROUP * HEAD_DIM

    lb = jnp.cumsum(jax.nn.softmax(lb_gamma.astype(F32), axis=0), axis=0)[0]
    rope = _rope_tables(sl)

    nrow = -(-(sb + 1) // 8) * 8
    cond = jnp.zeros((nrow, d), F32).at[:sb].set(c).at[sb].set(c_ctx)
    xs = (x_prompt.reshape(pb * pl_, d), x_sample.reshape(sb * sl, d))
    tms = (_tile(pb * pl_, 1024), _tile(sl, 1024))
    rows_per_mod = (pb * pl_, sl)

    def mods(w_mod, b_mod):
        m = _adaln(cond, w_mod, b_mod)
        parts = [m[:, i * d:(i + 1) * d] for i in range(3)]
        return ([p[sb:sb + 1, None, :] for p in parts], [p[:sb, None, :] for p in parts])

    def tn_for(n):
        for t in (1536, 1280, 1024, 768, 512, 256, 128):
            if n % t == 0:
                return t
        return n

    mod_p, mod_s = mods(l0_w_mod, l0_b_mod)
    w_in = l0_w_in.astype(BF16)
    w_out = l0_w_out.astype(BF16)
    tn = tn_for(w_in.shape[1])
    u_p = _inproj(xs[0], l0_norm, mod_p[0], mod_p[1], w_in, rows_per_mod[0], tms[0], tn).reshape(pb, pl_, -1)
    u_s = _inproj(xs[1], l0_norm, mod_s[0], mod_s[1], w_in, rows_per_mod[1], tms[1], tn).reshape(sb, sl, -1)

    oa_p, new_state = _hgrn(u_p, lb, l0_a_onorm, None, hb=A_HEADS, emit_state=True)
    (oa_s,) = _hgrn(u_s, lb, l0_a_onorm, state_l0_hgrn, hb=4, emit_state=False)

    q_off = 5 * aw
    k_off = q_off + bw
    v_off = k_off + bkv
    g_off = v_off + bkv
    ob_p, k0, v0 = _attn(u_p, q_off, k_off, v_off, g_off, B_KV_HEADS, l0_b_qnorm, l0_b_knorm,
                         tq=128, nqb=2, emit_kv=True)
    (ob_s,) = _attn(u_s, q_off, k_off, v_off, g_off, B_KV_HEADS, l0_b_qnorm, l0_b_knorm,
                    tq=128, nqb=_tile(sl // 128, 8), ctx=(cache_l0_k, cache_l0_v), rope=rope)

    otm = (_tile(pb * pl_, 512), _tile(sl, 512))
    y_p = _outproj([oa_p.reshape(pb * pl_, aw), ob_p.reshape(pb * pl_, bw)], w_out, xs[0], mod_p[2],
                   rows_per_mod[0], otm[0])
    y_s = _outproj([oa_s.reshape(sb * sl, aw), ob_s.reshape(sb * sl, bw)], w_out, xs[1], mod_s[2],
                   rows_per_mod[1], otm[1])

    mod_p, mod_s = mods(l1_w_mod, l1_b_mod)
    w_in = l1_w_in.astype(BF16)
    w_out = l1_w_out.astype(BF16)
    tn = tn_for(w_in.shape[1])
    u_p = _inproj(y_p, l1_norm, mod_p[0], mod_p[1], w_in, rows_per_mod[0], tms[0], tn).reshape(pb, pl_, -1)
    u_s = _inproj(y_s, l1_norm, mod_s[0], mod_s[1], w_in, rows_per_mod[1], tms[1], tn).reshape(sb, sl, -1)

    k_off = cw
    v_off = k_off + ckv
    g_off = v_off + ckv
    oc_p, k1, v1 = _attn(u_p, 0, k_off, v_off, g_off, C_KV_HEADS, l1_c_qnorm, l1_c_knorm,
                         tq=128, nqb=2, sink=l1_c_sink, emit_kv=True)
    (oc_s,) = _attn(u_s, 0, k_off, v_off, g_off, C_KV_HEADS, l1_c_qnorm, l1_c_knorm,
                    tq=WINDOW, nqb=_tile(sl // WINDOW, 16), ctx=(cache_l1_k, cache_l1_v), rope=rope, window=True, sink=l1_c_sink)

    z_p = _outproj([oc_p.reshape(pb * pl_, cw)], w_out, y_p, mod_p[2], rows_per_mod[0], otm[0])
    z_s = _outproj([oc_s.reshape(sb * sl, cw)], w_out, y_s, mod_s[2], rows_per_mod[1], otm[1])

    return (z_p.reshape(pb, pl_, d), z_s.reshape(sb, sl, d), new_state,
            k0.reshape(pb, pl_, B_KV_HEADS, HEAD_DIM), v0.reshape(pb, pl_, B_KV_HEADS, HEAD_DIM),
            k1.reshape(pb, pl_, C_KV_HEADS, HEAD_DIM), v1.reshape(pb, pl_, C_KV_HEADS, HEAD_DIM))
```

```python
import functools
import math

import jax
import jax.numpy as jnp
import numpy as np
from jax import lax
from jax.experimental import pallas as pl
from jax.experimental.pallas import tpu as pltpu

F32 = jnp.float32
BF16 = jnp.bfloat16

HEAD_DIM = 128
GRID_W = 64
ROPE_QUARTER = HEAD_DIM // 4
ROPE_THETA = 10000.0
NORM_EPS = 1e-6
A_HEADS = 8
A_CHUNK = 32
B_KV_HEADS = 2
C_KV_HEADS = 4
GROUP = 4
WINDOW = 128
NEG = -1e30
LOG2E = math.log2(math.e)
VMEM_LIMIT = 56 * 1024 * 1024

_ROT_FIRST = (np.arange(HEAD_DIM) % (2 * ROPE_QUARTER)) < ROPE_QUARTER
_ROT_PARTNER = np.where(_ROT_FIRST, np.arange(HEAD_DIM) + ROPE_QUARTER, np.arange(HEAD_DIM) - ROPE_QUARTER)

_NT = (((1,), (1,)), ((), ()))
_TN = (((0,), (0,)), ((), ()))


def _params(*sem):
    return pltpu.CompilerParams(dimension_semantics=sem, vmem_limit_bytes=VMEM_LIMIT)


def _silu(x):
    return x * jax.nn.sigmoid(x)


def _rms(x, g):
    ms = jnp.mean(x * x, axis=-1, keepdims=True)
    return x * lax.rsqrt(ms + NORM_EPS) * g


def _adaln_kernel(c_ref, w_ref, b_ref, o_ref):
    a = _silu(c_ref[...]).astype(BF16)
    o_ref[...] = jnp.dot(a, w_ref[...].astype(BF16), preferred_element_type=F32) + b_ref[...]


def _adaln(cond, w_mod, b_mod):
    r, d = cond.shape
    n = w_mod.shape[1]
    tn = _tile(n, 1024)
    return pl.pallas_call(
        _adaln_kernel,
        grid=(n // tn,),
        in_specs=[pl.BlockSpec((r, d), lambda j: (0, 0)),
                  pl.BlockSpec((d, tn), lambda j: (0, j)),
                  pl.BlockSpec((1, tn), lambda j: (0, j))],
        out_specs=pl.BlockSpec((r, tn), lambda j: (0, j)),
        out_shape=jax.ShapeDtypeStruct((r, n), F32),
        compiler_params=_params("parallel"),
        name="adaln",
    )(cond, w_mod, b_mod.reshape(1, n))


def _inproj_kernel(x_ref, g_ref, sh_ref, sc_ref, w_ref, o_ref, h_ref, *, rc, n_tiles, norm_steps):
    i = pl.program_id(0)
    j = pl.program_id(1)
    slice_rows = x_ref.shape[0] // norm_steps

    def matmul():
        o_ref[...] = jnp.dot(h_ref[(i + 1) % 2], w_ref[...], preferred_element_type=F32).astype(o_ref.dtype)

    def norm_slice():
        gain = g_ref[...] * (1.0 + sc_ref[...])
        sh = sh_ref[...]
        for r in range(slice_rows // rc):
            rows = pl.ds(pl.multiple_of(j * slice_rows + r * rc, rc), rc)
            h_ref[i % 2, rows, :] = (_rms(x_ref[rows, :], gain) + sh).astype(BF16)

    has_mm = i >= 1
    has_norm = (i < n_tiles) & (j < norm_steps)

    @pl.when(has_mm & has_norm)
    def _():
        matmul()
        norm_slice()

    @pl.when(has_mm & jnp.logical_not(has_norm))
    def _():
        matmul()

    @pl.when(jnp.logical_not(has_mm) & has_norm)
    def _():
        norm_slice()


def _inproj(x2d, norm_g, shift, scale, w, rows_per_mod, tm, tn):
    m, d = x2d.shape
    n = w.shape[1]
    n_tiles = m // tm
    n_cols = n // tn
    norm_steps = max(s for s in (1, 2, 4, 8) if s <= n_cols and (tm // s) % 32 == 0)

    def tile(i):
        return jnp.minimum(i, n_tiles - 1)

    def col(i, j):
        return jnp.where(i == 0, 0, j)

    mod_spec = pl.BlockSpec((None, 1, d), lambda i, j: ((tile(i) * tm) // rows_per_mod, 0, 0))
    return pl.pallas_call(
        functools.partial(_inproj_kernel, rc=32, n_tiles=n_tiles, norm_steps=norm_steps),
        grid=(n_tiles + 1, n_cols),
        in_specs=[pl.BlockSpec((tm, d), lambda i, j: (tile(i), 0)),
                  pl.BlockSpec((1, d), lambda i, j: (0, 0)),
                  mod_spec, mod_spec,
                  pl.BlockSpec((d, tn), lambda i, j: (0, col(i, j)))],
        out_specs=pl.BlockSpec((tm, tn), lambda i, j: (jnp.maximum(i - 1, 0), col(i, j))),
        out_shape=jax.ShapeDtypeStruct((m, n), BF16),
        scratch_shapes=[pltpu.VMEM((2, tm, d), BF16)],
        compiler_params=_params("arbitrary", "arbitrary"),
        name="inproj",
    )(x2d, norm_g.reshape(1, d), shift, scale, w)


def _outproj_kernel(*refs, widths):
    o_refs = refs[:len(widths)]
    w_ref, x_ref, gt_ref, y_ref = refs[len(widths):]
    acc = None
    start = 0
    for o_ref, wd in zip(o_refs, widths):
        part = jnp.dot(o_ref[...], w_ref[start:start + wd, :], preferred_element_type=F32)
        acc = part if acc is None else acc + part
        start += wd
    y_ref[...] = x_ref[...] + gt_ref[...] * acc


def _outproj(os, w, x2d, gate, rows_per_mod, tm):
    m, d = x2d.shape
    widths = tuple(o.shape[1] for o in os)
    k = w.shape[0]
    return pl.pallas_call(
        functools.partial(_outproj_kernel, widths=widths),
        grid=(m // tm,),
        in_specs=[pl.BlockSpec((tm, wd), lambda i: (i, 0)) for wd in widths] + [
            pl.BlockSpec((k, d), lambda i: (0, 0)),
            pl.BlockSpec((tm, d), lambda i: (i, 0)),
            pl.BlockSpec((None, 1, d), lambda i: ((i * tm) // rows_per_mod, 0, 0))],
        out_specs=pl.BlockSpec((tm, d), lambda i: (i, 0)),
        out_shape=jax.ShapeDtypeStruct((m, d), F32),
        compiler_params=_params("parallel"),
        name="outproj",
    )(*os, w, x2d, gate)


SUBLANES = 8
HGRN_BLOCK = 128


def _chunk_cumprod(f, reverse):
    n = f.shape[0]
    tiles = n // SUBLANES
    per_chunk = A_CHUNK // SUBLANES
    x = f.reshape(tiles, SUBLANES, HEAD_DIM)
    sub = lax.broadcasted_iota(jnp.int32, x.shape, 1)
    s = 1
    while s < SUBLANES:
        if reverse:
            x = x * jnp.where(sub < SUBLANES - s, pltpu.roll(x, SUBLANES - s, axis=1), 1.0)
        else:
            x = x * jnp.where(sub >= s, pltpu.roll(x, s, axis=1), 1.0)
        s *= 2
    edge = 0 if reverse else SUBLANES - 1
    out, tot = [], []
    for c in range(n // A_CHUNK):
        ts = [x[c * per_chunk + i] for i in range(per_chunk)]
        order = range(per_chunk - 1, -1, -1) if reverse else range(per_chunk)
        carry = None
        done = {}
        for i in order:
            t = ts[i] if carry is None else ts[i] * carry
            done[i] = t
            carry = t[edge:edge + 1, :]
        out += [done[i] for i in range(per_chunk)]
        tot += [jnp.broadcast_to(carry, (A_CHUNK, HEAD_DIM))]
    return jnp.concatenate(out, axis=0), jnp.concatenate(tot, axis=0)


def _hgrn_kernel(*refs, seq, hb, has_s0, emit_state):
    q_ref, ff_ref, fb_ref, v_ref, g_ref, lb_ref, on_ref = refs[:7]
    pos = 7
    s0_ref = None
    if has_s0:
        s0_ref = refs[pos]
        pos += 1
    o_ref = refs[pos]
    pos += 1
    so_ref = None
    if emit_state:
        so_ref = refs[pos]
        pos += 1
    oacc_ref, st_ref, qd_ref, ke_ref, dec_ref = refs[pos:]

    nc = seq // A_CHUNK
    rb = HGRN_BLOCK
    cpb = rb // A_CHUNK
    ri = lax.broadcasted_iota(jnp.int32, (rb, rb), 0)
    ci = lax.broadcasted_iota(jnp.int32, (rb, rb), 1)
    shift = A_CHUNK.bit_length() - 1
    same_chunk = jnp.right_shift(ri, shift) == jnp.right_shift(ci, shift)
    masks = (same_chunk & (ri >= ci), same_chunk & (ci >= ri))

    for j in range(hb):
        for d in range(2):
            lanes = slice(d * HEAD_DIM, (d + 1) * HEAD_DIM)
            if has_s0:
                st_ref[j, :, lanes] = s0_ref[d, j].T
            else:
                st_ref[j, :, lanes] = jnp.zeros((HEAD_DIM, HEAD_DIM), F32)

    def stage1(blk, carry):
        rows = pl.ds(pl.multiple_of(blk * rb, rb), rb)
        heads = range(hb)
        chains = [(j, d) for j in heads for d in range(2)]
        cols = [slice(j * HEAD_DIM, (j + 1) * HEAD_DIM) for j in heads]
        qs = [_silu(q_ref[rows, cols[j]].astype(F32)) for j in heads]
        ts, fs, scans, qds, kinvs, atts = {}, {}, {}, {}, {}, {}
        for j, d in chains:
            fr = (fb_ref if d else ff_ref)[rows, cols[j]].astype(F32)
            lb = lb_ref[d:d + 1, cols[j]]
            ts[j, d] = (1.0 - lb) * jax.nn.sigmoid(fr)
            fs[j, d] = lb + ts[j, d]
        for j, d in chains:
            scans[j, d] = _chunk_cumprod(fs[j, d], reverse=bool(d))
        for j, d in chains:
            eb, dec = scans[j, d]
            lb = lb_ref[d:d + 1, cols[j]]
            qds[j, d] = (qs[j] * eb).astype(BF16)
            kinvs[j, d] = ((1.0 - lb) - ts[j, d]) * (1.0 / eb)
            qd_ref[d, j, rows, :] = qds[j, d]
            ke_ref[d, j, rows, :] = (kinvs[j, d] * dec).astype(BF16)
            for c in range(cpb):
                dec_ref[d, j, pl.ds(blk * cpb + c, 1), :] = dec[c * A_CHUNK:c * A_CHUNK + 1, :]
        for j, d in chains:
            att = lax.dot_general(qds[j, d], kinvs[j, d].astype(BF16), _NT, preferred_element_type=F32)
            atts[j, d] = jnp.where(masks[d], att, 0.0).astype(BF16)
        for j in heads:
            v = v_ref[rows, cols[j]]
            oacc_ref[rows, cols[j]] = (jnp.dot(atts[j, 0], v, preferred_element_type=F32)
                                       + jnp.dot(atts[j, 1], v, preferred_element_type=F32))
        return carry

    zero = jnp.zeros((A_CHUNK, HEAD_DIM), BF16)

    def blockdiag(a, b):
        return jnp.concatenate([jnp.concatenate([a, zero], axis=1), jnp.concatenate([zero, b], axis=1)], axis=0)

    def stage2(n, carry):
        cf, cb = n, nc - 1 - n
        rows_f = pl.ds(pl.multiple_of(cf * A_CHUNK, A_CHUNK), A_CHUNK)
        rows_b = pl.ds(pl.multiple_of(cb * A_CHUNK, A_CHUNK), A_CHUNK)
        for j in range(hb):
            cols = slice(j * HEAD_DIM, (j + 1) * HEAD_DIM)
            st = st_ref[j]
            qd = blockdiag(qd_ref[0, j, rows_f, :], qd_ref[1, j, rows_b, :])
            ke = blockdiag(ke_ref[0, j, rows_f, :], ke_ref[1, j, rows_b, :])
            v = jnp.concatenate([v_ref[rows_f, cols], v_ref[rows_b, cols]], axis=0)
            dec = jnp.concatenate([dec_ref[0, j, pl.ds(cf, 1), :], dec_ref[1, j, pl.ds(cb, 1), :]], axis=1)
            o = lax.dot_general(qd, st.astype(BF16), _NT, preferred_element_type=F32)
            st_ref[j] = st * dec + lax.dot_general(v, ke, _TN, preferred_element_type=F32)
            oacc_ref[rows_f, cols] += o[:A_CHUNK, :]
            oacc_ref[rows_b, cols] += o[A_CHUNK:, :]
        return carry

    nb = seq // rb
    mid = nb // 2
    on = on_ref[...]

    def merged(p, carry):
        stage1(p, carry)
        stage1(nb - 1 - p, carry)
        for i in range(cpb):
            stage2(p * cpb + i, carry)
        return carry

    def finish_block(r):
        rows = pl.ds(pl.multiple_of(r * rb, rb), rb)
        for j in range(hb):
            cols = slice(j * HEAD_DIM, (j + 1) * HEAD_DIM)
            y = _rms(oacc_ref[rows, cols], on)
            o_ref[rows, cols] = (y * _silu(g_ref[rows, cols].astype(F32))).astype(o_ref.dtype)

    def tail(g, carry):
        finish_block(mid - g)
        finish_block(mid + g - 1)
        for i in range(cpb):
            stage2(nc // 2 + g * cpb + i, carry)
        return carry

    lax.fori_loop(0, mid, merged, 0)
    for i in range(cpb):
        stage2(nc // 2 + i, 0)
    lax.fori_loop(1, mid, tail, 0)
    finish_block(0)
    finish_block(nb - 1)

    if emit_state:
        for j in range(hb):
            for d in range(2):
                so_ref[d, j] = st_ref[j, :, d * HEAD_DIM:(d + 1) * HEAD_DIM].T


def _hgrn(u3, lb, onorm, s0, hb, emit_state):
    bsz, seq, _ = u3.shape
    bw = hb * HEAD_DIM
    nh = A_HEADS // hb

    def seg(s):
        return pl.BlockSpec((None, seq, bw), lambda b, h: (b, 0, s * nh + h))

    in_specs = [seg(0), seg(1), seg(2), seg(3), seg(4),
                pl.BlockSpec((2, bw), lambda b, h: (0, h)),
                pl.BlockSpec((1, HEAD_DIM), lambda b, h: (0, 0))]
    args = [u3, u3, u3, u3, u3, lb, onorm.reshape(1, HEAD_DIM)]
    st_spec = pl.BlockSpec((None, 2, hb, HEAD_DIM, HEAD_DIM), lambda b, h: (b, 0, h, 0, 0))
    if s0 is not None:
        in_specs.append(st_spec)
        args.append(s0)
    out_specs = [pl.BlockSpec((None, seq, bw), lambda b, h: (b, 0, h))]
    out_shape = [jax.ShapeDtypeStruct((bsz, seq, A_HEADS * HEAD_DIM), BF16)]
    if emit_state:
        out_specs.append(st_spec)
        out_shape.append(jax.ShapeDtypeStruct((bsz, 2, A_HEADS, HEAD_DIM, HEAD_DIM), F32))
    return pl.pallas_call(
        functools.partial(_hgrn_kernel, seq=seq, hb=hb, has_s0=s0 is not None,
                          emit_state=emit_state),
        grid=(bsz, nh),
        in_specs=in_specs,
        out_specs=out_specs,
        out_shape=out_shape,
        scratch_shapes=[pltpu.VMEM((seq, bw), F32),
                        pltpu.VMEM((hb, HEAD_DIM, 2 * HEAD_DIM), F32),
                        pltpu.VMEM((2, hb, seq, HEAD_DIM), BF16),
                        pltpu.VMEM((2, hb, seq, HEAD_DIM), BF16),
                        pltpu.VMEM((2, hb, seq // A_CHUNK, HEAD_DIM), F32)],
        compiler_params=_params("parallel", "parallel"),
        name="hgrn",
    )(*args)


def _norm_rope(xb, g_cos, gp_sin, rot_ref, out_scale):
    x = xb.astype(F32)
    sq = (x * x).astype(BF16)
    y = x * g_cos
    if gp_sin is not None:
        both = jnp.dot(jnp.concatenate([xb, sq], axis=1), rot_ref[...], preferred_element_type=F32)
        y = y + both[:, :HEAD_DIM] * gp_sin
        ms = both[:, HEAD_DIM:]
    else:
        mean_mat = jnp.full((HEAD_DIM, HEAD_DIM), 1.0 / HEAD_DIM, BF16)
        ms = jnp.dot(sq, mean_mat, preferred_element_type=F32)
    r = lax.rsqrt(ms + NORM_EPS)
    if out_scale != 1.0:
        r = r * out_scale
    return y * r


def _attn_kernel(*refs, seq, tq, nqb, ctx_len, rope, window, sink, emit_kv):
    refs = list(refs)
    q_ref, gate_ref, k_ref, v_ref = refs[:4]
    pos = 4
    kc_ref = vc_ref = cos_ref = sin_ref = rot_ref = bias_ref = sink_ref = ko_ref = vo_ref = None
    if ctx_len:
        kc_ref, vc_ref = refs[pos:pos + 2]
        pos += 2
    if rope:
        cos_ref, sin_ref, rot_ref = refs[pos:pos + 3]
        pos += 3
    if window:
        bias_ref = refs[pos]
        pos += 1
    qn_ref, kn_ref = refs[pos:pos + 2]
    pos += 2
    if sink:
        sink_ref = refs[pos]
        pos += 1
    o_ref = refs[pos]
    pos += 1
    if emit_kv:
        ko_ref, vo_ref = refs[pos:pos + 2]
        pos += 2
    ks_ref, vs_ref = refs[pos:pos + 2]
    pos += 2
    qc_ref = qs_ref = None
    if rope:
        qc_ref, qs_ref = refs[pos:]

    kvh = pl.program_id(1)
    step = pl.program_id(2)
    nq = seq // tq
    pad = WINDOW if window else 0
    ctx_off = seq + 2 * pad
    qscale = LOG2E / math.sqrt(HEAD_DIM)

    @pl.when(step == 0)
    def _():
        if window:
            zeros = jnp.zeros((pad, HEAD_DIM), BF16)
            for ref in (ks_ref, vs_ref):
                ref[0:pad, 0:HEAD_DIM] = zeros
                ref[pad + seq:ctx_off, 0:HEAD_DIM] = zeros
        rc = 256
        kg = kn_ref[0:1, :]
        kgp = kn_ref[1:2, :]
        ones = jnp.ones((rc, HEAD_DIM), BF16)

        def body(r, carry):
            rows = pl.ds(pl.multiple_of(r * rc, rc), rc)
            dst = pl.ds(pl.multiple_of(pad + r * rc, HEAD_DIM), rc)
            if rope:
                cos = cos_ref[rows, :]
                sin = sin_ref[rows, :]
                qc_ref[rows, :] = cos * qn_ref[0:1, :]
                qs_ref[rows, :] = sin * qn_ref[1:2, :]
                k = _norm_rope(k_ref[rows, :], cos * kg, sin * kgp, rot_ref, 1.0)
            else:
                k = _norm_rope(k_ref[rows, :], kg, None, None, 1.0)
            if emit_kv:
                ko_ref[rows, :] = k
                vo_ref[rows, :] = v_ref[rows, :].astype(F32)
            ks_ref[dst, :] = k.astype(BF16)
            vs_ref[dst, 0:HEAD_DIM] = v_ref[rows, :]
            return carry

        lax.fori_loop(0, seq // rc, body, 0)
        if ctx_len:
            ks_ref[ctx_off:ctx_off + ctx_len, :] = kc_ref[...].astype(BF16)
            vs_ref[ctx_off:ctx_off + ctx_len, 0:HEAD_DIM] = vc_ref[...].astype(BF16)

        def fill(r, carry):
            vs_ref[pl.ds(pl.multiple_of(r * rc, rc), rc), HEAD_DIM:2 * HEAD_DIM] = ones
            return carry

        lax.fori_loop(0, vs_ref.shape[0] // rc, fill, 0)

    sink_col = None
    if sink:
        sink_col = jnp.concatenate(
            [jnp.full((tq, 1), sink_ref[kvh * GROUP + h] * LOG2E, F32) for h in range(GROUP)], axis=0)

    def prep(t):
        qi = step * nqb + t
        qrows = pl.ds(pl.multiple_of(qi * tq, tq), tq)
        xs = []
        for h in range(GROUP):
            xb = q_ref[t * tq:(t + 1) * tq, h * HEAD_DIM:(h + 1) * HEAD_DIM]
            if rope:
                x = _norm_rope(xb, qc_ref[qrows, :], qs_ref[qrows, :], rot_ref, qscale)
            else:
                x = _norm_rope(xb, qn_ref[0:1, :], None, None, qscale)
            xs.append(x.astype(BF16))
        return jnp.concatenate(xs, axis=0)

    def key_parts(t):
        qi = step * nqb + t
        if not window:
            return [(slice(None), None)]
        start = pl.multiple_of(qi * tq, HEAD_DIM)
        edge = jnp.where(qi == 0, 0, jnp.where(qi == nq - 1, 2, 1))
        parts = [(pl.ds(start, 3 * WINDOW), bias_ref[edge])]
        if ctx_len:
            parts.append((slice(ctx_off, ctx_off + ctx_len), None))
        return parts

    def logits(q, parts):
        ss = []
        for rows, bias in parts:
            s = lax.dot_general(q, ks_ref[rows, :], _NT, preferred_element_type=F32)
            ss.append(s if bias is None else s + bias)
        return ss

    def row_max(ss):
        slabs = [s[:, c:c + HEAD_DIM] for s in ss for c in range(0, s.shape[1], HEAD_DIM)]
        return jnp.max(functools.reduce(jnp.maximum, slabs), axis=-1, keepdims=True)

    def weighted(ss, m, parts):
        acc = None
        for s, (rows, _) in zip(ss, parts):
            pv = jnp.dot(jnp.exp2(s - m).astype(BF16), vs_ref[rows, :], preferred_element_type=F32)
            acc = pv if acc is None else acc + pv
        return acc

    def finish(t, acc, m):
        l = acc[:, HEAD_DIM:]
        if sink:
            l = l + jnp.exp2(sink_col - m)
        for h in range(GROUP):
            cols = slice(h * HEAD_DIM, (h + 1) * HEAD_DIM)
            rows = slice(h * tq, (h + 1) * tq)
            gate = gate_ref[t * tq:(t + 1) * tq, cols].astype(F32)
            inv = 1.0 / (l[rows, :] * (1.0 + jnp.exp(-gate)))
            o_ref[t * tq:(t + 1) * tq, cols] = (acc[rows, :HEAD_DIM] * gate * inv).astype(o_ref.dtype)

    blocks = range(nqb)
    parts = [key_parts(t) for t in blocks]
    ss, ms = {}, {}
    q_next = prep(0)
    for t in blocks:
        q = q_next
        if t + 1 < nqb:
            q_next = prep(t + 1)
        ss[t] = logits(q, parts[t])
        ms[t] = row_max(ss[t])
        if t >= 1:
            finish(t - 1, weighted(ss.pop(t - 1), ms[t - 1], parts[t - 1]), ms.pop(t - 1))
    finish(nqb - 1, weighted(ss[nqb - 1], ms[nqb - 1], parts[nqb - 1]), ms[nqb - 1])


def _window_bias(tq):
    r = (jnp.arange(GROUP * tq) % tq)[:, None]
    c = jnp.arange(3 * WINDOW)[None, :]
    band = (c >= r) & (c <= r + 2 * WINDOW)
    keep = jnp.stack([band & (c >= WINDOW), band, band & (c < 2 * WINDOW)])
    return jnp.where(keep, 0.0, NEG).astype(F32)


def _attn(u3, q_off, k_off, v_off, g_off, kv_heads, qnorm, knorm, *, tq, nqb, ctx=None, rope=None,
          window=False, sink=None, emit_kv=False):
    bsz, seq, _ = u3.shape
    gw = GROUP * HEAD_DIM
    heads = kv_heads * GROUP
    ctx_len = 0 if ctx is None else ctx[0].shape[1]
    pad = WINDOW if window else 0
    assert not window or (tq == WINDOW and seq // tq >= 2)
    assert seq % (tq * nqb) == 0

    def wide(off):
        return pl.BlockSpec((None, tq * nqb, gw), lambda b, h, i: (b, i, off // gw + h))

    def narrow(off, rows):
        return pl.BlockSpec((None, rows, HEAD_DIM), lambda b, h, i: (b, 0, off // HEAD_DIM + h))

    def whole(shape):
        return pl.BlockSpec(shape, lambda b, h, i: (0,) * len(shape))

    in_specs = [wide(q_off), wide(g_off), narrow(k_off, seq), narrow(v_off, seq)]
    args = [u3, u3, u3, u3]
    if ctx is not None:
        for cache in ctx:
            in_specs.append(narrow(0, ctx_len))
            args.append(cache.reshape(bsz, ctx_len, kv_heads * HEAD_DIM))
    if rope is not None:
        for t in rope:
            in_specs.append(whole(t.shape))
            args.append(t)
    if window:
        in_specs.append(whole((3, GROUP * tq, 3 * WINDOW)))
        args.append(_window_bias(tq))
    in_specs += [whole((2, HEAD_DIM)), whole((2, HEAD_DIM))]
    args += [jnp.stack([qnorm, qnorm[_ROT_PARTNER]]), jnp.stack([knorm, knorm[_ROT_PARTNER]])]
    if sink is not None:
        in_specs.append(pl.BlockSpec(memory_space=pltpu.SMEM))
        args.append(sink)
    out_specs = [pl.BlockSpec((None, tq * nqb, gw), lambda b, h, i: (b, i, h))]
    out_shape = [jax.ShapeDtypeStruct((bsz, seq, heads * HEAD_DIM), BF16)]
    if emit_kv:
        for _ in range(2):
            out_specs.append(narrow(0, seq))
            out_shape.append(jax.ShapeDtypeStruct((bsz, seq, kv_heads * HEAD_DIM), F32))
    rows = seq + 2 * pad + ctx_len
    kern = functools.partial(_attn_kernel, seq=seq, tq=tq, nqb=nqb, ctx_len=ctx_len,
                             rope=rope is not None, window=window, sink=sink is not None,
                             emit_kv=emit_kv)
    return pl.pallas_call(
        kern,
        grid=(bsz, kv_heads, seq // (tq * nqb)),
        in_specs=in_specs,
        out_specs=out_specs,
        out_shape=out_shape,
        scratch_shapes=[pltpu.VMEM((rows, HEAD_DIM), BF16),
                        pltpu.VMEM((rows, 2 * HEAD_DIM), BF16)]
        + ([pltpu.VMEM((seq, HEAD_DIM), F32)] * 2 if rope is not None else []),
        compiler_params=_params("parallel", "parallel", "arbitrary"),
        name="attn",
    )(*args)


def _rope_tables(n_tokens):
    rows = n_tokens // GRID_W
    row = jnp.repeat(jnp.arange(rows), GRID_W).astype(F32)
    col = (jnp.arange(rows * GRID_W) % GRID_W).astype(F32)
    inv = ROPE_THETA ** (-jnp.arange(ROPE_QUARTER, dtype=F32) / ROPE_QUARTER)
    ar = row[:, None] * inv
    ac = col[:, None] * inv
    ang = jnp.concatenate([ar, ar, ac, ac], axis=-1)
    rot = np.zeros((2 * HEAD_DIM, 2 * HEAD_DIM), np.float32)
    rot[_ROT_PARTNER, np.arange(HEAD_DIM)] = np.where(_ROT_FIRST, -1.0, 1.0)
    rot[HEAD_DIM:, HEAD_DIM:] = 1.0 / HEAD_DIM
    return jnp.cos(ang), jnp.sin(ang), jnp.asarray(rot, BF16)


def _tile(m, pref):
    t = pref
    while m % t:
        t //= 2
    return t


def kernel(x_prompt, x_sample, state_l0_hgrn, cache_l0_k, cache_l0_v, cache_l1_k, cache_l1_v, c, c_ctx, lb_gamma, l0_norm, l0_w_mod, l0_b_mod, l0_w_in, l0_w_out, l0_a_onorm, l0_b_qnorm, l0_b_knorm, l1_norm, l1_w_mod, l1_b_mod, l1_w_in, l1_w_out, l1_c_qnorm, l1_c_knorm, l1_c_sink):
    pb, pl_, d = x_prompt.shape
    sb, sl, _ = x_sample.shape
    aw = A_HEADS * HEAD_DIM
    bkv = B_KV_HEADS * HEAD_DIM
    bw = B_KV_HEADS * GROUP * HEAD_DIM
    ckv = C_KV_HEADS * HEAD_DIM
    cw = C_KV_HEADS * GROUP * HEAD_DIM

    lb = jnp.cumsum(jax.nn.softmax(lb_gamma.astype(F32), axis=0), axis=0)[0]
    rope = _rope_tables(sl)

    nrow = -(-(sb + 1) // 8) * 8
    cond = jnp.zeros((nrow, d), F32).at[:sb].set(c).at[sb].set(c_ctx)
    xs = (x_prompt.reshape(pb * pl_, d), x_sample.reshape(sb * sl, d))
    tms = (_tile(pb * pl_, 1024), _tile(sl, 1024))
    rows_per_mod = (pb * pl_, sl)

    def mods(w_mod, b_mod):
        m = _adaln(cond, w_mod, b_mod)
        parts = [m[:, i * d:(i + 1) * d] for i in range(3)]
        return ([p[sb:sb + 1, None, :] for p in parts], [p[:sb, None, :] for p in parts])

    def tn_for(n):
        for t in (1536, 1280, 1024, 768, 512, 256, 128):
            if n % t == 0:
                return t
        return n

    mod_p, mod_s = mods(l0_w_mod, l0_b_mod)
    w_in = l0_w_in.astype(BF16)
    w_out = l0_w_out.astype(BF16)
    tn = tn_for(w_in.shape[1])
    u_p = _inproj(xs[0], l0_norm, mod_p[0], mod_p[1], w_in, rows_per_mod[0], tms[0], tn).reshape(pb, pl_, -1)
    u_s = _inproj(xs[1], l0_norm, mod_s[0], mod_s[1], w_in, rows_per_mod[1], tms[1], tn).reshape(sb, sl, -1)

    oa_p, new_state = _hgrn(u_p, lb, l0_a_onorm, None, hb=A_HEADS, emit_state=True)
    (oa_s,) = _hgrn(u_s, lb, l0_a_onorm, state_l0_hgrn, hb=4, emit_state=False)

    q_off = 5 * aw
    k_off = q_off + bw
    v_off = k_off + bkv
    g_off = v_off + bkv
    ob_p, k0, v0 = _attn(u_p, q_off, k_off, v_off, g_off, B_KV_HEADS, l0_b_qnorm, l0_b_knorm,
                         tq=128, nqb=2, emit_kv=True)
    (ob_s,) = _attn(u_s, q_off, k_off, v_off, g_off, B_KV_HEADS, l0_b_qnorm, l0_b_knorm,
                    tq=128, nqb=_tile(sl // 128, 8), ctx=(cache_l0_k, cache_l0_v), rope=rope)

    otm = (_tile(pb * pl_, 512), _tile(sl, 512))
    y_p = _outproj([oa_p.reshape(pb * pl_, aw), ob_p.reshape(pb * pl_, bw)], w_out, xs[0], mod_p[2],
                   rows_per_mod[0], otm[0])
    y_s = _outproj([oa_s.reshape(sb * sl, aw), ob_s.reshape(sb * sl, bw)], w_out, xs[1], mod_s[2],
                   rows_per_mod[1], otm[1])

    mod_p, mod_s = mods(l1_w_mod, l1_b_mod)
    w_in = l1_w_in.astype(BF16)
    w_out = l1_w_out.astype(BF16)
    tn = tn_for(w_in.shape[1])
    u_p = _inproj(y_p, l1_norm, mod_p[0], mod_p[1], w_in, rows_per_mod[0], tms[0], tn).reshape(pb, pl_, -1)
    u_s = _inproj(y_s, l1_norm, mod_s[0], mod_s[1], w_in, rows_per_mod[1], tms[1], tn).reshape(sb, sl, -1)

    k_off = cw
    v_off = k_off + ckv
    g_off = v_off + ckv
    oc_p, k1, v1 = _attn(u_p, 0, k_off, v_off, g_off, C_KV_HEADS, l1_c_qnorm, l1_c_knorm,
                         tq=128, nqb=2, sink=l1_c_sink, emit_kv=True)
    (oc_s,) = _attn(u_s, 0, k_off, v_off, g_off, C_KV_HEADS, l1_c_qnorm, l1_c_knorm,
                    tq=WINDOW, nqb=_tile(sl // WINDOW, 16), ctx=(cache_l1_k, cache_l1_v), rope=rope, window=True, sink=l1_c_sink)

    z_p = _outproj([oc_p.reshape(pb * pl_, cw)], w_out, y_p, mod_p[2], rows_per_mod[0], otm[0])
    z_s = _outproj([oc_s.reshape(sb * sl, cw)], w_out, y_s, mod_s[2], rows_per_mod[1], otm[1])

    return (z_p.reshape(pb, pl_, d), z_s.reshape(sb, sl, d), new_state,
            k0.reshape(pb, pl_, B_KV_HEADS, HEAD_DIM), v0.reshape(pb, pl_, B_KV_HEADS, HEAD_DIM),
            k1.reshape(pb, pl_, C_KV_HEADS, HEAD_DIM), v1.reshape(pb, pl_, C_KV_HEADS, HEAD_DIM))
```

```python
import functools
import math

import jax
import jax.numpy as jnp
import numpy as np
from jax import lax
from jax.experimental import pallas as pl
from jax.experimental.pallas import tpu as pltpu

F32 = jnp.float32
BF16 = jnp.bfloat16

HEAD_DIM = 128
GRID_W = 64
ROPE_QUARTER = HEAD_DIM // 4
ROPE_THETA = 10000.0
NORM_EPS = 1e-6
A_HEADS = 8
A_CHUNK = 32
B_KV_HEADS = 2
C_KV_HEADS = 4
GROUP = 4
WINDOW = 128
NEG = -1e30
LOG2E = math.log2(math.e)
MAX_UNSHIFTED_LOGIT = 60.0
LOGIT_BOUND_MARGIN = 1.1
VMEM_LIMIT = 56 * 1024 * 1024

_ROT_FIRST = (np.arange(HEAD_DIM) % (2 * ROPE_QUARTER)) < ROPE_QUARTER
_ROT_PARTNER = np.where(_ROT_FIRST, np.arange(HEAD_DIM) + ROPE_QUARTER, np.arange(HEAD_DIM) - ROPE_QUARTER)

_NT = (((1,), (1,)), ((), ()))
_TN = (((0,), (0,)), ((), ()))


def _params(*sem):
    return pltpu.CompilerParams(dimension_semantics=sem, vmem_limit_bytes=VMEM_LIMIT)


def _silu(x):
    return x * jax.nn.sigmoid(x)


def _rms(x, g):
    ms = jnp.mean(x * x, axis=-1, keepdims=True)
    return x * lax.rsqrt(ms + NORM_EPS) * g


def _adaln_kernel(c_ref, w_ref, b_ref, o_ref):
    a = _silu(c_ref[...]).astype(BF16)
    o_ref[...] = jnp.dot(a, w_ref[...].astype(BF16), preferred_element_type=F32) + b_ref[...]


def _adaln(cond, w_mod, b_mod):
    r, d = cond.shape
    n = w_mod.shape[1]
    tn = _tile(n, 1024)
    return pl.pallas_call(
        _adaln_kernel,
        grid=(n // tn,),
        in_specs=[pl.BlockSpec((r, d), lambda j: (0, 0)),
                  pl.BlockSpec((d, tn), lambda j: (0, j)),
                  pl.BlockSpec((1, tn), lambda j: (0, j))],
        out_specs=pl.BlockSpec((r, tn), lambda j: (0, j)),
        out_shape=jax.ShapeDtypeStruct((r, n), F32),
        compiler_params=_params("parallel"),
        name="adaln",
    )(cond, w_mod, b_mod.reshape(1, n))


def _inproj_kernel(x_ref, g_ref, sh_ref, sc_ref, w_ref, o_ref, h_ref, *, rc, n_tiles, norm_steps):
    i = pl.program_id(0)
    j = pl.program_id(1)
    slice_rows = x_ref.shape[0] // norm_steps

    def matmul():
        o_ref[...] = jnp.dot(h_ref[(i + 1) % 2], w_ref[...], preferred_element_type=F32).astype(o_ref.dtype)

    def norm_slice():
        gain = g_ref[...] * (1.0 + sc_ref[...])
        sh = sh_ref[...]
        for r in range(slice_rows // rc):
            rows = pl.ds(pl.multiple_of(j * slice_rows + r * rc, rc), rc)
            h_ref[i % 2, rows, :] = (_rms(x_ref[rows, :], gain) + sh).astype(BF16)

    has_mm = i >= 1
    has_norm = (i < n_tiles) & (j < norm_steps)

    @pl.when(has_mm & has_norm)
    def _():
        matmul()
        norm_slice()

    @pl.when(has_mm & jnp.logical_not(has_norm))
    def _():
        matmul()

    @pl.when(jnp.logical_not(has_mm) & has_norm)
    def _():
        norm_slice()


def _inproj(x2d, norm_g, shift, scale, w, rows_per_mod, tm, tn):
    m, d = x2d.shape
    n = w.shape[1]
    n_tiles = m // tm
    n_cols = n // tn
    norm_steps = max(s for s in (1, 2, 4, 8) if s <= n_cols and (tm // s) % 32 == 0)

    def tile(i):
        return jnp.minimum(i, n_tiles - 1)

    def col(i, j):
        return jnp.where(i == 0, 0, j)

    mod_spec = pl.BlockSpec((None, 1, d), lambda i, j: ((tile(i) * tm) // rows_per_mod, 0, 0))
    return pl.pallas_call(
        functools.partial(_inproj_kernel, rc=32, n_tiles=n_tiles, norm_steps=norm_steps),
        grid=(n_tiles + 1, n_cols),
        in_specs=[pl.BlockSpec((tm, d), lambda i, j: (tile(i), 0)),
                  pl.BlockSpec((1, d), lambda i, j: (0, 0)),
                  mod_spec, mod_spec,
                  pl.BlockSpec((d, tn), lambda i, j: (0, col(i, j)))],
        out_specs=pl.BlockSpec((tm, tn), lambda i, j: (jnp.maximum(i - 1, 0), col(i, j))),
        out_shape=jax.ShapeDtypeStruct((m, n), BF16),
        scratch_shapes=[pltpu.VMEM((2, tm, d), BF16)],
        compiler_params=_params("arbitrary", "arbitrary"),
        name="inproj",
    )(x2d, norm_g.reshape(1, d), shift, scale, w)


def _outproj_kernel(*refs, widths):
    o_refs = refs[:len(widths)]
    w_ref, x_ref, gt_ref, y_ref = refs[len(widths):]
    acc = None
    start = 0
    for o_ref, wd in zip(o_refs, widths):
        part = jnp.dot(o_ref[...], w_ref[start:start + wd, :], preferred_element_type=F32)
        acc = part if acc is None else acc + part
        start += wd
    y_ref[...] = x_ref[...] + gt_ref[...] * acc


def _outproj(os, w, x2d, gate, rows_per_mod, tm):
    m, d = x2d.shape
    widths = tuple(o.shape[1] for o in os)
    k = w.shape[0]
    return pl.pallas_call(
        functools.partial(_outproj_kernel, widths=widths),
        grid=(m // tm,),
        in_specs=[pl.BlockSpec((tm, wd), lambda i: (i, 0)) for wd in widths] + [
            pl.BlockSpec((k, d), lambda i: (0, 0)),
            pl.BlockSpec((tm, d), lambda i: (i, 0)),
            pl.BlockSpec((None, 1, d), lambda i: ((i * tm) // rows_per_mod, 0, 0))],
        out_specs=pl.BlockSpec((tm, d), lambda i: (i, 0)),
        out_shape=jax.ShapeDtypeStruct((m, d), F32),
        compiler_params=_params("parallel"),
        name="outproj",
    )(*os, w, x2d, gate)


SUBLANES = 8
HGRN_BLOCK = 128


def _chunk_cumprod(f, reverse):
    n = f.shape[0]
    tiles = n // SUBLANES
    per_chunk = A_CHUNK // SUBLANES
    x = f.reshape(tiles, SUBLANES, HEAD_DIM)
    sub = lax.broadcasted_iota(jnp.int32, x.shape, 1)
    s = 1
    while s < SUBLANES:
        if reverse:
            x = x * jnp.where(sub < SUBLANES - s, pltpu.roll(x, SUBLANES - s, axis=1), 1.0)
        else:
            x = x * jnp.where(sub >= s, pltpu.roll(x, s, axis=1), 1.0)
        s *= 2
    edge = 0 if reverse else SUBLANES - 1
    out, tot = [], []
    for c in range(n // A_CHUNK):
        ts = [x[c * per_chunk + i] for i in range(per_chunk)]
        order = range(per_chunk - 1, -1, -1) if reverse else range(per_chunk)
        carry = None
        done = {}
        for i in order:
            t = ts[i] if carry is None else ts[i] * carry
            done[i] = t
            carry = t[edge:edge + 1, :]
        out += [done[i] for i in range(per_chunk)]
        tot += [jnp.broadcast_to(carry, (A_CHUNK, HEAD_DIM))]
    return jnp.concatenate(out, axis=0), jnp.concatenate(tot, axis=0)


def _hgrn_kernel(*refs, seq, hb, has_s0, emit_state):
    q_ref, ff_ref, fb_ref, v_ref, g_ref, lb_ref, on_ref = refs[:7]
    pos = 7
    s0_ref = None
    if has_s0:
        s0_ref = refs[pos]
        pos += 1
    o_ref = refs[pos]
    pos += 1
    so_ref = None
    if emit_state:
        so_ref = refs[pos]
        pos += 1
    oacc_ref, st_ref, qd_ref, ke_ref, dec_ref = refs[pos:]

    nc = seq // A_CHUNK
    rb = HGRN_BLOCK
    cpb = rb // A_CHUNK
    ri = lax.broadcasted_iota(jnp.int32, (rb, rb), 0)
    ci = lax.broadcasted_iota(jnp.int32, (rb, rb), 1)
    shift = A_CHUNK.bit_length() - 1
    same_chunk = jnp.right_shift(ri, shift) == jnp.right_shift(ci, shift)
    masks = (same_chunk & (ri >= ci), same_chunk & (ci >= ri))

    for j in range(hb):
        for d in range(2):
            lanes = slice(d * HEAD_DIM, (d + 1) * HEAD_DIM)
            if has_s0:
                st_ref[j, :, lanes] = s0_ref[d, j].T
            else:
                st_ref[j, :, lanes] = jnp.zeros((HEAD_DIM, HEAD_DIM), F32)

    def stage1(blk, carry):
        rows = pl.ds(pl.multiple_of(blk * rb, rb), rb)
        heads = range(hb)
        chains = [(j, d) for j in heads for d in range(2)]
        cols = [slice(j * HEAD_DIM, (j + 1) * HEAD_DIM) for j in heads]
        qs = [_silu(q_ref[rows, cols[j]].astype(F32)) for j in heads]
        ts, fs, scans, qds, kinvs, atts = {}, {}, {}, {}, {}, {}
        for j, d in chains:
            fr = (fb_ref if d else ff_ref)[rows, cols[j]].astype(F32)
            lb = lb_ref[d:d + 1, cols[j]]
            ts[j, d] = (1.0 - lb) * jax.nn.sigmoid(fr)
            fs[j, d] = lb + ts[j, d]
        for j, d in chains:
            scans[j, d] = _chunk_cumprod(fs[j, d], reverse=bool(d))
        for j, d in chains:
            eb, dec = scans[j, d]
            lb = lb_ref[d:d + 1, cols[j]]
            qds[j, d] = (qs[j] * eb).astype(BF16)
            kinvs[j, d] = ((1.0 - lb) - ts[j, d]) * (1.0 / eb)
            qd_ref[d, j, rows, :] = qds[j, d]
            ke_ref[d, j, rows, :] = (kinvs[j, d] * dec).astype(BF16)
            for c in range(cpb):
                dec_ref[d, j, pl.ds(blk * cpb + c, 1), :] = dec[c * A_CHUNK:c * A_CHUNK + 1, :]
        for j, d in chains:
            att = lax.dot_general(qds[j, d], kinvs[j, d].astype(BF16), _NT, preferred_element_type=F32)
            atts[j, d] = jnp.where(masks[d], att, 0.0).astype(BF16)
        for j in heads:
            v = v_ref[rows, cols[j]]
            oacc_ref[rows, cols[j]] = (jnp.dot(atts[j, 0], v, preferred_element_type=F32)
                                       + jnp.dot(atts[j, 1], v, preferred_element_type=F32))
        return carry

    zero = jnp.zeros((A_CHUNK, HEAD_DIM), BF16)

    def blockdiag(a, b):
        return jnp.concatenate([jnp.concatenate([a, zero], axis=1), jnp.concatenate([zero, b], axis=1)], axis=0)

    def stage2(n, carry):
        cf, cb = n, nc - 1 - n
        rows_f = pl.ds(pl.multiple_of(cf * A_CHUNK, A_CHUNK), A_CHUNK)
        rows_b = pl.ds(pl.multiple_of(cb * A_CHUNK, A_CHUNK), A_CHUNK)
        for j in range(hb):
            cols = slice(j * HEAD_DIM, (j + 1) * HEAD_DIM)
            st = st_ref[j]
            qd = blockdiag(qd_ref[0, j, rows_f, :], qd_ref[1, j, rows_b, :])
            ke = blockdiag(ke_ref[0, j, rows_f, :], ke_ref[1, j, rows_b, :])
            v = jnp.concatenate([v_ref[rows_f, cols], v_ref[rows_b, cols]], axis=0)
            dec = jnp.concatenate([dec_ref[0, j, pl.ds(cf, 1), :], dec_ref[1, j, pl.ds(cb, 1), :]], axis=1)
            o = lax.dot_general(qd, st.astype(BF16), _NT, preferred_element_type=F32)
            st_ref[j] = st * dec + lax.dot_general(v, ke, _TN, preferred_element_type=F32)
            oacc_ref[rows_f, cols] += o[:A_CHUNK, :]
            oacc_ref[rows_b, cols] += o[A_CHUNK:, :]
        return carry

    nb = seq // rb
    mid = nb // 2
    on = on_ref[...]

    def merged(p, carry):
        stage1(p, carry)
        stage1(nb - 1 - p, carry)
        for i in range(cpb):
            stage2(p * cpb + i, carry)
        return carry

    def finish_block(r):
        rows = pl.ds(pl.multiple_of(r * rb, rb), rb)
        for j in range(hb):
            cols = slice(j * HEAD_DIM, (j + 1) * HEAD_DIM)
            y = _rms(oacc_ref[rows, cols], on)
            o_ref[rows, cols] = (y * _silu(g_ref[rows, cols].astype(F32))).astype(o_ref.dtype)

    def tail(g, carry):
        finish_block(mid - g)
        finish_block(mid + g - 1)
        for i in range(cpb):
            stage2(nc // 2 + g * cpb + i, carry)
        return carry

    lax.fori_loop(0, mid, merged, 0)
    for i in range(cpb):
        stage2(nc // 2 + i, 0)
    lax.fori_loop(1, mid, tail, 0)
    finish_block(0)
    finish_block(nb - 1)

    if emit_state:
        for j in range(hb):
            for d in range(2):
                so_ref[d, j] = st_ref[j, :, d * HEAD_DIM:(d + 1) * HEAD_DIM].T


def _hgrn(u3, lb, onorm, s0, hb, emit_state):
    bsz, seq, _ = u3.shape
    bw = hb * HEAD_DIM
    nh = A_HEADS // hb

    def seg(s):
        return pl.BlockSpec((None, seq, bw), lambda b, h: (b, 0, s * nh + h))

    in_specs = [seg(0), seg(1), seg(2), seg(3), seg(4),
                pl.BlockSpec((2, bw), lambda b, h: (0, h)),
                pl.BlockSpec((1, HEAD_DIM), lambda b, h: (0, 0))]
    args = [u3, u3, u3, u3, u3, lb, onorm.reshape(1, HEAD_DIM)]
    st_spec = pl.BlockSpec((None, 2, hb, HEAD_DIM, HEAD_DIM), lambda b, h: (b, 0, h, 0, 0))
    if s0 is not None:
        in_specs.append(st_spec)
        args.append(s0)
    out_specs = [pl.BlockSpec((None, seq, bw), lambda b, h: (b, 0, h))]
    out_shape = [jax.ShapeDtypeStruct((bsz, seq, A_HEADS * HEAD_DIM), BF16)]
    if emit_state:
        out_specs.append(st_spec)
        out_shape.append(jax.ShapeDtypeStruct((bsz, 2, A_HEADS, HEAD_DIM, HEAD_DIM), F32))
    return pl.pallas_call(
        functools.partial(_hgrn_kernel, seq=seq, hb=hb, has_s0=s0 is not None,
                          emit_state=emit_state),
        grid=(bsz, nh),
        in_specs=in_specs,
        out_specs=out_specs,
        out_shape=out_shape,
        scratch_shapes=[pltpu.VMEM((seq, bw), F32),
                        pltpu.VMEM((hb, HEAD_DIM, 2 * HEAD_DIM), F32),
                        pltpu.VMEM((2, hb, seq, HEAD_DIM), BF16),
                        pltpu.VMEM((2, hb, seq, HEAD_DIM), BF16),
                        pltpu.VMEM((2, hb, seq // A_CHUNK, HEAD_DIM), F32)],
        compiler_params=_params("parallel", "parallel"),
        name="hgrn",
    )(*args)


def _norm_rope(xb, g_cos, gp_sin, rot_ref, out_scale):
    x = xb.astype(F32)
    sq = (x * x).astype(BF16)
    y = x * g_cos
    if gp_sin is not None:
        both = jnp.dot(jnp.concatenate([xb, sq], axis=1), rot_ref[...], preferred_element_type=F32)
        y = y + both[:, :HEAD_DIM] * gp_sin
        ms = both[:, HEAD_DIM:]
    else:
        mean_mat = jnp.full((HEAD_DIM, HEAD_DIM), 1.0 / HEAD_DIM, BF16)
        ms = jnp.dot(sq, mean_mat, preferred_element_type=F32)
    r = lax.rsqrt(ms + NORM_EPS)
    if out_scale != 1.0:
        r = r * out_scale
    return y * r


def _attn_kernel(*refs, seq, tq, nqb, ctx_len, rope, window, sink, emit_kv, shift_free):
    refs = list(refs)
    q_ref, gate_ref, k_ref, v_ref = refs[:4]
    pos = 4
    kc_ref = vc_ref = cos_ref = sin_ref = rot_ref = bias_ref = sink_ref = ko_ref = vo_ref = None
    if ctx_len:
        kc_ref, vc_ref = refs[pos:pos + 2]
        pos += 2
    if rope:
        cos_ref, sin_ref, rot_ref = refs[pos:pos + 3]
        pos += 3
    if window:
        bias_ref = refs[pos]
        pos += 1
    qn_ref, kn_ref = refs[pos:pos + 2]
    pos += 2
    if sink:
        sink_ref = refs[pos]
        pos += 1
    o_ref = refs[pos]
    pos += 1
    if emit_kv:
        ko_ref, vo_ref = refs[pos:pos + 2]
        pos += 2
    ks_ref, vs_ref = refs[pos:pos + 2]
    pos += 2
    qc_ref = qs_ref = None
    if rope:
        qc_ref, qs_ref = refs[pos:]

    kvh = pl.program_id(1)
    step = pl.program_id(2)
    nq = seq // tq
    pad = WINDOW if window else 0
    ctx_off = seq + 2 * pad
    qscale = LOG2E / math.sqrt(HEAD_DIM)

    @pl.when(step == 0)
    def _():
        if window:
            zeros = jnp.zeros((pad, HEAD_DIM), BF16)
            for ref in (ks_ref, vs_ref):
                ref[0:pad, 0:HEAD_DIM] = zeros
                ref[pad + seq:ctx_off, 0:HEAD_DIM] = zeros
        rc = 256
        kg = kn_ref[0:1, :]
        kgp = kn_ref[1:2, :]
        ones = jnp.ones((rc, HEAD_DIM), BF16)

        def body(r, carry):
            rows = pl.ds(pl.multiple_of(r * rc, rc), rc)
            dst = pl.ds(pl.multiple_of(pad + r * rc, HEAD_DIM), rc)
            if rope:
                cos = cos_ref[rows, :]
                sin = sin_ref[rows, :]
                qc_ref[rows, :] = cos * qn_ref[0:1, :]
                qs_ref[rows, :] = sin * qn_ref[1:2, :]
                k = _norm_rope(k_ref[rows, :], cos * kg, sin * kgp, rot_ref, 1.0)
            else:
                k = _norm_rope(k_ref[rows, :], kg, None, None, 1.0)
            if emit_kv:
                ko_ref[rows, :] = k
                vo_ref[rows, :] = v_ref[rows, :].astype(F32)
            ks_ref[dst, :] = k.astype(BF16)
            vs_ref[dst, 0:HEAD_DIM] = v_ref[rows, :]
            return carry

        lax.fori_loop(0, seq // rc, body, 0)
        if ctx_len:
            ks_ref[ctx_off:ctx_off + ctx_len, :] = kc_ref[...].astype(BF16)
            vs_ref[ctx_off:ctx_off + ctx_len, 0:HEAD_DIM] = vc_ref[...].astype(BF16)

        def fill(r, carry):
            vs_ref[pl.ds(pl.multiple_of(r * rc, rc), rc), HEAD_DIM:2 * HEAD_DIM] = ones
            return carry

        lax.fori_loop(0, vs_ref.shape[0] // rc, fill, 0)

    sink_col = None
    if sink:
        sink_col = jnp.concatenate(
            [jnp.full((tq, 1), sink_ref[kvh * GROUP + h] * LOG2E, F32) for h in range(GROUP)], axis=0)

    def prep(t):
        qi = step * nqb + t
        qrows = pl.ds(pl.multiple_of(qi * tq, tq), tq)
        xs = []
        for h in range(GROUP):
            xb = q_ref[t * tq:(t + 1) * tq, h * HEAD_DIM:(h + 1) * HEAD_DIM]
            if rope:
                x = _norm_rope(xb, qc_ref[qrows, :], qs_ref[qrows, :], rot_ref, qscale)
            else:
                x = _norm_rope(xb, qn_ref[0:1, :], None, None, qscale)
            xs.append(x.astype(BF16))
        return jnp.concatenate(xs, axis=0)

    def key_parts(t):
        qi = step * nqb + t
        if not window:
            return [(slice(None), None)]
        start = pl.multiple_of(qi * tq, HEAD_DIM)
        edge = jnp.where(qi == 0, 0, jnp.where(qi == nq - 1, 2, 1))
        parts = [(pl.ds(start, 3 * WINDOW), bias_ref[edge])]
        if ctx_len:
            parts.append((slice(ctx_off, ctx_off + ctx_len), None))
        return parts

    def logits(q, parts):
        ss = []
        for rows, bias in parts:
            s = lax.dot_general(q, ks_ref[rows, :], _NT, preferred_element_type=F32)
            ss.append(s if bias is None else s + bias)
        return ss

    def row_max(ss):
        slabs = [s[:, c:c + HEAD_DIM] for s in ss for c in range(0, s.shape[1], HEAD_DIM)]
        return jnp.max(functools.reduce(jnp.maximum, slabs), axis=-1, keepdims=True)

    def weighted(ss, m, parts):
        acc = None
        for s, (rows, _) in zip(ss, parts):
            p = jnp.exp2(s if m is None else s - m)
            pv = jnp.dot(p.astype(BF16), vs_ref[rows, :], preferred_element_type=F32)
            acc = pv if acc is None else acc + pv
        return acc

    def finish(t, acc, m):
        l = acc[:, HEAD_DIM:]
        if sink:
            l = l + jnp.exp2(sink_col if m is None else sink_col - m)
        for h in range(GROUP):
            cols = slice(h * HEAD_DIM, (h + 1) * HEAD_DIM)
            rows = slice(h * tq, (h + 1) * tq)
            gate = gate_ref[t * tq:(t + 1) * tq, cols].astype(F32)
            inv = 1.0 / (l[rows, :] * (1.0 + jnp.exp(-gate)))
            o_ref[t * tq:(t + 1) * tq, cols] = (acc[rows, :HEAD_DIM] * gate * inv).astype(o_ref.dtype)

    blocks = range(nqb)
    parts = [key_parts(t) for t in blocks]
    ss, ms = {}, {}
    q_next = prep(0)
    for t in blocks:
        q = q_next
        if t + 1 < nqb:
            q_next = prep(t + 1)
        ss[t] = logits(q, parts[t])
        ms[t] = None if shift_free else row_max(ss[t])
        if t >= 1:
            finish(t - 1, weighted(ss.pop(t - 1), ms[t - 1], parts[t - 1]), ms.pop(t - 1))
    finish(nqb - 1, weighted(ss[nqb - 1], ms[nqb - 1], parts[nqb - 1]), ms[nqb - 1])


def _window_bias(tq):
    r = (jnp.arange(GROUP * tq) % tq)[:, None]
    c = jnp.arange(3 * WINDOW)[None, :]
    band = (c >= r) & (c <= r + 2 * WINDOW)
    keep = jnp.stack([band & (c >= WINDOW), band, band & (c < 2 * WINDOW)])
    return jnp.where(keep, 0.0, NEG).astype(F32)


def _attn(u3, q_off, k_off, v_off, g_off, kv_heads, qnorm, knorm, *, tq, nqb, ctx=None, rope=None,
          window=False, sink=None, emit_kv=False):
    k_norm = math.sqrt(HEAD_DIM) * jnp.max(jnp.abs(knorm))
    if ctx is not None:
        k_norm = jnp.maximum(k_norm, jnp.sqrt(jnp.max(jnp.sum(jnp.square(ctx[0]), axis=-1))))
    bound = jnp.max(jnp.abs(qnorm)) * LOG2E * k_norm * LOGIT_BOUND_MARGIN
    args = (u3, q_off, k_off, v_off, g_off, kv_heads, qnorm, knorm)
    kwargs = dict(tq=tq, nqb=nqb, ctx=ctx, rope=rope, window=window, sink=sink, emit_kv=emit_kv)
    return lax.cond(bound <= MAX_UNSHIFTED_LOGIT,
                    lambda: _attn_call(*args, shift_free=True, **kwargs),
                    lambda: _attn_call(*args, shift_free=False, **kwargs))


def _attn_call(u3, q_off, k_off, v_off, g_off, kv_heads, qnorm, knorm, *, tq, nqb, ctx, rope,
               window, sink, emit_kv, shift_free):
    bsz, seq, _ = u3.shape
    gw = GROUP * HEAD_DIM
    heads = kv_heads * GROUP
    ctx_len = 0 if ctx is None else ctx[0].shape[1]
    pad = WINDOW if window else 0
    assert not window or (tq == WINDOW and seq // tq >= 2)
    assert seq % (tq * nqb) == 0

    def wide(off):
        return pl.BlockSpec((None, tq * nqb, gw), lambda b, h, i: (b, i, off // gw + h))

    def narrow(off, rows):
        return pl.BlockSpec((None, rows, HEAD_DIM), lambda b, h, i: (b, 0, off // HEAD_DIM + h))

    def whole(shape):
        return pl.BlockSpec(shape, lambda b, h, i: (0,) * len(shape))

    in_specs = [wide(q_off), wide(g_off), narrow(k_off, seq), narrow(v_off, seq)]
    args = [u3, u3, u3, u3]
    if ctx is not None:
        for cache in ctx:
            in_specs.append(narrow(0, ctx_len))
            args.append(cache.reshape(bsz, ctx_len, kv_heads * HEAD_DIM))
    if rope is not None:
        for t in rope:
            in_specs.append(whole(t.shape))
            args.append(t)
    if window:
        in_specs.append(whole((3, GROUP * tq, 3 * WINDOW)))
        args.append(_window_bias(tq))
    in_specs += [whole((2, HEAD_DIM)), whole((2, HEAD_DIM))]
    args += [jnp.stack([qnorm, qnorm[_ROT_PARTNER]]), jnp.stack([knorm, knorm[_ROT_PARTNER]])]
    if sink is not None:
        in_specs.append(pl.BlockSpec(memory_space=pltpu.SMEM))
        args.append(sink)
    out_specs = [pl.BlockSpec((None, tq * nqb, gw), lambda b, h, i: (b, i, h))]
    out_shape = [jax.ShapeDtypeStruct((bsz, seq, heads * HEAD_DIM), BF16)]
    if emit_kv:
        for _ in range(2):
            out_specs.append(narrow(0, seq))
            out_shape.append(jax.ShapeDtypeStruct((bsz, seq, kv_heads * HEAD_DIM), F32))
    rows = seq + 2 * pad + ctx_len
    kern = functools.partial(_attn_kernel, seq=seq, tq=tq, nqb=nqb, ctx_len=ctx_len,
                             rope=rope is not None, window=window, sink=sink is not None,
                             emit_kv=emit_kv, shift_free=shift_free)
    return pl.pallas_call(
        kern,
        grid=(bsz, kv_heads, seq // (tq * nqb)),
        in_specs=in_specs,
        out_specs=out_specs,
        out_shape=out_shape,
        scratch_shapes=[pltpu.VMEM((rows, HEAD_DIM), BF16),
                        pltpu.VMEM((rows, 2 * HEAD_DIM), BF16)]
        + ([pltpu.VMEM((seq, HEAD_DIM), F32)] * 2 if rope is not None else []),
        compiler_params=_params("parallel", "parallel", "arbitrary"),
        name="attn",
    )(*args)


def _rope_tables(n_tokens):
    rows = n_tokens // GRID_W
    row = jnp.repeat(jnp.arange(rows), GRID_W).astype(F32)
    col = (jnp.arange(rows * GRID_W) % GRID_W).astype(F32)
    inv = ROPE_THETA ** (-jnp.arange(ROPE_QUARTER, dtype=F32) / ROPE_QUARTER)
    ar = row[:, None] * inv
    ac = col[:, None] * inv
    ang = jnp.concatenate([ar, ar, ac, ac], axis=-1)
    rot = np.zeros((2 * HEAD_DIM, 2 * HEAD_DIM), np.float32)
    rot[_ROT_PARTNER, np.arange(HEAD_DIM)] = np.where(_ROT_FIRST, -1.0, 1.0)
    rot[HEAD_DIM:, HEAD_DIM:] = 1.0 / HEAD_DIM
    return jnp.cos(ang), jnp.sin(ang), jnp.asarray(rot, BF16)


def _tile(m, pref):
    t = pref
    while m % t:
        t //= 2
    return t


def kernel(x_prompt, x_sample, state_l0_hgrn, cache_l0_k, cache_l0_v, cache_l1_k, cache_l1_v, c, c_ctx, lb_gamma, l0_norm, l0_w_mod, l0_b_mod, l0_w_in, l0_w_out, l0_a_onorm, l0_b_qnorm, l0_b_knorm, l1_norm, l1_w_mod, l1_b_mod, l1_w_in, l1_w_out, l1_c_qnorm, l1_c_knorm, l1_c_sink):
    pb, pl_, d = x_prompt.shape
    sb, sl, _ = x_sample.shape
    aw = A_HEADS * HEAD_DIM
    bkv = B_KV_HEADS * HEAD_DIM
    bw = B_KV_HEADS * GROUP * HEAD_DIM
    ckv = C_KV_HEADS * HEAD_DIM
    cw = C_KV_HEADS * GROUP * HEAD_DIM

    lb = jnp.cumsum(jax.nn.softmax(lb_gamma.astype(F32), axis=0), axis=0)[0]
    rope = _rope_tables(sl)

    nrow = -(-(sb + 1) // 8) * 8
    cond = jnp.zeros((nrow, d), F32).at[:sb].set(c).at[sb].set(c_ctx)
    xs = (x_prompt.reshape(pb * pl_, d), x_sample.reshape(sb * sl, d))
    tms = (_tile(pb * pl_, 1024), _tile(sl, 1024))
    rows_per_mod = (pb * pl_, sl)

    def mods(w_mod, b_mod):
        m = _adaln(cond, w_mod, b_mod)
        parts = [m[:, i * d:(i + 1) * d] for i in range(3)]
        return ([p[sb:sb + 1, None, :] for p in parts], [p[:sb, None, :] for p in parts])

    def tn_for(n):
        for t in (1536, 1280, 1024, 768, 512, 256, 128):
            if n % t == 0:
                return t
        return n

    mod_p, mod_s = mods(l0_w_mod, l0_b_mod)
    w_in = l0_w_in.astype(BF16)
    w_out = l0_w_out.astype(BF16)
    tn = tn_for(w_in.shape[1])
    u_p = _inproj(xs[0], l0_norm, mod_p[0], mod_p[1], w_in, rows_per_mod[0], tms[0], tn).reshape(pb, pl_, -1)
    u_s = _inproj(xs[1], l0_norm, mod_s[0], mod_s[1], w_in, rows_per_mod[1], tms[1], tn).reshape(sb, sl, -1)

    oa_p, new_state = _hgrn(u_p, lb, l0_a_onorm, None, hb=A_HEADS, emit_state=True)
    (oa_s,) = _hgrn(u_s, lb, l0_a_onorm, state_l0_hgrn, hb=4, emit_state=False)

    q_off = 5 * aw
    k_off = q_off + bw
    v_off = k_off + bkv
    g_off = v_off + bkv
    ob_p, k0, v0 = _attn(u_p, q_off, k_off, v_off, g_off, B_KV_HEADS, l0_b_qnorm, l0_b_knorm,
                         tq=128, nqb=2, emit_kv=True)
    (ob_s,) = _attn(u_s, q_off, k_off, v_off, g_off, B_KV_HEADS, l0_b_qnorm, l0_b_knorm,
                    tq=128, nqb=_tile(sl // 128, 8), ctx=(cache_l0_k, cache_l0_v), rope=rope)

    otm = (_tile(pb * pl_, 512), _tile(sl, 512))
    y_p = _outproj([oa_p.reshape(pb * pl_, aw), ob_p.reshape(pb * pl_, bw)], w_out, xs[0], mod_p[2],
                   rows_per_mod[0], otm[0])
    y_s = _outproj([oa_s.reshape(sb * sl, aw), ob_s.reshape(sb * sl, bw)], w_out, xs[1], mod_s[2],
                   rows_per_mod[1], otm[1])

    mod_p, mod_s = mods(l1_w_mod, l1_b_mod)
    w_in = l1_w_in.astype(BF16)
    w_out = l1_w_out.astype(BF16)
    tn = tn_for(w_in.shape[1])
    u_p = _inproj(y_p, l1_norm, mod_p[0], mod_p[1], w_in, rows_per_mod[0], tms[0], tn).reshape(pb, pl_, -1)
    u_s = _inproj(y_s, l1_norm, mod_s[0], mod_s[1], w_in, rows_per_mod[1], tms[1], tn).reshape(sb, sl, -1)

    k_off = cw
    v_off = k_off + ckv
    g_off = v_off + ckv
    oc_p, k1, v1 = _attn(u_p, 0, k_off, v_off, g_off, C_KV_HEADS, l1_c_qnorm, l1_c_knorm,
                         tq=128, nqb=2, sink=l1_c_sink, emit_kv=True)
    (oc_s,) = _attn(u_s, 0, k_off, v_off, g_off, C_KV_HEADS, l1_c_qnorm, l1_c_knorm,
                    tq=WINDOW, nqb=_tile(sl // WINDOW, 16), ctx=(cache_l1_k, cache_l1_v), rope=rope, window=True, sink=l1_c_sink)

    z_p = _outproj([oc_p.reshape(pb * pl_, cw)], w_out, y_p, mod_p[2], rows_per_mod[0], otm[0])
    z_s = _outproj([oc_s.reshape(sb * sl, cw)], w_out, y_s, mod_s[2], rows_per_mod[1], otm[1])

    return (z_p.reshape(pb, pl_, d), z_s.reshape(sb, sl, d), new_state,
            k0.reshape(pb, pl_, B_KV_HEADS, HEAD_DIM), v0.reshape(pb, pl_, B_KV_HEADS, HEAD_DIM),
            k1.reshape(pb, pl_, C_KV_HEADS, HEAD_DIM), v1.reshape(pb, pl_, C_KV_HEADS, HEAD_DIM))
```

```python
import functools
import math

import jax
import jax.numpy as jnp
import numpy as np
from jax import lax
from jax.experimental import pallas as pl
from jax.experimental.pallas import tpu as pltpu

F32 = jnp.float32
BF16 = jnp.bfloat16

HEAD_DIM = 128
GRID_W = 64
ROPE_QUARTER = HEAD_DIM // 4
ROPE_THETA = 10000.0
NORM_EPS = 1e-6
A_HEADS = 8
A_CHUNK = 32
B_KV_HEADS = 2
C_KV_HEADS = 4
GROUP = 4
WINDOW = 128
NEG = -1e30
LOG2E = math.log2(math.e)
MAX_UNSHIFTED_LOGIT = 60.0
LOGIT_BOUND_MARGIN = 1.1
VMEM_LIMIT = 56 * 1024 * 1024

_ROT_FIRST = (np.arange(HEAD_DIM) % (2 * ROPE_QUARTER)) < ROPE_QUARTER
_ROT_PARTNER = np.where(_ROT_FIRST, np.arange(HEAD_DIM) + ROPE_QUARTER, np.arange(HEAD_DIM) - ROPE_QUARTER)

_NT = (((1,), (1,)), ((), ()))
_TN = (((0,), (0,)), ((), ()))


def _params(*sem):
    return pltpu.CompilerParams(dimension_semantics=sem, vmem_limit_bytes=VMEM_LIMIT)


def _silu(x):
    return x * jax.nn.sigmoid(x)


def _rms(x, g):
    ms = jnp.mean(x * x, axis=-1, keepdims=True)
    return x * lax.rsqrt(ms + NORM_EPS) * g


def _adaln_kernel(c_ref, w_ref, b_ref, o_ref):
    a = _silu(c_ref[...]).astype(BF16)
    o_ref[...] = jnp.dot(a, w_ref[...].astype(BF16), preferred_element_type=F32) + b_ref[...]


def _adaln(cond, w_mod, b_mod):
    r, d = cond.shape
    n = w_mod.shape[1]
    tn = _tile(n, 1024)
    return pl.pallas_call(
        _adaln_kernel,
        grid=(n // tn,),
        in_specs=[pl.BlockSpec((r, d), lambda j: (0, 0)),
                  pl.BlockSpec((d, tn), lambda j: (0, j)),
                  pl.BlockSpec((1, tn), lambda j: (0, j))],
        out_specs=pl.BlockSpec((r, tn), lambda j: (0, j)),
        out_shape=jax.ShapeDtypeStruct((r, n), F32),
        compiler_params=_params("parallel"),
        name="adaln",
    )(cond, w_mod, b_mod.reshape(1, n))


def _inproj_kernel(x_ref, g_ref, sh_ref, sc_ref, w_ref, o_ref, h_ref, *, rc, n_tiles, norm_steps):
    i = pl.program_id(0)
    j = pl.program_id(1)
    slice_rows = x_ref.shape[0] // norm_steps

    def matmul():
        o_ref[...] = jnp.dot(h_ref[(i + 1) % 2], w_ref[...], preferred_element_type=F32).astype(o_ref.dtype)

    def norm_slice():
        gain = g_ref[...] * (1.0 + sc_ref[...])
        sh = sh_ref[...]
        for r in range(slice_rows // rc):
            rows = pl.ds(pl.multiple_of(j * slice_rows + r * rc, rc), rc)
            h_ref[i % 2, rows, :] = (_rms(x_ref[rows, :], gain) + sh).astype(BF16)

    has_mm = i >= 1
    has_norm = (i < n_tiles) & (j < norm_steps)

    @pl.when(has_mm & has_norm)
    def _():
        matmul()
        norm_slice()

    @pl.when(has_mm & jnp.logical_not(has_norm))
    def _():
        matmul()

    @pl.when(jnp.logical_not(has_mm) & has_norm)
    def _():
        norm_slice()


def _inproj(x2d, norm_g, shift, scale, w, rows_per_mod, tm, tn):
    m, d = x2d.shape
    n = w.shape[1]
    n_tiles = m // tm
    n_cols = n // tn
    norm_steps = max(s for s in (1, 2, 4, 8) if s <= n_cols and (tm // s) % 32 == 0)

    def tile(i):
        return jnp.minimum(i, n_tiles - 1)

    def col(i, j):
        return jnp.where(i == 0, 0, j)

    mod_spec = pl.BlockSpec((None, 1, d), lambda i, j: ((tile(i) * tm) // rows_per_mod, 0, 0))
    return pl.pallas_call(
        functools.partial(_inproj_kernel, rc=32, n_tiles=n_tiles, norm_steps=norm_steps),
        grid=(n_tiles + 1, n_cols),
        in_specs=[pl.BlockSpec((tm, d), lambda i, j: (tile(i), 0)),
                  pl.BlockSpec((1, d), lambda i, j: (0, 0)),
                  mod_spec, mod_spec,
                  pl.BlockSpec((d, tn), lambda i, j: (0, col(i, j)))],
        out_specs=pl.BlockSpec((tm, tn), lambda i, j: (jnp.maximum(i - 1, 0), col(i, j))),
        out_shape=jax.ShapeDtypeStruct((m, n), BF16),
        scratch_shapes=[pltpu.VMEM((2, tm, d), BF16)],
        compiler_params=_params("arbitrary", "arbitrary"),
        name="inproj",
    )(x2d, norm_g.reshape(1, d), shift, scale, w)


def _outproj_kernel(*refs, widths):
    o_refs = refs[:len(widths)]
    w_ref, x_ref, gt_ref, y_ref = refs[len(widths):]
    acc = None
    start = 0
    for o_ref, wd in zip(o_refs, widths):
        part = jnp.dot(o_ref[...], w_ref[start:start + wd, :], preferred_element_type=F32)
        acc = part if acc is None else acc + part
        start += wd
    y_ref[...] = x_ref[...] + gt_ref[...] * acc


def _outproj(os, w, x2d, gate, rows_per_mod, tm):
    m, d = x2d.shape
    widths = tuple(o.shape[1] for o in os)
    k = w.shape[0]
    return pl.pallas_call(
        functools.partial(_outproj_kernel, widths=widths),
        grid=(m // tm,),
        in_specs=[pl.BlockSpec((tm, wd), lambda i: (i, 0)) for wd in widths] + [
            pl.BlockSpec((k, d), lambda i: (0, 0)),
            pl.BlockSpec((tm, d), lambda i: (i, 0)),
            pl.BlockSpec((None, 1, d), lambda i: ((i * tm) // rows_per_mod, 0, 0))],
        out_specs=pl.BlockSpec((tm, d), lambda i: (i, 0)),
        out_shape=jax.ShapeDtypeStruct((m, d), F32),
        compiler_params=_params("parallel"),
        name="outproj",
    )(*os, w, x2d, gate)


SUBLANES = 8
HGRN_BLOCK = 128


def _chunk_cumprod(f, reverse):
    n = f.shape[0]
    tiles = n // SUBLANES
    per_chunk = A_CHUNK // SUBLANES
    x = f.reshape(tiles, SUBLANES, HEAD_DIM)
    sub = lax.broadcasted_iota(jnp.int32, x.shape, 1)
    s = 1
    while s < SUBLANES:
        if reverse:
            x = x * jnp.where(sub < SUBLANES - s, pltpu.roll(x, SUBLANES - s, axis=1), 1.0)
        else:
            x = x * jnp.where(sub >= s, pltpu.roll(x, s, axis=1), 1.0)
        s *= 2
    edge = 0 if reverse else SUBLANES - 1
    out, tot = [], []
    for c in range(n // A_CHUNK):
        ts = [x[c * per_chunk + i] for i in range(per_chunk)]
        order = range(per_chunk - 1, -1, -1) if reverse else range(per_chunk)
        carry = None
        done = {}
        for i in order:
            t = ts[i] if carry is None else ts[i] * carry
            done[i] = t
            carry = t[edge:edge + 1, :]
        out += [done[i] for i in range(per_chunk)]
        tot += [jnp.broadcast_to(carry, (A_CHUNK, HEAD_DIM))]
    return jnp.concatenate(out, axis=0), jnp.concatenate(tot, axis=0)


def _hgrn_kernel(*refs, seq, hb, has_s0, emit_state):
    q_ref, ff_ref, fb_ref, v_ref, g_ref, lb_ref, on_ref = refs[:7]
    pos = 7
    s0_ref = None
    if has_s0:
        s0_ref = refs[pos]
        pos += 1
    o_ref = refs[pos]
    pos += 1
    so_ref = None
    if emit_state:
        so_ref = refs[pos]
        pos += 1
    oacc_ref, st_ref, qd_ref, ke_ref, dec_ref = refs[pos:]

    nc = seq // A_CHUNK
    rb = HGRN_BLOCK
    cpb = rb // A_CHUNK
    ri = lax.broadcasted_iota(jnp.int32, (rb, rb), 0)
    ci = lax.broadcasted_iota(jnp.int32, (rb, rb), 1)
    shift = A_CHUNK.bit_length() - 1
    same_chunk = jnp.right_shift(ri, shift) == jnp.right_shift(ci, shift)
    masks = (same_chunk & (ri >= ci), same_chunk & (ci >= ri))

    for j in range(hb):
        for d in range(2):
            lanes = slice(d * HEAD_DIM, (d + 1) * HEAD_DIM)
            if has_s0:
                st_ref[j, :, lanes] = s0_ref[d, j].T
            else:
                st_ref[j, :, lanes] = jnp.zeros((HEAD_DIM, HEAD_DIM), F32)

    def stage1(blk, carry):
        rows = pl.ds(pl.multiple_of(blk * rb, rb), rb)
        heads = range(hb)
        chains = [(j, d) for j in heads for d in range(2)]
        cols = [slice(j * HEAD_DIM, (j + 1) * HEAD_DIM) for j in heads]
        qs = [_silu(q_ref[rows, cols[j]].astype(F32)) for j in heads]
        ts, fs, scans, qds, kinvs, atts = {}, {}, {}, {}, {}, {}
        for j, d in chains:
            fr = (fb_ref if d else ff_ref)[rows, cols[j]].astype(F32)
            lb = lb_ref[d:d + 1, cols[j]]
            ts[j, d] = (1.0 - lb) * jax.nn.sigmoid(fr)
            fs[j, d] = lb + ts[j, d]
        for j, d in chains:
            scans[j, d] = _chunk_cumprod(fs[j, d], reverse=bool(d))
        for j, d in chains:
            eb, dec = scans[j, d]
            lb = lb_ref[d:d + 1, cols[j]]
            qds[j, d] = (qs[j] * eb).astype(BF16)
            kinvs[j, d] = ((1.0 - lb) - ts[j, d]) * (1.0 / eb)
            qd_ref[d, j, rows, :] = qds[j, d]
            ke_ref[d, j, rows, :] = (kinvs[j, d] * dec).astype(BF16)
            for c in range(cpb):
                dec_ref[d, j, pl.ds(blk * cpb + c, 1), :] = dec[c * A_CHUNK:c * A_CHUNK + 1, :]
        for j, d in chains:
            att = lax.dot_general(qds[j, d], kinvs[j, d].astype(BF16), _NT, preferred_element_type=F32)
            atts[j, d] = jnp.where(masks[d], att, 0.0).astype(BF16)
        for j in heads:
            v = v_ref[rows, cols[j]]
            oacc_ref[rows, cols[j]] = (jnp.dot(atts[j, 0], v, preferred_element_type=F32)
                                       + jnp.dot(atts[j, 1], v, preferred_element_type=F32))
        return carry

    zero = jnp.zeros((A_CHUNK, HEAD_DIM), BF16)

    def blockdiag(a, b):
        return jnp.concatenate([jnp.concatenate([a, zero], axis=1), jnp.concatenate([zero, b], axis=1)], axis=0)

    def stage2(n, carry):
        cf, cb = n, nc - 1 - n
        rows_f = pl.ds(pl.multiple_of(cf * A_CHUNK, A_CHUNK), A_CHUNK)
        rows_b = pl.ds(pl.multiple_of(cb * A_CHUNK, A_CHUNK), A_CHUNK)
        for j in range(hb):
            cols = slice(j * HEAD_DIM, (j + 1) * HEAD_DIM)
            st = st_ref[j]
            qd = blockdiag(qd_ref[0, j, rows_f, :], qd_ref[1, j, rows_b, :])
            ke = blockdiag(ke_ref[0, j, rows_f, :], ke_ref[1, j, rows_b, :])
            v = jnp.concatenate([v_ref[rows_f, cols], v_ref[rows_b, cols]], axis=0)
            dec = jnp.concatenate([dec_ref[0, j, pl.ds(cf, 1), :], dec_ref[1, j, pl.ds(cb, 1), :]], axis=1)
            o = lax.dot_general(qd, st.astype(BF16), _NT, preferred_element_type=F32)
            st_ref[j] = st * dec + lax.dot_general(v, ke, _TN, preferred_element_type=F32)
            oacc_ref[rows_f, cols] += o[:A_CHUNK, :]
            oacc_ref[rows_b, cols] += o[A_CHUNK:, :]
        return carry

    nb = seq // rb
    mid = nb // 2
    on = on_ref[...]

    def merged(p, carry):
        stage1(p, carry)
        stage1(nb - 1 - p, carry)
        for i in range(cpb):
            stage2(p * cpb + i, carry)
        return carry

    def finish_block(r):
        rows = pl.ds(pl.multiple_of(r * rb, rb), rb)
        for j in range(hb):
            cols = slice(j * HEAD_DIM, (j + 1) * HEAD_DIM)
            y = _rms(oacc_ref[rows, cols], on)
            o_ref[rows, cols] = (y * _silu(g_ref[rows, cols].astype(F32))).astype(o_ref.dtype)

    def tail(g, carry):
        finish_block(mid - g)
        finish_block(mid + g - 1)
        for i in range(cpb):
            stage2(nc // 2 + g * cpb + i, carry)
        return carry

    lax.fori_loop(0, mid, merged, 0)
    for i in range(cpb):
        stage2(nc // 2 + i, 0)
    lax.fori_loop(1, mid, tail, 0)
    finish_block(0)
    finish_block(nb - 1)

    if emit_state:
        for j in range(hb):
            for d in range(2):
                so_ref[d, j] = st_ref[j, :, d * HEAD_DIM:(d + 1) * HEAD_DIM].T


def _hgrn(u3, lb, onorm, s0, hb, emit_state):
    bsz, seq, _ = u3.shape
    bw = hb * HEAD_DIM
    nh = A_HEADS // hb

    def seg(s):
        return pl.BlockSpec((None, seq, bw), lambda b, h: (b, 0, s * nh + h))

    in_specs = [seg(0), seg(1), seg(2), seg(3), seg(4),
                pl.BlockSpec((2, bw), lambda b, h: (0, h)),
                pl.BlockSpec((1, HEAD_DIM), lambda b, h: (0, 0))]
    args = [u3, u3, u3, u3, u3, lb, onorm.reshape(1, HEAD_DIM)]
    st_spec = pl.BlockSpec((None, 2, hb, HEAD_DIM, HEAD_DIM), lambda b, h: (b, 0, h, 0, 0))
    if s0 is not None:
        in_specs.append(st_spec)
        args.append(s0)
    out_specs = [pl.BlockSpec((None, seq, bw), lambda b, h: (b, 0, h))]
    out_shape = [jax.ShapeDtypeStruct((bsz, seq, A_HEADS * HEAD_DIM), BF16)]
    if emit_state:
        out_specs.append(st_spec)
        out_shape.append(jax.ShapeDtypeStruct((bsz, 2, A_HEADS, HEAD_DIM, HEAD_DIM), F32))
    return pl.pallas_call(
        functools.partial(_hgrn_kernel, seq=seq, hb=hb, has_s0=s0 is not None,
                          emit_state=emit_state),
        grid=(bsz, nh),
        in_specs=in_specs,
        out_specs=out_specs,
        out_shape=out_shape,
        scratch_shapes=[pltpu.VMEM((seq, bw), F32),
                        pltpu.VMEM((hb, HEAD_DIM, 2 * HEAD_DIM), F32),
                        pltpu.VMEM((2, hb, seq, HEAD_DIM), BF16),
                        pltpu.VMEM((2, hb, seq, HEAD_DIM), BF16),
                        pltpu.VMEM((2, hb, seq // A_CHUNK, HEAD_DIM), F32)],
        compiler_params=_params("parallel", "parallel"),
        name="hgrn",
    )(*args)


def _norm_rope(xb, g_cos, gp_sin, rot_ref, out_scale):
    x = xb.astype(F32)
    sq = (x * x).astype(BF16)
    y = x * g_cos
    if gp_sin is not None:
        both = jnp.dot(jnp.concatenate([xb, sq], axis=1), rot_ref[...], preferred_element_type=F32)
        y = y + both[:, :HEAD_DIM] * gp_sin
        ms = both[:, HEAD_DIM:]
    else:
        mean_mat = jnp.full((HEAD_DIM, HEAD_DIM), 1.0 / HEAD_DIM, BF16)
        ms = jnp.dot(sq, mean_mat, preferred_element_type=F32)
    r = lax.rsqrt(ms + NORM_EPS)
    if out_scale != 1.0:
        r = r * out_scale
    return y * r


def _attn_kernel(*refs, seq, tq, nqb, ctx_len, rope, window, sink, emit_kv, shift_free):
    refs = list(refs)
    q_ref, gate_ref, k_ref, v_ref = refs[:4]
    pos = 4
    kc_ref = vc_ref = cos_ref = sin_ref = rot_ref = bias_ref = sink_ref = ko_ref = vo_ref = None
    if ctx_len:
        kc_ref, vc_ref = refs[pos:pos + 2]
        pos += 2
    if rope:
        cos_ref, sin_ref, rot_ref = refs[pos:pos + 3]
        pos += 3
    if window:
        bias_ref = refs[pos]
        pos += 1
    qn_ref, kn_ref = refs[pos:pos + 2]
    pos += 2
    if sink:
        sink_ref = refs[pos]
        pos += 1
    o_ref = refs[pos]
    pos += 1
    if emit_kv:
        ko_ref, vo_ref = refs[pos:pos + 2]
        pos += 2
    ks_ref, vs_ref = refs[pos:pos + 2]
    pos += 2
    qc_ref = qs_ref = None
    if rope:
        qc_ref, qs_ref = refs[pos:]

    kvh = pl.program_id(1)
    step = pl.program_id(2)
    nq = seq // tq
    pad = WINDOW if window else 0
    ctx_off = seq + 2 * pad
    qscale = LOG2E / math.sqrt(HEAD_DIM)

    @pl.when(step == 0)
    def _():
        if window:
            zeros = jnp.zeros((pad, HEAD_DIM), BF16)
            for ref in (ks_ref, vs_ref):
                ref[0:pad, 0:HEAD_DIM] = zeros
                ref[pad + seq:ctx_off, 0:HEAD_DIM] = zeros
        rc = 256
        kg = kn_ref[0:1, :]
        kgp = kn_ref[1:2, :]
        ones = jnp.ones((rc, HEAD_DIM), BF16)

        def body(r, carry):
            rows = pl.ds(pl.multiple_of(r * rc, rc), rc)
            dst = pl.ds(pl.multiple_of(pad + r * rc, HEAD_DIM), rc)
            if rope:
                cos = cos_ref[rows, :]
                sin = sin_ref[rows, :]
                qc_ref[rows, :] = cos * qn_ref[0:1, :]
                qs_ref[rows, :] = sin * qn_ref[1:2, :]
                k = _norm_rope(k_ref[rows, :], cos * kg, sin * kgp, rot_ref, 1.0)
            else:
                k = _norm_rope(k_ref[rows, :], kg, None, None, 1.0)
            if emit_kv:
                ko_ref[rows, :] = k
                vo_ref[rows, :] = v_ref[rows, :].astype(F32)
            ks_ref[dst, :] = k.astype(BF16)
            vs_ref[dst, 0:HEAD_DIM] = v_ref[rows, :]
            return carry

        lax.fori_loop(0, seq // rc, body, 0)
        if ctx_len:
            ks_ref[ctx_off:ctx_off + ctx_len, :] = kc_ref[...].astype(BF16)
            vs_ref[ctx_off:ctx_off + ctx_len, 0:HEAD_DIM] = vc_ref[...].astype(BF16)

        def fill(r, carry):
            vs_ref[pl.ds(pl.multiple_of(r * rc, rc), rc), HEAD_DIM:2 * HEAD_DIM] = ones
            return carry

        lax.fori_loop(0, vs_ref.shape[0] // rc, fill, 0)

    sink_col = None
    if sink:
        sink_col = jnp.concatenate(
            [jnp.full((tq, 1), sink_ref[kvh * GROUP + h] * LOG2E, F32) for h in range(GROUP)], axis=0)

    def prep(t):
        qi = step * nqb + t
        qrows = pl.ds(pl.multiple_of(qi * tq, tq), tq)
        xs = []
        for h in range(GROUP):
            xb = q_ref[t * tq:(t + 1) * tq, h * HEAD_DIM:(h + 1) * HEAD_DIM]
            if rope:
                x = _norm_rope(xb, qc_ref[qrows, :], qs_ref[qrows, :], rot_ref, qscale)
            else:
                x = _norm_rope(xb, qn_ref[0:1, :], None, None, qscale)
            xs.append(x.astype(BF16))
        return jnp.concatenate(xs, axis=0)

    def key_parts(t):
        qi = step * nqb + t
        if not window:
            return [(slice(None), None)]
        start = pl.multiple_of(qi * tq, HEAD_DIM)
        edge = jnp.where(qi == 0, 0, jnp.where(qi == nq - 1, 2, 1))
        parts = [(pl.ds(start, 3 * WINDOW), bias_ref[edge])]
        if ctx_len:
            parts.append((slice(ctx_off, ctx_off + ctx_len), None))
        return parts

    def logits(q, parts):
        ss = []
        for rows, bias in parts:
            s = lax.dot_general(q, ks_ref[rows, :], _NT, preferred_element_type=F32)
            ss.append(s if bias is None else s + bias)
        return ss

    def row_max(ss):
        slabs = [s[:, c:c + HEAD_DIM] for s in ss for c in range(0, s.shape[1], HEAD_DIM)]
        return jnp.max(functools.reduce(jnp.maximum, slabs), axis=-1, keepdims=True)

    def weighted(ss, m, parts):
        acc = None
        for s, (rows, _) in zip(ss, parts):
            p = jnp.exp2(s if m is None else s - m)
            pv = jnp.dot(p.astype(BF16), vs_ref[rows, :], preferred_element_type=F32)
            acc = pv if acc is None else acc + pv
        return acc

    def finish(t, acc, m):
        l = acc[:, HEAD_DIM:]
        if sink:
            l = l + jnp.exp2(sink_col if m is None else sink_col - m)
        for h in range(GROUP):
            cols = slice(h * HEAD_DIM, (h + 1) * HEAD_DIM)
            rows = slice(h * tq, (h + 1) * tq)
            gate = gate_ref[t * tq:(t + 1) * tq, cols].astype(F32)
            inv = 1.0 / (l[rows, :] * (1.0 + jnp.exp(-gate)))
            o_ref[t * tq:(t + 1) * tq, cols] = (acc[rows, :HEAD_DIM] * gate * inv).astype(o_ref.dtype)

    blocks = range(nqb)
    parts = [key_parts(t) for t in blocks]
    ss, ms = {}, {}
    q_next = prep(0)
    for t in blocks:
        q = q_next
        if t + 1 < nqb:
            q_next = prep(t + 1)
        ss[t] = logits(q, parts[t])
        ms[t] = None if shift_free else row_max(ss[t])
        if t >= 1:
            finish(t - 1, weighted(ss.pop(t - 1), ms[t - 1], parts[t - 1]), ms.pop(t - 1))
    finish(nqb - 1, weighted(ss[nqb - 1], ms[nqb - 1], parts[nqb - 1]), ms[nqb - 1])


def _window_bias(tq):
    r = (jnp.arange(GROUP * tq) % tq)[:, None]
    c = jnp.arange(3 * WINDOW)[None, :]
    band = (c >= r) & (c <= r + 2 * WINDOW)
    keep = jnp.stack([band & (c >= WINDOW), band, band & (c < 2 * WINDOW)])
    return jnp.where(keep, 0.0, NEG).astype(F32)


def _attn(u3, q_off, k_off, v_off, g_off, kv_heads, qnorm, knorm, *, tq, nqb, ctx=None, rope=None,
          window=False, sink=None, emit_kv=False):
    bsz, seq, _ = u3.shape
    gw = GROUP * HEAD_DIM
    heads = kv_heads * GROUP
    ctx_len = 0 if ctx is None else ctx[0].shape[1]
    pad = WINDOW if window else 0
    assert not window or (tq == WINDOW and seq // tq >= 2)
    assert seq % (tq * nqb) == 0

    def wide(off):
        return pl.BlockSpec((None, tq * nqb, gw), lambda b, h, i: (b, i, off // gw + h))

    def narrow(off, rows):
        return pl.BlockSpec((None, rows, HEAD_DIM), lambda b, h, i: (b, 0, off // HEAD_DIM + h))

    def whole(shape):
        return pl.BlockSpec(shape, lambda b, h, i: (0,) * len(shape))

    in_specs = [wide(q_off), wide(g_off), narrow(k_off, seq), narrow(v_off, seq)]
    args = [u3, u3, u3, u3]
    if ctx is not None:
        for cache in ctx:
            in_specs.append(narrow(0, ctx_len))
            args.append(cache.reshape(bsz, ctx_len, kv_heads * HEAD_DIM))
    if rope is not None:
        for t in rope:
            in_specs.append(whole(t.shape))
            args.append(t)
    if window:
        in_specs.append(whole((3, GROUP * tq, 3 * WINDOW)))
        args.append(_window_bias(tq))
    in_specs += [whole((2, HEAD_DIM)), whole((2, HEAD_DIM))]
    args += [jnp.stack([qnorm, qnorm[_ROT_PARTNER]]), jnp.stack([knorm, knorm[_ROT_PARTNER]])]
    if sink is not None:
        in_specs.append(pl.BlockSpec(memory_space=pltpu.SMEM))
        args.append(sink)
    out_specs = [pl.BlockSpec((None, tq * nqb, gw), lambda b, h, i: (b, i, h))]
    out_shape = [jax.ShapeDtypeStruct((bsz, seq, heads * HEAD_DIM), BF16)]
    if emit_kv:
        for _ in range(2):
            out_specs.append(narrow(0, seq))
            out_shape.append(jax.ShapeDtypeStruct((bsz, seq, kv_heads * HEAD_DIM), F32))
    rows = seq + 2 * pad + ctx_len

    def call(shift_free, *operands):
        kern = functools.partial(_attn_kernel, seq=seq, tq=tq, nqb=nqb, ctx_len=ctx_len,
                                 rope=rope is not None, window=window, sink=sink is not None,
                                 emit_kv=emit_kv, shift_free=shift_free)
        return pl.pallas_call(
            kern,
            grid=(bsz, kv_heads, seq // (tq * nqb)),
            in_specs=in_specs,
            out_specs=out_specs,
            out_shape=out_shape,
            scratch_shapes=[pltpu.VMEM((rows, HEAD_DIM), BF16),
                            pltpu.VMEM((rows, 2 * HEAD_DIM), BF16)]
            + ([pltpu.VMEM((seq, HEAD_DIM), F32)] * 2 if rope is not None else []),
            compiler_params=_params("parallel", "parallel", "arbitrary"),
            name="attn",
        )(*operands)

    k_norm = math.sqrt(HEAD_DIM) * jnp.max(jnp.abs(knorm))
    if ctx is not None:
        k_norm = jnp.maximum(k_norm, jnp.sqrt(jnp.max(jnp.sum(jnp.square(ctx[0]), axis=-1))))
    bound = jnp.max(jnp.abs(qnorm)) * LOG2E * k_norm * LOGIT_BOUND_MARGIN
    return lax.cond(bound <= MAX_UNSHIFTED_LOGIT,
                    functools.partial(call, True), functools.partial(call, False), *args)


def _rope_tables(n_tokens):
    rows = n_tokens // GRID_W
    row = jnp.repeat(jnp.arange(rows), GRID_W).astype(F32)
    col = (jnp.arange(rows * GRID_W) % GRID_W).astype(F32)
    inv = ROPE_THETA ** (-jnp.arange(ROPE_QUARTER, dtype=F32) / ROPE_QUARTER)
    ar = row[:, None] * inv
    ac = col[:, None] * inv
    ang = jnp.concatenate([ar, ar, ac, ac], axis=-1)
    rot = np.zeros((2 * HEAD_DIM, 2 * HEAD_DIM), np.float32)
    rot[_ROT_PARTNER, np.arange(HEAD_DIM)] = np.where(_ROT_FIRST, -1.0, 1.0)
    rot[HEAD_DIM:, HEAD_DIM:] = 1.0 / HEAD_DIM
    return jnp.cos(ang), jnp.sin(ang), jnp.asarray(rot, BF16)


def _tile(m, pref):
    t = pref
    while m % t:
        t //= 2
    return t


def kernel(x_prompt, x_sample, state_l0_hgrn, cache_l0_k, cache_l0_v, cache_l1_k, cache_l1_v, c, c_ctx, lb_gamma, l0_norm, l0_w_mod, l0_b_mod, l0_w_in, l0_w_out, l0_a_onorm, l0_b_qnorm, l0_b_knorm, l1_norm, l1_w_mod, l1_b_mod, l1_w_in, l1_w_out, l1_c_qnorm, l1_c_knorm, l1_c_sink):
    pb, pl_, d = x_prompt.shape
    sb, sl, _ = x_sample.shape
    aw = A_HEADS * HEAD_DIM
    bkv = B_KV_HEADS * HEAD_DIM
    bw = B_KV_HEADS * GROUP * HEAD_DIM
    ckv = C_KV_HEADS * HEAD_DIM
    cw = C_KV_HEADS * GROUP * HEAD_DIM

    lb = jnp.cumsum(jax.nn.softmax(lb_gamma.astype(F32), axis=0), axis=0)[0]
    rope = _rope_tables(sl)

    nrow = -(-(sb + 1) // 8) * 8
    cond = jnp.zeros((nrow, d), F32).at[:sb].set(c).at[sb].set(c_ctx)
    xs = (x_prompt.reshape(pb * pl_, d), x_sample.reshape(sb * sl, d))
    tms = (_tile(pb * pl_, 1024), _tile(sl, 1024))
    rows_per_mod = (pb * pl_, sl)

    def mods(w_mod, b_mod):
        m = _adaln(cond, w_mod, b_mod)
        parts = [m[:, i * d:(i + 1) * d] for i in range(3)]
        return ([p[sb:sb + 1, None, :] for p in parts], [p[:sb, None, :] for p in parts])

    def tn_for(n):
        for t in (1536, 1280, 1024, 768, 512, 256, 128):
            if n % t == 0:
                return t
        return n

    mod_p, mod_s = mods(l0_w_mod, l0_b_mod)
    w_in = l0_w_in.astype(BF16)
    w_out = l0_w_out.astype(BF16)
    tn = tn_for(w_in.shape[1])
    u_p = _inproj(xs[0], l0_norm, mod_p[0], mod_p[1], w_in, rows_per_mod[0], tms[0], tn).reshape(pb, pl_, -1)
    u_s = _inproj(xs[1], l0_norm, mod_s[0], mod_s[1], w_in, rows_per_mod[1], tms[1], tn).reshape(sb, sl, -1)

    oa_p, new_state = _hgrn(u_p, lb, l0_a_onorm, None, hb=A_HEADS, emit_state=True)
    (oa_s,) = _hgrn(u_s, lb, l0_a_onorm, state_l0_hgrn, hb=4, emit_state=False)

    q_off = 5 * aw
    k_off = q_off + bw
    v_off = k_off + bkv
    g_off = v_off + bkv
    ob_p, k0, v0 = _attn(u_p, q_off, k_off, v_off, g_off, B_KV_HEADS, l0_b_qnorm, l0_b_knorm,
                         tq=128, nqb=2, emit_kv=True)
    (ob_s,) = _attn(u_s, q_off, k_off, v_off, g_off, B_KV_HEADS, l0_b_qnorm, l0_b_knorm,
                    tq=128, nqb=_tile(sl // 128, 8), ctx=(cache_l0_k, cache_l0_v), rope=rope)

    otm = (_tile(pb * pl_, 512), _tile(sl, 512))
    y_p = _outproj([oa_p.reshape(pb * pl_, aw), ob_p.reshape(pb * pl_, bw)], w_out, xs[0], mod_p[2],
                   rows_per_mod[0], otm[0])
    y_s = _outproj([oa_s.reshape(sb * sl, aw), ob_s.reshape(sb * sl, bw)], w_out, xs[1], mod_s[2],
                   rows_per_mod[1], otm[1])

    mod_p, mod_s = mods(l1_w_mod, l1_b_mod)
    w_in = l1_w_in.astype(BF16)
    w_out = l1_w_out.astype(BF16)
    tn = tn_for(w_in.shape[1])
    u_p = _inproj(y_p, l1_norm, mod_p[0], mod_p[1], w_in, rows_per_mod[0], tms[0], tn).reshape(pb, pl_, -1)
    u_s = _inproj(y_s, l1_norm, mod_s[0], mod_s[1], w_in, rows_per_mod[1], tms[1], tn).reshape(sb, sl, -1)

    k_off = cw
    v_off = k_off + ckv
    g_off = v_off + ckv
    oc_p, k1, v1 = _attn(u_p, 0, k_off, v_off, g_off, C_KV_HEADS, l1_c_qnorm, l1_c_knorm,
                         tq=128, nqb=2, sink=l1_c_sink, emit_kv=True)
    (oc_s,) = _attn(u_s, 0, k_off, v_off, g_off, C_KV_HEADS, l1_c_qnorm, l1_c_knorm,
                    tq=WINDOW, nqb=_tile(sl // WINDOW, 16), ctx=(cache_l1_k, cache_l1_v), rope=rope, window=True, sink=l1_c_sink)

    z_p = _outproj([oc_p.reshape(pb * pl_, cw)], w_out, y_p, mod_p[2], rows_per_mod[0], otm[0])
    z_s = _outproj([oc_s.reshape(sb * sl, cw)], w_out, y_s, mod_s[2], rows_per_mod[1], otm[1])

    return (z_p.reshape(pb, pl_, d), z_s.reshape(sb, sl, d), new_state,
            k0.reshape(pb, pl_, B_KV_HEADS, HEAD_DIM), v0.reshape(pb, pl_, B_KV_HEADS, HEAD_DIM),
            k1.reshape(pb, pl_, C_KV_HEADS, HEAD_DIM), v1.reshape(pb, pl_, C_KV_HEADS, HEAD_DIM))
```

```python
import functools
import math

import jax
import jax.numpy as jnp
import numpy as np
from jax import lax
from jax.experimental import pallas as pl
from jax.experimental.pallas import tpu as pltpu

F32 = jnp.float32
BF16 = jnp.bfloat16

HEAD_DIM = 128
GRID_W = 64
ROPE_QUARTER = HEAD_DIM // 4
ROPE_THETA = 10000.0
NORM_EPS = 1e-6
A_HEADS = 8
A_CHUNK = 32
B_KV_HEADS = 2
C_KV_HEADS = 4
GROUP = 4
WINDOW = 128
NEG = -1e30
LOG2E = math.log2(math.e)
MAX_UNSHIFTED_LOGIT = 60.0
LOGIT_BOUND_MARGIN = 1.1
VMEM_LIMIT = 56 * 1024 * 1024

_ROT_FIRST = (np.arange(HEAD_DIM) % (2 * ROPE_QUARTER)) < ROPE_QUARTER
_ROT_PARTNER = np.where(_ROT_FIRST, np.arange(HEAD_DIM) + ROPE_QUARTER, np.arange(HEAD_DIM) - ROPE_QUARTER)

_NT = (((1,), (1,)), ((), ()))
_TN = (((0,), (0,)), ((), ()))


def _params(*sem):
    return pltpu.CompilerParams(dimension_semantics=sem, vmem_limit_bytes=VMEM_LIMIT)


def _silu(x):
    return x * jax.nn.sigmoid(x)


def _rms(x, g):
    ms = jnp.mean(x * x, axis=-1, keepdims=True)
    return x * lax.rsqrt(ms + NORM_EPS) * g


def _adaln_kernel(c_ref, w_ref, b_ref, o_ref):
    a = _silu(c_ref[...]).astype(BF16)
    o_ref[...] = jnp.dot(a, w_ref[...].astype(BF16), preferred_element_type=F32) + b_ref[...]


def _adaln(cond, w_mod, b_mod):
    r, d = cond.shape
    n = w_mod.shape[1]
    tn = _tile(n, 1024)
    return pl.pallas_call(
        _adaln_kernel,
        grid=(n // tn,),
        in_specs=[pl.BlockSpec((r, d), lambda j: (0, 0)),
                  pl.BlockSpec((d, tn), lambda j: (0, j)),
                  pl.BlockSpec((1, tn), lambda j: (0, j))],
        out_specs=pl.BlockSpec((r, tn), lambda j: (0, j)),
        out_shape=jax.ShapeDtypeStruct((r, n), F32),
        compiler_params=_params("parallel"),
        name="adaln",
    )(cond, w_mod, b_mod.reshape(1, n))


def _inproj_kernel(x_ref, g_ref, sh_ref, sc_ref, w_ref, o_ref, h_ref, *, rc, n_tiles, norm_steps):
    i = pl.program_id(0)
    j = pl.program_id(1)
    slice_rows = x_ref.shape[0] // norm_steps

    def matmul():
        o_ref[...] = jnp.dot(h_ref[(i + 1) % 2], w_ref[...], preferred_element_type=F32).astype(o_ref.dtype)

    def norm_slice():
        gain = g_ref[...] * (1.0 + sc_ref[...])
        sh = sh_ref[...]
        for r in range(slice_rows // rc):
            rows = pl.ds(pl.multiple_of(j * slice_rows + r * rc, rc), rc)
            h_ref[i % 2, rows, :] = (_rms(x_ref[rows, :], gain) + sh).astype(BF16)

    has_mm = i >= 1
    has_norm = (i < n_tiles) & (j < norm_steps)

    @pl.when(has_mm & has_norm)
    def _():
        matmul()
        norm_slice()

    @pl.when(has_mm & jnp.logical_not(has_norm))
    def _():
        matmul()

    @pl.when(jnp.logical_not(has_mm) & has_norm)
    def _():
        norm_slice()


def _inproj(x2d, norm_g, shift, scale, w, rows_per_mod, tm, tn):
    m, d = x2d.shape
    n = w.shape[1]
    n_tiles = m // tm
    n_cols = n // tn
    norm_steps = max(s for s in (1, 2, 4, 8) if s <= n_cols and (tm // s) % 32 == 0)

    def tile(i):
        return jnp.minimum(i, n_tiles - 1)

    def col(i, j):
        return jnp.where(i == 0, 0, j)

    mod_spec = pl.BlockSpec((None, 1, d), lambda i, j: ((tile(i) * tm) // rows_per_mod, 0, 0))
    return pl.pallas_call(
        functools.partial(_inproj_kernel, rc=32, n_tiles=n_tiles, norm_steps=norm_steps),
        grid=(n_tiles + 1, n_cols),
        in_specs=[pl.BlockSpec((tm, d), lambda i, j: (tile(i), 0)),
                  pl.BlockSpec((1, d), lambda i, j: (0, 0)),
                  mod_spec, mod_spec,
                  pl.BlockSpec((d, tn), lambda i, j: (0, col(i, j)))],
        out_specs=pl.BlockSpec((tm, tn), lambda i, j: (jnp.maximum(i - 1, 0), col(i, j))),
        out_shape=jax.ShapeDtypeStruct((m, n), BF16),
        scratch_shapes=[pltpu.VMEM((2, tm, d), BF16)],
        compiler_params=_params("arbitrary", "arbitrary"),
        name="inproj",
    )(x2d, norm_g.reshape(1, d), shift, scale, w)


def _outproj_kernel(*refs, widths):
    o_refs = refs[:len(widths)]
    w_ref, x_ref, gt_ref, y_ref = refs[len(widths):]
    acc = None
    start = 0
    for o_ref, wd in zip(o_refs, widths):
        part = jnp.dot(o_ref[...], w_ref[start:start + wd, :], preferred_element_type=F32)
        acc = part if acc is None else acc + part
        start += wd
    y_ref[...] = x_ref[...] + gt_ref[...] * acc


def _outproj(os, w, x2d, gate, rows_per_mod, tm):
    m, d = x2d.shape
    widths = tuple(o.shape[1] for o in os)
    k = w.shape[0]
    return pl.pallas_call(
        functools.partial(_outproj_kernel, widths=widths),
        grid=(m // tm,),
        in_specs=[pl.BlockSpec((tm, wd), lambda i: (i, 0)) for wd in widths] + [
            pl.BlockSpec((k, d), lambda i: (0, 0)),
            pl.BlockSpec((tm, d), lambda i: (i, 0)),
            pl.BlockSpec((None, 1, d), lambda i: ((i * tm) // rows_per_mod, 0, 0))],
        out_specs=pl.BlockSpec((tm, d), lambda i: (i, 0)),
        out_shape=jax.ShapeDtypeStruct((m, d), F32),
        compiler_params=_params("parallel"),
        name="outproj",
    )(*os, w, x2d, gate)


SUBLANES = 8
HGRN_BLOCK = 128


def _chunk_cumprod(f, reverse):
    n = f.shape[0]
    tiles = n // SUBLANES
    per_chunk = A_CHUNK // SUBLANES
    x = f.reshape(tiles, SUBLANES, HEAD_DIM)
    sub = lax.broadcasted_iota(jnp.int32, x.shape, 1)
    s = 1
    while s < SUBLANES:
        if reverse:
            x = x * jnp.where(sub < SUBLANES - s, pltpu.roll(x, SUBLANES - s, axis=1), 1.0)
        else:
            x = x * jnp.where(sub >= s, pltpu.roll(x, s, axis=1), 1.0)
        s *= 2
    edge = 0 if reverse else SUBLANES - 1
    out, tot = [], []
    for c in range(n // A_CHUNK):
        ts = [x[c * per_chunk + i] for i in range(per_chunk)]
        order = range(per_chunk - 1, -1, -1) if reverse else range(per_chunk)
        carry = None
        done = {}
        for i in order:
            t = ts[i] if carry is None else ts[i] * carry
            done[i] = t
            carry = t[edge:edge + 1, :]
        out += [done[i] for i in range(per_chunk)]
        tot += [jnp.broadcast_to(carry, (A_CHUNK, HEAD_DIM))]
    return jnp.concatenate(out, axis=0), jnp.concatenate(tot, axis=0)


def _hgrn_kernel(*refs, seq, hb, has_s0, emit_state):
    q_ref, ff_ref, fb_ref, v_ref, g_ref, lb_ref, on_ref = refs[:7]
    pos = 7
    s0_ref = None
    if has_s0:
        s0_ref = refs[pos]
        pos += 1
    o_ref = refs[pos]
    pos += 1
    so_ref = None
    if emit_state:
        so_ref = refs[pos]
        pos += 1
    oacc_ref, st_ref, qd_ref, ke_ref, dec_ref = refs[pos:]

    nc = seq // A_CHUNK
    rb = HGRN_BLOCK
    cpb = rb // A_CHUNK
    ri = lax.broadcasted_iota(jnp.int32, (rb, rb), 0)
    ci = lax.broadcasted_iota(jnp.int32, (rb, rb), 1)
    shift = A_CHUNK.bit_length() - 1
    same_chunk = jnp.right_shift(ri, shift) == jnp.right_shift(ci, shift)
    masks = (same_chunk & (ri >= ci), same_chunk & (ci >= ri))

    for j in range(hb):
        for d in range(2):
            lanes = slice(d * HEAD_DIM, (d + 1) * HEAD_DIM)
            if has_s0:
                st_ref[j, :, lanes] = s0_ref[d, j].T
            else:
                st_ref[j, :, lanes] = jnp.zeros((HEAD_DIM, HEAD_DIM), F32)

    def stage1(blk, carry):
        rows = pl.ds(pl.multiple_of(blk * rb, rb), rb)
        heads = range(hb)
        chains = [(j, d) for j in heads for d in range(2)]
        cols = [slice(j * HEAD_DIM, (j + 1) * HEAD_DIM) for j in heads]
        qs = [_silu(q_ref[rows, cols[j]].astype(F32)) for j in heads]
        ts, fs, scans, qds, kinvs, atts = {}, {}, {}, {}, {}, {}
        for j, d in chains:
            fr = (fb_ref if d else ff_ref)[rows, cols[j]].astype(F32)
            lb = lb_ref[d:d + 1, cols[j]]
            ts[j, d] = (1.0 - lb) * jax.nn.sigmoid(fr)
            fs[j, d] = lb + ts[j, d]
        for j, d in chains:
            scans[j, d] = _chunk_cumprod(fs[j, d], reverse=bool(d))
        for j, d in chains:
            eb, dec = scans[j, d]
            lb = lb_ref[d:d + 1, cols[j]]
            qds[j, d] = (qs[j] * eb).astype(BF16)
            kinvs[j, d] = ((1.0 - lb) - ts[j, d]) * (1.0 / eb)
            qd_ref[d, j, rows, :] = qds[j, d]
            ke_ref[d, j, rows, :] = (kinvs[j, d] * dec).astype(BF16)
            for c in range(cpb):
                dec_ref[d, j, pl.ds(blk * cpb + c, 1), :] = dec[c * A_CHUNK:c * A_CHUNK + 1, :]
        for j, d in chains:
            att = lax.dot_general(qds[j, d], kinvs[j, d].astype(BF16), _NT, preferred_element_type=F32)
            atts[j, d] = jnp.where(masks[d], att, 0.0).astype(BF16)
        for j in heads:
            v = v_ref[rows, cols[j]]
            oacc_ref[rows, cols[j]] = (jnp.dot(atts[j, 0], v, preferred_element_type=F32)
                                       + jnp.dot(atts[j, 1], v, preferred_element_type=F32))
        return carry

    zero = jnp.zeros((A_CHUNK, HEAD_DIM), BF16)

    def blockdiag(a, b):
        return jnp.concatenate([jnp.concatenate([a, zero], axis=1), jnp.concatenate([zero, b], axis=1)], axis=0)

    def stage2(n, carry):
        cf, cb = n, nc - 1 - n
        rows_f = pl.ds(pl.multiple_of(cf * A_CHUNK, A_CHUNK), A_CHUNK)
        rows_b = pl.ds(pl.multiple_of(cb * A_CHUNK, A_CHUNK), A_CHUNK)
        for j in range(hb):
            cols = slice(j * HEAD_DIM, (j + 1) * HEAD_DIM)
            st = st_ref[j]
            qd = blockdiag(qd_ref[0, j, rows_f, :], qd_ref[1, j, rows_b, :])
            ke = blockdiag(ke_ref[0, j, rows_f, :], ke_ref[1, j, rows_b, :])
            v = jnp.concatenate([v_ref[rows_f, cols], v_ref[rows_b, cols]], axis=0)
            dec = jnp.concatenate([dec_ref[0, j, pl.ds(cf, 1), :], dec_ref[1, j, pl.ds(cb, 1), :]], axis=1)
            o = lax.dot_general(qd, st.astype(BF16), _NT, preferred_element_type=F32)
            st_ref[j] = st * dec + lax.dot_general(v, ke, _TN, preferred_element_type=F32)
            oacc_ref[rows_f, cols] += o[:A_CHUNK, :]
            oacc_ref[rows_b, cols] += o[A_CHUNK:, :]
        return carry

    nb = seq // rb
    mid = nb // 2
    on = on_ref[...]

    def merged(p, carry):
        stage1(p, carry)
        stage1(nb - 1 - p, carry)
        for i in range(cpb):
            stage2(p * cpb + i, carry)
        return carry

    def finish_block(r):
        rows = pl.ds(pl.multiple_of(r * rb, rb), rb)
        for j in range(hb):
            cols = slice(j * HEAD_DIM, (j + 1) * HEAD_DIM)
            y = _rms(oacc_ref[rows, cols], on)
            o_ref[rows, cols] = (y * _silu(g_ref[rows, cols].astype(F32))).astype(o_ref.dtype)

    def tail(g, carry):
        finish_block(mid - g)
        finish_block(mid + g - 1)
        for i in range(cpb):
            stage2(nc // 2 + g * cpb + i, carry)
        return carry

    lax.fori_loop(0, mid, merged, 0)
    for i in range(cpb):
        stage2(nc // 2 + i, 0)
    lax.fori_loop(1, mid, tail, 0)
    finish_block(0)
    finish_block(nb - 1)

    if emit_state:
        for j in range(hb):
            for d in range(2):
                so_ref[d, j] = st_ref[j, :, d * HEAD_DIM:(d + 1) * HEAD_DIM].T


def _hgrn(u3, lb, onorm, s0, hb, emit_state):
    bsz, seq, _ = u3.shape
    bw = hb * HEAD_DIM
    nh = A_HEADS // hb

    def seg(s):
        return pl.BlockSpec((None, seq, bw), lambda b, h: (b, 0, s * nh + h))

    in_specs = [seg(0), seg(1), seg(2), seg(3), seg(4),
                pl.BlockSpec((2, bw), lambda b, h: (0, h)),
                pl.BlockSpec((1, HEAD_DIM), lambda b, h: (0, 0))]
    args = [u3, u3, u3, u3, u3, lb, onorm.reshape(1, HEAD_DIM)]
    st_spec = pl.BlockSpec((None, 2, hb, HEAD_DIM, HEAD_DIM), lambda b, h: (b, 0, h, 0, 0))
    if s0 is not None:
        in_specs.append(st_spec)
        args.append(s0)
    out_specs = [pl.BlockSpec((None, seq, bw), lambda b, h: (b, 0, h))]
    out_shape = [jax.ShapeDtypeStruct((bsz, seq, A_HEADS * HEAD_DIM), BF16)]
    if emit_state:
        out_specs.append(st_spec)
        out_shape.append(jax.ShapeDtypeStruct((bsz, 2, A_HEADS, HEAD_DIM, HEAD_DIM), F32))
    return pl.pallas_call(
        functools.partial(_hgrn_kernel, seq=seq, hb=hb, has_s0=s0 is not None,
                          emit_state=emit_state),
        grid=(bsz, nh),
        in_specs=in_specs,
        out_specs=out_specs,
        out_shape=out_shape,
        scratch_shapes=[pltpu.VMEM((seq, bw), F32),
                        pltpu.VMEM((hb, HEAD_DIM, 2 * HEAD_DIM), F32),
                        pltpu.VMEM((2, hb, seq, HEAD_DIM), BF16),
                        pltpu.VMEM((2, hb, seq, HEAD_DIM), BF16),
                        pltpu.VMEM((2, hb, seq // A_CHUNK, HEAD_DIM), F32)],
        compiler_params=_params("parallel", "parallel"),
        name="hgrn",
    )(*args)


def _norm_rope(xb, g_cos, gp_sin, rot_ref, out_scale):
    x = xb.astype(F32)
    sq = (x * x).astype(BF16)
    y = x * g_cos
    if gp_sin is not None:
        both = jnp.dot(jnp.concatenate([xb, sq], axis=1), rot_ref[...], preferred_element_type=F32)
        y = y + both[:, :HEAD_DIM] * gp_sin
        ms = both[:, HEAD_DIM:]
    else:
        mean_mat = jnp.full((HEAD_DIM, HEAD_DIM), 1.0 / HEAD_DIM, BF16)
        ms = jnp.dot(sq, mean_mat, preferred_element_type=F32)
    r = lax.rsqrt(ms + NORM_EPS)
    if out_scale != 1.0:
        r = r * out_scale
    return y * r


def _attn_kernel(*refs, seq, tq, nqb, ctx_len, rope, window, sink, emit_kv, shift_free):
    refs = list(refs)
    q_ref, gate_ref, k_ref, v_ref = refs[:4]
    pos = 4
    kc_ref = vc_ref = cos_ref = sin_ref = rot_ref = bias_ref = sink_ref = ko_ref = vo_ref = None
    if ctx_len:
        kc_ref, vc_ref = refs[pos:pos + 2]
        pos += 2
    if rope:
        cos_ref, sin_ref, rot_ref = refs[pos:pos + 3]
        pos += 3
    if window:
        bias_ref = refs[pos]
        pos += 1
    qn_ref, kn_ref = refs[pos:pos + 2]
    pos += 2
    if sink:
        sink_ref = refs[pos]
        pos += 1
    o_ref = refs[pos]
    pos += 1
    if emit_kv:
        ko_ref, vo_ref = refs[pos:pos + 2]
        pos += 2
    ks_ref, vs_ref = refs[pos:pos + 2]
    pos += 2
    qc_ref = qs_ref = None
    if rope:
        qc_ref, qs_ref = refs[pos:]

    kvh = pl.program_id(1)
    step = pl.program_id(2)
    nq = seq // tq
    pad = WINDOW if window else 0
    ctx_off = seq + 2 * pad
    qscale = LOG2E / math.sqrt(HEAD_DIM)

    @pl.when(step == 0)
    def _():
        if window:
            zeros = jnp.zeros((pad, HEAD_DIM), BF16)
            for ref in (ks_ref, vs_ref):
                ref[0:pad, 0:HEAD_DIM] = zeros
                ref[pad + seq:ctx_off, 0:HEAD_DIM] = zeros
        rc = 256
        kg = kn_ref[0:1, :]
        kgp = kn_ref[1:2, :]
        ones = jnp.ones((rc, HEAD_DIM), BF16)

        def body(r, carry):
            rows = pl.ds(pl.multiple_of(r * rc, rc), rc)
            dst = pl.ds(pl.multiple_of(pad + r * rc, HEAD_DIM), rc)
            if rope:
                cos = cos_ref[rows, :]
                sin = sin_ref[rows, :]
                qc_ref[rows, :] = cos * qn_ref[0:1, :]
                qs_ref[rows, :] = sin * qn_ref[1:2, :]
                k = _norm_rope(k_ref[rows, :], cos * kg, sin * kgp, rot_ref, 1.0)
            else:
                k = _norm_rope(k_ref[rows, :], kg, None, None, 1.0)
            if emit_kv:
                ko_ref[rows, :] = k
                vo_ref[rows, :] = v_ref[rows, :].astype(F32)
            ks_ref[dst, :] = k.astype(BF16)
            vs_ref[dst, 0:HEAD_DIM] = v_ref[rows, :]
            return carry

        lax.fori_loop(0, seq // rc, body, 0)
        if ctx_len:
            ks_ref[ctx_off:ctx_off + ctx_len, :] = kc_ref[...].astype(BF16)
            vs_ref[ctx_off:ctx_off + ctx_len, 0:HEAD_DIM] = vc_ref[...].astype(BF16)

        def fill(r, carry):
            vs_ref[pl.ds(pl.multiple_of(r * rc, rc), rc), HEAD_DIM:2 * HEAD_DIM] = ones
            return carry

        lax.fori_loop(0, vs_ref.shape[0] // rc, fill, 0)

    sink_col = None
    if sink:
        sink_col = jnp.concatenate(
            [jnp.full((tq, 1), sink_ref[kvh * GROUP + h] * LOG2E, F32) for h in range(GROUP)], axis=0)

    def prep(t):
        qi = step * nqb + t
        qrows = pl.ds(pl.multiple_of(qi * tq, tq), tq)
        xs = []
        for h in range(GROUP):
            xb = q_ref[t * tq:(t + 1) * tq, h * HEAD_DIM:(h + 1) * HEAD_DIM]
            if rope:
                x = _norm_rope(xb, qc_ref[qrows, :], qs_ref[qrows, :], rot_ref, qscale)
            else:
                x = _norm_rope(xb, qn_ref[0:1, :], None, None, qscale)
            xs.append(x.astype(BF16))
        return jnp.concatenate(xs, axis=0)

    def key_parts(t):
        qi = step * nqb + t
        if not window:
            return [(slice(None), None)]
        start = pl.multiple_of(qi * tq, HEAD_DIM)
        edge = jnp.where(qi == 0, 0, jnp.where(qi == nq - 1, 2, 1))
        parts = [(pl.ds(start, 3 * WINDOW), bias_ref[edge])]
        if ctx_len:
            parts.append((slice(ctx_off, ctx_off + ctx_len), None))
        return parts

    def logits(q, parts):
        ss = []
        for rows, bias in parts:
            s = lax.dot_general(q, ks_ref[rows, :], _NT, preferred_element_type=F32)
            ss.append(s if bias is None else s + bias)
        return ss

    def row_max(ss):
        slabs = [s[:, c:c + HEAD_DIM] for s in ss for c in range(0, s.shape[1], HEAD_DIM)]
        return jnp.max(functools.reduce(jnp.maximum, slabs), axis=-1, keepdims=True)

    def weighted(ss, m, parts):
        acc = None
        for s, (rows, _) in zip(ss, parts):
            p = jnp.exp2(s if m is None else s - m)
            pv = jnp.dot(p.astype(BF16), vs_ref[rows, :], preferred_element_type=F32)
            acc = pv if acc is None else acc + pv
        return acc

    def finish(t, acc, m):
        l = acc[:, HEAD_DIM:]
        if sink:
            l = l + jnp.exp2(sink_col if m is None else sink_col - m)
        for h in range(GROUP):
            cols = slice(h * HEAD_DIM, (h + 1) * HEAD_DIM)
            rows = slice(h * tq, (h + 1) * tq)
            gate = gate_ref[t * tq:(t + 1) * tq, cols].astype(F32)
            inv = 1.0 / (l[rows, :] * (1.0 + jnp.exp(-gate)))
            o_ref[t * tq:(t + 1) * tq, cols] = (acc[rows, :HEAD_DIM] * gate * inv).astype(o_ref.dtype)

    blocks = range(nqb)
    parts = [key_parts(t) for t in blocks]
    ss, ms = {}, {}
    q_next = prep(0)
    for t in blocks:
        q = q_next
        if t + 1 < nqb:
            q_next = prep(t + 1)
        ss[t] = logits(q, parts[t])
        ms[t] = None if shift_free else row_max(ss[t])
        if t >= 1:
            finish(t - 1, weighted(ss.pop(t - 1), ms[t - 1], parts[t - 1]), ms.pop(t - 1))
    finish(nqb - 1, weighted(ss[nqb - 1], ms[nqb - 1], parts[nqb - 1]), ms[nqb - 1])


def _window_bias(tq):
    r = (jnp.arange(GROUP * tq) % tq)[:, None]
    c = jnp.arange(3 * WINDOW)[None, :]
    band = (c >= r) & (c <= r + 2 * WINDOW)
    keep = jnp.stack([band & (c >= WINDOW), band, band & (c < 2 * WINDOW)])
    return jnp.where(keep, 0.0, NEG).astype(F32)


def _attn(u3, q_off, k_off, v_off, g_off, kv_heads, qnorm, knorm, *, tq, nqb, ctx=None, rope=None,
          window=False, sink=None, emit_kv=False):
    bsz, seq, _ = u3.shape
    gw = GROUP * HEAD_DIM
    heads = kv_heads * GROUP
    ctx_len = 0 if ctx is None else ctx[0].shape[1]
    pad = WINDOW if window else 0
    assert not window or (tq == WINDOW and seq // tq >= 2)
    assert seq % (tq * nqb) == 0

    def wide(off):
        return pl.BlockSpec((None, tq * nqb, gw), lambda b, h, i: (b, i, off // gw + h))

    def narrow(off, rows):
        return pl.BlockSpec((None, rows, HEAD_DIM), lambda b, h, i: (b, 0, off // HEAD_DIM + h))

    def whole(shape):
        return pl.BlockSpec(shape, lambda b, h, i: (0,) * len(shape))

    in_specs = [wide(q_off), wide(g_off), narrow(k_off, seq), narrow(v_off, seq)]
    args = [u3, u3, u3, u3]
    if ctx is not None:
        for cache in ctx:
            in_specs.append(narrow(0, ctx_len))
            args.append(cache.reshape(bsz, ctx_len, kv_heads * HEAD_DIM))
    if rope is not None:
        for t in rope:
            in_specs.append(whole(t.shape))
            args.append(t)
    if window:
        in_specs.append(whole((3, GROUP * tq, 3 * WINDOW)))
        args.append(_window_bias(tq))
    in_specs += [whole((2, HEAD_DIM)), whole((2, HEAD_DIM))]
    def with_partner(g):
        return jnp.stack([g, g.reshape(2, 2, ROPE_QUARTER)[:, ::-1, :].reshape(HEAD_DIM)])

    args += [with_partner(qnorm), with_partner(knorm)]
    if sink is not None:
        in_specs.append(pl.BlockSpec(memory_space=pltpu.SMEM))
        args.append(sink)
    out_specs = [pl.BlockSpec((None, tq * nqb, gw), lambda b, h, i: (b, i, h))]
    out_shape = [jax.ShapeDtypeStruct((bsz, seq, heads * HEAD_DIM), BF16)]
    if emit_kv:
        for _ in range(2):
            out_specs.append(narrow(0, seq))
            out_shape.append(jax.ShapeDtypeStruct((bsz, seq, kv_heads * HEAD_DIM), F32))
    rows = seq + 2 * pad + ctx_len

    def call(shift_free, *operands):
        kern = functools.partial(_attn_kernel, seq=seq, tq=tq, nqb=nqb, ctx_len=ctx_len,
                                 rope=rope is not None, window=window, sink=sink is not None,
                                 emit_kv=emit_kv, shift_free=shift_free)
        return pl.pallas_call(
            kern,
            grid=(bsz, kv_heads, seq // (tq * nqb)),
            in_specs=in_specs,
            out_specs=out_specs,
            out_shape=out_shape,
            scratch_shapes=[pltpu.VMEM((rows, HEAD_DIM), BF16),
                            pltpu.VMEM((rows, 2 * HEAD_DIM), BF16)]
            + ([pltpu.VMEM((seq, HEAD_DIM), F32)] * 2 if rope is not None else []),
            compiler_params=_params("parallel", "parallel", "arbitrary"),
            name="attn",
        )(*operands)

    k_norm = math.sqrt(HEAD_DIM) * jnp.max(jnp.abs(knorm))
    if ctx is not None:
        k_norm = jnp.maximum(k_norm, jnp.sqrt(jnp.max(jnp.sum(jnp.square(ctx[0]), axis=-1))))
    bound = jnp.max(jnp.abs(qnorm)) * LOG2E * k_norm * LOGIT_BOUND_MARGIN
    return lax.cond(bound <= MAX_UNSHIFTED_LOGIT,
                    functools.partial(call, True), functools.partial(call, False), *args)


def _rope_tables(n_tokens):
    rows = n_tokens // GRID_W
    row = np.repeat(np.arange(rows), GRID_W).astype(np.float32)
    col = (np.arange(rows * GRID_W) % GRID_W).astype(np.float32)
    inv = np.float32(ROPE_THETA) ** (-np.arange(ROPE_QUARTER, dtype=np.float32) / np.float32(ROPE_QUARTER))
    ar = row[:, None] * inv
    ac = col[:, None] * inv
    ang = np.concatenate([ar, ar, ac, ac], axis=-1).astype(np.float32)
    rot = np.zeros((2 * HEAD_DIM, 2 * HEAD_DIM), np.float32)
    rot[_ROT_PARTNER, np.arange(HEAD_DIM)] = np.where(_ROT_FIRST, -1.0, 1.0)
    rot[HEAD_DIM:, HEAD_DIM:] = 1.0 / HEAD_DIM
    return jnp.asarray(np.cos(ang), F32), jnp.asarray(np.sin(ang), F32), jnp.asarray(rot, BF16)


def _tile(m, pref):
    t = pref
    while m % t:
        t //= 2
    return t


def kernel(x_prompt, x_sample, state_l0_hgrn, cache_l0_k, cache_l0_v, cache_l1_k, cache_l1_v, c, c_ctx, lb_gamma, l0_norm, l0_w_mod, l0_b_mod, l0_w_in, l0_w_out, l0_a_onorm, l0_b_qnorm, l0_b_knorm, l1_norm, l1_w_mod, l1_b_mod, l1_w_in, l1_w_out, l1_c_qnorm, l1_c_knorm, l1_c_sink):
    pb, pl_, d = x_prompt.shape
    sb, sl, _ = x_sample.shape
    aw = A_HEADS * HEAD_DIM
    bkv = B_KV_HEADS * HEAD_DIM
    bw = B_KV_HEADS * GROUP * HEAD_DIM
    ckv = C_KV_HEADS * HEAD_DIM
    cw = C_KV_HEADS * GROUP * HEAD_DIM

    lb = jnp.cumsum(jax.nn.softmax(lb_gamma.astype(F32), axis=0), axis=0)[0]
    rope = _rope_tables(sl)

    nrow = -(-(sb + 1) // 8) * 8
    cond = jnp.zeros((nrow, d), F32).at[:sb].set(c).at[sb].set(c_ctx)
    xs = (x_prompt.reshape(pb * pl_, d), x_sample.reshape(sb * sl, d))
    tms = (_tile(pb * pl_, 1024), _tile(sl, 1024))
    rows_per_mod = (pb * pl_, sl)

    def mods(w_mod, b_mod):
        m = _adaln(cond, w_mod, b_mod)
        parts = [m[:, i * d:(i + 1) * d] for i in range(3)]
        return ([p[sb:sb + 1, None, :] for p in parts], [p[:sb, None, :] for p in parts])

    def tn_for(n):
        for t in (1536, 1280, 1024, 768, 512, 256, 128):
            if n % t == 0:
                return t
        return n

    mod_p, mod_s = mods(l0_w_mod, l0_b_mod)
    w_in = l0_w_in.astype(BF16)
    w_out = l0_w_out.astype(BF16)
    tn = tn_for(w_in.shape[1])
    u_p = _inproj(xs[0], l0_norm, mod_p[0], mod_p[1], w_in, rows_per_mod[0], tms[0], tn).reshape(pb, pl_, -1)
    u_s = _inproj(xs[1], l0_norm, mod_s[0], mod_s[1], w_in, rows_per_mod[1], tms[1], tn).reshape(sb, sl, -1)

    oa_p, new_state = _hgrn(u_p, lb, l0_a_onorm, None, hb=A_HEADS, emit_state=True)
    (oa_s,) = _hgrn(u_s, lb, l0_a_onorm, state_l0_hgrn, hb=4, emit_state=False)

    q_off = 5 * aw
    k_off = q_off + bw
    v_off = k_off + bkv
    g_off = v_off + bkv
    ob_p, k0, v0 = _attn(u_p, q_off, k_off, v_off, g_off, B_KV_HEADS, l0_b_qnorm, l0_b_knorm,
                         tq=128, nqb=2, emit_kv=True)
    (ob_s,) = _attn(u_s, q_off, k_off, v_off, g_off, B_KV_HEADS, l0_b_qnorm, l0_b_knorm,
                    tq=128, nqb=_tile(sl // 128, 8), ctx=(cache_l0_k, cache_l0_v), rope=rope)

    otm = (_tile(pb * pl_, 512), _tile(sl, 512))
    y_p = _outproj([oa_p.reshape(pb * pl_, aw), ob_p.reshape(pb * pl_, bw)], w_out, xs[0], mod_p[2],
                   rows_per_mod[0], otm[0])
    y_s = _outproj([oa_s.reshape(sb * sl, aw), ob_s.reshape(sb * sl, bw)], w_out, xs[1], mod_s[2],
                   rows_per_mod[1], otm[1])

    mod_p, mod_s = mods(l1_w_mod, l1_b_mod)
    w_in = l1_w_in.astype(BF16)
    w_out = l1_w_out.astype(BF16)
    tn = tn_for(w_in.shape[1])
    u_p = _inproj(y_p, l1_norm, mod_p[0], mod_p[1], w_in, rows_per_mod[0], tms[0], tn).reshape(pb, pl_, -1)
    u_s = _inproj(y_s, l1_norm, mod_s[0], mod_s[1], w_in, rows_per_mod[1], tms[1], tn).reshape(sb, sl, -1)

    k_off = cw
    v_off = k_off + ckv
    g_off = v_off + ckv
    oc_p, k1, v1 = _attn(u_p, 0, k_off, v_off, g_off, C_KV_HEADS, l1_c_qnorm, l1_c_knorm,
                         tq=128, nqb=2, sink=l1_c_sink, emit_kv=True)
    (oc_s,) = _attn(u_s, 0, k_off, v_off, g_off, C_KV_HEADS, l1_c_qnorm, l1_c_knorm,
                    tq=WINDOW, nqb=_tile(sl // WINDOW, 16), ctx=(cache_l1_k, cache_l1_v), rope=rope, window=True, sink=l1_c_sink)

    z_p = _outproj([oc_p.reshape(pb * pl_, cw)], w_out, y_p, mod_p[2], rows_per_mod[0], otm[0])
    z_s = _outproj([oc_s.reshape(sb * sl, cw)], w_out, y_s, mod_s[2], rows_per_mod[1], otm[1])

    return (z_p.reshape(pb, pl_, d), z_s.reshape(sb, sl, d), new_state,
            k0.reshape(pb, pl_, B_KV_HEADS, HEAD_DIM), v0.reshape(pb, pl_, B_KV_HEADS, HEAD_DIM),
            k1.reshape(pb, pl_, C_KV_HEADS, HEAD_DIM), v1.reshape(pb, pl_, C_KV_HEADS, HEAD_DIM))
```

```python
import functools
import math

import jax
import jax.numpy as jnp
import numpy as np
from jax import lax
from jax.experimental import pallas as pl
from jax.experimental.pallas import tpu as pltpu

F32 = jnp.float32
BF16 = jnp.bfloat16

HEAD_DIM = 128
GRID_W = 64
ROPE_QUARTER = HEAD_DIM // 4
ROPE_THETA = 10000.0
NORM_EPS = 1e-6
A_HEADS = 8
A_CHUNK = 32
B_KV_HEADS = 2
C_KV_HEADS = 4
GROUP = 4
WINDOW = 128
NEG = -1e30
LOG2E = math.log2(math.e)
MAX_UNSHIFTED_LOGIT = 60.0
LOGIT_BOUND_MARGIN = 1.1
VMEM_LIMIT = 56 * 1024 * 1024

ADALN_COLS = 1024
INPROJ_ROWS = 1024
INPROJ_COLS = (1536, 1280, 1024, 768, 512, 256, 128)
NORM_ROWS = 32
OUTPROJ_ROWS = 512
Q_ROWS = 128
Q_BLOCKS_PROMPT = 2
Q_BLOCKS_DENSE = 8
Q_BLOCKS_WINDOW = 16
KEY_PREP_ROWS = 256
HGRN_HEADS_SAMPLE = 4

_ROT_FIRST = (np.arange(HEAD_DIM) % (2 * ROPE_QUARTER)) < ROPE_QUARTER
_ROT_PARTNER = np.where(_ROT_FIRST, np.arange(HEAD_DIM) + ROPE_QUARTER, np.arange(HEAD_DIM) - ROPE_QUARTER)

_NT = (((1,), (1,)), ((), ()))
_TN = (((0,), (0,)), ((), ()))


def _params(*sem):
    return pltpu.CompilerParams(dimension_semantics=sem, vmem_limit_bytes=VMEM_LIMIT)


def _silu(x):
    return x * jax.nn.sigmoid(x)


def _rms(x, g):
    ms = jnp.mean(x * x, axis=-1, keepdims=True)
    return x * lax.rsqrt(ms + NORM_EPS) * g


def _adaln_kernel(c_ref, w_ref, b_ref, o_ref):
    a = _silu(c_ref[...]).astype(BF16)
    o_ref[...] = jnp.dot(a, w_ref[...].astype(BF16), preferred_element_type=F32) + b_ref[...]


def _adaln(cond, w_mod, b_mod):
    r, d = cond.shape
    n = w_mod.shape[1]
    tn = _tile(n, ADALN_COLS)
    return pl.pallas_call(
        _adaln_kernel,
        grid=(n // tn,),
        in_specs=[pl.BlockSpec((r, d), lambda j: (0, 0)),
                  pl.BlockSpec((d, tn), lambda j: (0, j)),
                  pl.BlockSpec((1, tn), lambda j: (0, j))],
        out_specs=pl.BlockSpec((r, tn), lambda j: (0, j)),
        out_shape=jax.ShapeDtypeStruct((r, n), F32),
        compiler_params=_params("parallel"),
        name="adaln",
    )(cond, w_mod, b_mod.reshape(1, n))


def _inproj_kernel(x_ref, g_ref, sh_ref, sc_ref, w_ref, o_ref, h_ref, *, rc, n_tiles, norm_steps):
    i = pl.program_id(0)
    j = pl.program_id(1)
    slice_rows = x_ref.shape[0] // norm_steps

    def matmul():
        o_ref[...] = jnp.dot(h_ref[(i + 1) % 2], w_ref[...], preferred_element_type=F32).astype(o_ref.dtype)

    def norm_slice():
        gain = g_ref[...] * (1.0 + sc_ref[...])
        sh = sh_ref[...]
        for r in range(slice_rows // rc):
            rows = pl.ds(pl.multiple_of(j * slice_rows + r * rc, rc), rc)
            h_ref[i % 2, rows, :] = (_rms(x_ref[rows, :], gain) + sh).astype(BF16)

    has_mm = i >= 1
    has_norm = (i < n_tiles) & (j < norm_steps)

    @pl.when(has_mm & has_norm)
    def _():
        matmul()
        norm_slice()

    @pl.when(has_mm & jnp.logical_not(has_norm))
    def _():
        matmul()

    @pl.when(jnp.logical_not(has_mm) & has_norm)
    def _():
        norm_slice()


def _inproj(x2d, norm_g, shift, scale, w, rows_per_mod, tm, tn):
    m, d = x2d.shape
    n = w.shape[1]
    n_tiles = m // tm
    n_cols = n // tn
    norm_steps = max(s for s in (1, 2, 4, 8) if s <= n_cols and (tm // s) % 32 == 0)

    def tile(i):
        return jnp.minimum(i, n_tiles - 1)

    def col(i, j):
        return jnp.where(i == 0, 0, j)

    mod_spec = pl.BlockSpec((None, 1, d), lambda i, j: ((tile(i) * tm) // rows_per_mod, 0, 0))
    return pl.pallas_call(
        functools.partial(_inproj_kernel, rc=NORM_ROWS, n_tiles=n_tiles, norm_steps=norm_steps),
        grid=(n_tiles + 1, n_cols),
        in_specs=[pl.BlockSpec((tm, d), lambda i, j: (tile(i), 0)),
                  pl.BlockSpec((1, d), lambda i, j: (0, 0)),
                  mod_spec, mod_spec,
                  pl.BlockSpec((d, tn), lambda i, j: (0, col(i, j)))],
        out_specs=pl.BlockSpec((tm, tn), lambda i, j: (jnp.maximum(i - 1, 0), col(i, j))),
        out_shape=jax.ShapeDtypeStruct((m, n), BF16),
        scratch_shapes=[pltpu.VMEM((2, tm, d), BF16)],
        compiler_params=_params("arbitrary", "arbitrary"),
        name="inproj",
    )(x2d, norm_g.reshape(1, d), shift, scale, w)


def _outproj_kernel(*refs, widths):
    o_refs = refs[:len(widths)]
    w_ref, x_ref, gt_ref, y_ref = refs[len(widths):]
    acc = None
    start = 0
    for o_ref, wd in zip(o_refs, widths):
        part = jnp.dot(o_ref[...], w_ref[start:start + wd, :], preferred_element_type=F32)
        acc = part if acc is None else acc + part
        start += wd
    y_ref[...] = x_ref[...] + gt_ref[...] * acc


def _outproj(os, w, x2d, gate, rows_per_mod, tm):
    m, d = x2d.shape
    widths = tuple(o.shape[1] for o in os)
    k = w.shape[0]
    return pl.pallas_call(
        functools.partial(_outproj_kernel, widths=widths),
        grid=(m // tm,),
        in_specs=[pl.BlockSpec((tm, wd), lambda i: (i, 0)) for wd in widths] + [
            pl.BlockSpec((k, d), lambda i: (0, 0)),
            pl.BlockSpec((tm, d), lambda i: (i, 0)),
            pl.BlockSpec((None, 1, d), lambda i: ((i * tm) // rows_per_mod, 0, 0))],
        out_specs=pl.BlockSpec((tm, d), lambda i: (i, 0)),
        out_shape=jax.ShapeDtypeStruct((m, d), F32),
        compiler_params=_params("parallel"),
        name="outproj",
    )(*os, w, x2d, gate)


SUBLANES = 8
HGRN_BLOCK = 128


def _chunk_cumprod(f, reverse):
    n = f.shape[0]
    tiles = n // SUBLANES
    per_chunk = A_CHUNK // SUBLANES
    x = f.reshape(tiles, SUBLANES, HEAD_DIM)
    sub = lax.broadcasted_iota(jnp.int32, x.shape, 1)
    s = 1
    while s < SUBLANES:
        if reverse:
            x = x * jnp.where(sub < SUBLANES - s, pltpu.roll(x, SUBLANES - s, axis=1), 1.0)
        else:
            x = x * jnp.where(sub >= s, pltpu.roll(x, s, axis=1), 1.0)
        s *= 2
    edge = 0 if reverse else SUBLANES - 1
    out, tot = [], []
    for c in range(n // A_CHUNK):
        ts = [x[c * per_chunk + i] for i in range(per_chunk)]
        order = range(per_chunk - 1, -1, -1) if reverse else range(per_chunk)
        carry = None
        done = {}
        for i in order:
            t = ts[i] if carry is None else ts[i] * carry
            done[i] = t
            carry = t[edge:edge + 1, :]
        out += [done[i] for i in range(per_chunk)]
        tot += [jnp.broadcast_to(carry, (A_CHUNK, HEAD_DIM))]
    return jnp.concatenate(out, axis=0), jnp.concatenate(tot, axis=0)


def _hgrn_kernel(*refs, seq, hb, has_s0, emit_state):
    q_ref, ff_ref, fb_ref, v_ref, g_ref, lb_ref, on_ref = refs[:7]
    pos = 7
    s0_ref = None
    if has_s0:
        s0_ref = refs[pos]
        pos += 1
    o_ref = refs[pos]
    pos += 1
    so_ref = None
    if emit_state:
        so_ref = refs[pos]
        pos += 1
    oacc_ref, st_ref, qd_ref, ke_ref, dec_ref = refs[pos:]

    nc = seq // A_CHUNK
    rb = HGRN_BLOCK
    cpb = rb // A_CHUNK
    ri = lax.broadcasted_iota(jnp.int32, (rb, rb), 0)
    ci = lax.broadcasted_iota(jnp.int32, (rb, rb), 1)
    shift = A_CHUNK.bit_length() - 1
    same_chunk = jnp.right_shift(ri, shift) == jnp.right_shift(ci, shift)
    masks = (same_chunk & (ri >= ci), same_chunk & (ci >= ri))

    for j in range(hb):
        for d in range(2):
            lanes = slice(d * HEAD_DIM, (d + 1) * HEAD_DIM)
            if has_s0:
                st_ref[j, :, lanes] = s0_ref[d, j].T
            else:
                st_ref[j, :, lanes] = jnp.zeros((HEAD_DIM, HEAD_DIM), F32)

    def stage1(blk, carry):
        rows = pl.ds(pl.multiple_of(blk * rb, rb), rb)
        heads = range(hb)
        chains = [(j, d) for j in heads for d in range(2)]
        cols = [slice(j * HEAD_DIM, (j + 1) * HEAD_DIM) for j in heads]
        qs = [_silu(q_ref[rows, cols[j]].astype(F32)) for j in heads]
        ts, fs, scans, qds, kinvs, atts = {}, {}, {}, {}, {}, {}
        for j, d in chains:
            fr = (fb_ref if d else ff_ref)[rows, cols[j]].astype(F32)
            lb = lb_ref[d:d + 1, cols[j]]
            ts[j, d] = (1.0 - lb) * jax.nn.sigmoid(fr)
            fs[j, d] = lb + ts[j, d]
        for j, d in chains:
            scans[j, d] = _chunk_cumprod(fs[j, d], reverse=bool(d))
        for j, d in chains:
            eb, dec = scans[j, d]
            lb = lb_ref[d:d + 1, cols[j]]
            qds[j, d] = (qs[j] * eb).astype(BF16)
            kinvs[j, d] = ((1.0 - lb) - ts[j, d]) * (1.0 / eb)
            qd_ref[d, j, rows, :] = qds[j, d]
            ke_ref[d, j, rows, :] = (kinvs[j, d] * dec).astype(BF16)
            for c in range(cpb):
                dec_ref[d, j, pl.ds(blk * cpb + c, 1), :] = dec[c * A_CHUNK:c * A_CHUNK + 1, :]
        for j, d in chains:
            att = lax.dot_general(qds[j, d], kinvs[j, d].astype(BF16), _NT, preferred_element_type=F32)
            atts[j, d] = jnp.where(masks[d], att, 0.0).astype(BF16)
        for j in heads:
            v = v_ref[rows, cols[j]]
            oacc_ref[rows, cols[j]] = (jnp.dot(atts[j, 0], v, preferred_element_type=F32)
                                       + jnp.dot(atts[j, 1], v, preferred_element_type=F32))
        return carry

    zero = jnp.zeros((A_CHUNK, HEAD_DIM), BF16)

    def blockdiag(a, b):
        return jnp.concatenate([jnp.concatenate([a, zero], axis=1), jnp.concatenate([zero, b], axis=1)], axis=0)

    def stage2(n, carry):
        cf, cb = n, nc - 1 - n
        rows_f = pl.ds(pl.multiple_of(cf * A_CHUNK, A_CHUNK), A_CHUNK)
        rows_b = pl.ds(pl.multiple_of(cb * A_CHUNK, A_CHUNK), A_CHUNK)
        for j in range(hb):
            cols = slice(j * HEAD_DIM, (j + 1) * HEAD_DIM)
            st = st_ref[j]
            qd = blockdiag(qd_ref[0, j, rows_f, :], qd_ref[1, j, rows_b, :])
            ke = blockdiag(ke_ref[0, j, rows_f, :], ke_ref[1, j, rows_b, :])
            v = jnp.concatenate([v_ref[rows_f, cols], v_ref[rows_b, cols]], axis=0)
            dec = jnp.concatenate([dec_ref[0, j, pl.ds(cf, 1), :], dec_ref[1, j, pl.ds(cb, 1), :]], axis=1)
            o = lax.dot_general(qd, st.astype(BF16), _NT, preferred_element_type=F32)
            st_ref[j] = st * dec + lax.dot_general(v, ke, _TN, preferred_element_type=F32)
            oacc_ref[rows_f, cols] += o[:A_CHUNK, :]
            oacc_ref[rows_b, cols] += o[A_CHUNK:, :]
        return carry

    nb = seq // rb
    mid = nb // 2
    on = on_ref[...]

    def merged(p, carry):
        stage1(p, carry)
        stage1(nb - 1 - p, carry)
        for i in range(cpb):
            stage2(p * cpb + i, carry)
        return carry

    def finish_block(r):
        rows = pl.ds(pl.multiple_of(r * rb, rb), rb)
        for j in range(hb):
            cols = slice(j * HEAD_DIM, (j + 1) * HEAD_DIM)
            y = _rms(oacc_ref[rows, cols], on)
            o_ref[rows, cols] = (y * _silu(g_ref[rows, cols].astype(F32))).astype(o_ref.dtype)

    def tail(g, carry):
        finish_block(mid - g)
        finish_block(mid + g - 1)
        for i in range(cpb):
            stage2(nc // 2 + g * cpb + i, carry)
        return carry

    lax.fori_loop(0, mid, merged, 0)
    for i in range(cpb):
        stage2(nc // 2 + i, 0)
    lax.fori_loop(1, mid, tail, 0)
    finish_block(0)
    finish_block(nb - 1)

    if emit_state:
        for j in range(hb):
            for d in range(2):
                so_ref[d, j] = st_ref[j, :, d * HEAD_DIM:(d + 1) * HEAD_DIM].T


def _hgrn(u3, lb, onorm, s0, hb, emit_state):
    bsz, seq, _ = u3.shape
    bw = hb * HEAD_DIM
    nh = A_HEADS // hb

    def seg(s):
        return pl.BlockSpec((None, seq, bw), lambda b, h: (b, 0, s * nh + h))

    in_specs = [seg(0), seg(1), seg(2), seg(3), seg(4),
                pl.BlockSpec((2, bw), lambda b, h: (0, h)),
                pl.BlockSpec((1, HEAD_DIM), lambda b, h: (0, 0))]
    args = [u3, u3, u3, u3, u3, lb, onorm.reshape(1, HEAD_DIM)]
    st_spec = pl.BlockSpec((None, 2, hb, HEAD_DIM, HEAD_DIM), lambda b, h: (b, 0, h, 0, 0))
    if s0 is not None:
        in_specs.append(st_spec)
        args.append(s0)
    out_specs = [pl.BlockSpec((None, seq, bw), lambda b, h: (b, 0, h))]
    out_shape = [jax.ShapeDtypeStruct((bsz, seq, A_HEADS * HEAD_DIM), BF16)]
    if emit_state:
        out_specs.append(st_spec)
        out_shape.append(jax.ShapeDtypeStruct((bsz, 2, A_HEADS, HEAD_DIM, HEAD_DIM), F32))
    return pl.pallas_call(
        functools.partial(_hgrn_kernel, seq=seq, hb=hb, has_s0=s0 is not None,
                          emit_state=emit_state),
        grid=(bsz, nh),
        in_specs=in_specs,
        out_specs=out_specs,
        out_shape=out_shape,
        scratch_shapes=[pltpu.VMEM((seq, bw), F32),
                        pltpu.VMEM((hb, HEAD_DIM, 2 * HEAD_DIM), F32),
                        pltpu.VMEM((2, hb, seq, HEAD_DIM), BF16),
                        pltpu.VMEM((2, hb, seq, HEAD_DIM), BF16),
                        pltpu.VMEM((2, hb, seq // A_CHUNK, HEAD_DIM), F32)],
        compiler_params=_params("parallel", "parallel"),
        name="hgrn",
    )(*args)


def _norm_rope(xb, g_cos, gp_sin, rot_ref, out_scale):
    x = xb.astype(F32)
    sq = (x * x).astype(BF16)
    y = x * g_cos
    if gp_sin is not None:
        both = jnp.dot(jnp.concatenate([xb, sq], axis=1), rot_ref[...], preferred_element_type=F32)
        y = y + both[:, :HEAD_DIM] * gp_sin
        ms = both[:, HEAD_DIM:]
    else:
        mean_mat = jnp.full((HEAD_DIM, HEAD_DIM), 1.0 / HEAD_DIM, BF16)
        ms = jnp.dot(sq, mean_mat, preferred_element_type=F32)
    r = lax.rsqrt(ms + NORM_EPS)
    if out_scale != 1.0:
        r = r * out_scale
    return y * r


def _attn_kernel(*refs, seq, tq, nqb, ctx_len, rope, window, sink, emit_kv, shift_free):
    refs = list(refs)
    q_ref, gate_ref, k_ref, v_ref = refs[:4]
    pos = 4
    kc_ref = vc_ref = cos_ref = sin_ref = rot_ref = bias_ref = sink_ref = ko_ref = vo_ref = None
    if ctx_len:
        kc_ref, vc_ref = refs[pos:pos + 2]
        pos += 2
    if rope:
        cos_ref, sin_ref, rot_ref = refs[pos:pos + 3]
        pos += 3
    if window:
        bias_ref = refs[pos]
        pos += 1
    qn_ref, kn_ref = refs[pos:pos + 2]
    pos += 2
    if sink:
        sink_ref = refs[pos]
        pos += 1
    o_ref = refs[pos]
    pos += 1
    if emit_kv:
        ko_ref, vo_ref = refs[pos:pos + 2]
        pos += 2
    ks_ref, vs_ref = refs[pos:pos + 2]
    pos += 2
    qc_ref = qs_ref = None
    if rope:
        qc_ref, qs_ref = refs[pos:]

    kvh = pl.program_id(1)
    step = pl.program_id(2)
    nq = seq // tq
    pad = WINDOW if window else 0
    ctx_off = seq + 2 * pad
    qscale = LOG2E / math.sqrt(HEAD_DIM)

    @pl.when(step == 0)
    def _():
        if window:
            zeros = jnp.zeros((pad, HEAD_DIM), BF16)
            for ref in (ks_ref, vs_ref):
                ref[0:pad, 0:HEAD_DIM] = zeros
                ref[pad + seq:ctx_off, 0:HEAD_DIM] = zeros
        rc = min(KEY_PREP_ROWS, seq)
        kg = kn_ref[0:1, :]
        kgp = kn_ref[1:2, :]
        ones = jnp.ones((rc, HEAD_DIM), BF16)

        def body(r, carry):
            rows = pl.ds(pl.multiple_of(r * rc, rc), rc)
            dst = pl.ds(pl.multiple_of(pad + r * rc, HEAD_DIM), rc)
            if rope:
                cos = cos_ref[rows, :]
                sin = sin_ref[rows, :]
                qc_ref[rows, :] = cos * qn_ref[0:1, :]
                qs_ref[rows, :] = sin * qn_ref[1:2, :]
                k = _norm_rope(k_ref[rows, :], cos * kg, sin * kgp, rot_ref, 1.0)
            else:
                k = _norm_rope(k_ref[rows, :], kg, None, None, 1.0)
            if emit_kv:
                ko_ref[rows, :] = k
                vo_ref[rows, :] = v_ref[rows, :].astype(F32)
            ks_ref[dst, :] = k.astype(BF16)
            vs_ref[dst, 0:HEAD_DIM] = v_ref[rows, :]
            return carry

        lax.fori_loop(0, seq // rc, body, 0)
        if ctx_len:
            ks_ref[ctx_off:ctx_off + ctx_len, :] = kc_ref[...].astype(BF16)
            vs_ref[ctx_off:ctx_off + ctx_len, 0:HEAD_DIM] = vc_ref[...].astype(BF16)

        def fill(r, carry):
            vs_ref[pl.ds(pl.multiple_of(r * rc, rc), rc), HEAD_DIM:2 * HEAD_DIM] = ones
            return carry

        lax.fori_loop(0, vs_ref.shape[0] // rc, fill, 0)

    sink_col = None
    if sink:
        sink_col = jnp.concatenate(
            [jnp.full((tq, 1), sink_ref[kvh * GROUP + h] * LOG2E, F32) for h in range(GROUP)], axis=0)

    def prep(t):
        qi = step * nqb + t
        qrows = pl.ds(pl.multiple_of(qi * tq, tq), tq)
        xs = []
        for h in range(GROUP):
            xb = q_ref[t * tq:(t + 1) * tq, h * HEAD_DIM:(h + 1) * HEAD_DIM]
            if rope:
                x = _norm_rope(xb, qc_ref[qrows, :], qs_ref[qrows, :], rot_ref, qscale)
            else:
                x = _norm_rope(xb, qn_ref[0:1, :], None, None, qscale)
            xs.append(x.astype(BF16))
        return jnp.concatenate(xs, axis=0)

    def key_parts(t):
        qi = step * nqb + t
        if not window:
            return [(slice(None), None)]
        start = pl.multiple_of(qi * tq, HEAD_DIM)
        edge = jnp.where(qi == 0, 0, jnp.where(qi == nq - 1, 2, 1))
        parts = [(pl.ds(start, 3 * WINDOW), bias_ref[edge])]
        if ctx_len:
            parts.append((slice(ctx_off, ctx_off + ctx_len), None))
        return parts

    def logits(q, parts):
        ss = []
        for rows, bias in parts:
            s = lax.dot_general(q, ks_ref[rows, :], _NT, preferred_element_type=F32)
            ss.append(s if bias is None else s + bias)
        return ss

    def row_max(ss):
        slabs = [s[:, c:c + HEAD_DIM] for s in ss for c in range(0, s.shape[1], HEAD_DIM)]
        return jnp.max(functools.reduce(jnp.maximum, slabs), axis=-1, keepdims=True)

    def weighted(ss, m, parts):
        acc = None
        for s, (rows, _) in zip(ss, parts):
            p = jnp.exp2(s if m is None else s - m)
            pv = jnp.dot(p.astype(BF16), vs_ref[rows, :], preferred_element_type=F32)
            acc = pv if acc is None else acc + pv
        return acc

    def finish(t, acc, m):
        l = acc[:, HEAD_DIM:]
        if sink:
            l = l + jnp.exp2(sink_col if m is None else sink_col - m)
        for h in range(GROUP):
            cols = slice(h * HEAD_DIM, (h + 1) * HEAD_DIM)
            rows = slice(h * tq, (h + 1) * tq)
            gate = gate_ref[t * tq:(t + 1) * tq, cols].astype(F32)
            inv = 1.0 / (l[rows, :] * (1.0 + jnp.exp(-gate)))
            o_ref[t * tq:(t + 1) * tq, cols] = (acc[rows, :HEAD_DIM] * gate * inv).astype(o_ref.dtype)

    blocks = range(nqb)
    parts = [key_parts(t) for t in blocks]
    ss, ms = {}, {}
    q_next = prep(0)
    for t in blocks:
        q = q_next
        if t + 1 < nqb:
            q_next = prep(t + 1)
        ss[t] = logits(q, parts[t])
        ms[t] = None if shift_free else row_max(ss[t])
        if t >= 1:
            finish(t - 1, weighted(ss.pop(t - 1), ms[t - 1], parts[t - 1]), ms.pop(t - 1))
    finish(nqb - 1, weighted(ss[nqb - 1], ms[nqb - 1], parts[nqb - 1]), ms[nqb - 1])


def _window_bias(tq):
    r = (jnp.arange(GROUP * tq) % tq)[:, None]
    c = jnp.arange(3 * WINDOW)[None, :]
    band = (c >= r) & (c <= r + 2 * WINDOW)
    keep = jnp.stack([band & (c >= WINDOW), band, band & (c < 2 * WINDOW)])
    return jnp.where(keep, 0.0, NEG).astype(F32)


def _attn(u3, q_off, k_off, v_off, g_off, kv_heads, qnorm, knorm, *, tq, nqb, ctx=None, rope=None,
          window=False, sink=None, emit_kv=False):
    bsz, seq, _ = u3.shape
    gw = GROUP * HEAD_DIM
    heads = kv_heads * GROUP
    ctx_len = 0 if ctx is None else ctx[0].shape[1]
    pad = WINDOW if window else 0
    assert not window or (tq == WINDOW and seq // tq >= 2)
    assert seq % (tq * nqb) == 0

    def wide(off):
        return pl.BlockSpec((None, tq * nqb, gw), lambda b, h, i: (b, i, off // gw + h))

    def narrow(off, rows):
        return pl.BlockSpec((None, rows, HEAD_DIM), lambda b, h, i: (b, 0, off // HEAD_DIM + h))

    def whole(shape):
        return pl.BlockSpec(shape, lambda b, h, i: (0,) * len(shape))

    in_specs = [wide(q_off), wide(g_off), narrow(k_off, seq), narrow(v_off, seq)]
    args = [u3, u3, u3, u3]
    if ctx is not None:
        for cache in ctx:
            in_specs.append(narrow(0, ctx_len))
            args.append(cache.reshape(bsz, ctx_len, kv_heads * HEAD_DIM))
    if rope is not None:
        for t in rope:
            in_specs.append(whole(t.shape))
            args.append(t)
    if window:
        in_specs.append(whole((3, GROUP * tq, 3 * WINDOW)))
        args.append(_window_bias(tq))
    in_specs += [whole((2, HEAD_DIM)), whole((2, HEAD_DIM))]
    args += [jnp.stack([qnorm, qnorm[_ROT_PARTNER]]), jnp.stack([knorm, knorm[_ROT_PARTNER]])]
    if sink is not None:
        in_specs.append(pl.BlockSpec(memory_space=pltpu.SMEM))
        args.append(sink)
    out_specs = [pl.BlockSpec((None, tq * nqb, gw), lambda b, h, i: (b, i, h))]
    out_shape = [jax.ShapeDtypeStruct((bsz, seq, heads * HEAD_DIM), BF16)]
    if emit_kv:
        for _ in range(2):
            out_specs.append(narrow(0, seq))
            out_shape.append(jax.ShapeDtypeStruct((bsz, seq, kv_heads * HEAD_DIM), F32))
    rows = seq + 2 * pad + ctx_len

    def call(shift_free, *operands):
        kern = functools.partial(_attn_kernel, seq=seq, tq=tq, nqb=nqb, ctx_len=ctx_len,
                                 rope=rope is not None, window=window, sink=sink is not None,
                                 emit_kv=emit_kv, shift_free=shift_free)
        return pl.pallas_call(
            kern,
            grid=(bsz, kv_heads, seq // (tq * nqb)),
            in_specs=in_specs,
            out_specs=out_specs,
            out_shape=out_shape,
            scratch_shapes=[pltpu.VMEM((rows, HEAD_DIM), BF16),
                            pltpu.VMEM((rows, 2 * HEAD_DIM), BF16)]
            + ([pltpu.VMEM((seq, HEAD_DIM), F32)] * 2 if rope is not None else []),
            compiler_params=_params("parallel", "parallel", "arbitrary"),
            name="attn",
        )(*operands)

    k_norm = math.sqrt(HEAD_DIM) * jnp.max(jnp.abs(knorm))
    if ctx is not None:
        k_norm = jnp.maximum(k_norm, jnp.sqrt(jnp.max(jnp.sum(jnp.square(ctx[0]), axis=-1))))
    bound = jnp.max(jnp.abs(qnorm)) * LOG2E * k_norm * LOGIT_BOUND_MARGIN
    return lax.cond(bound <= MAX_UNSHIFTED_LOGIT,
                    functools.partial(call, True), functools.partial(call, False), *args)


def _rope_tables(n_tokens):
    rows = n_tokens // GRID_W
    row = jnp.repeat(jnp.arange(rows), GRID_W).astype(F32)
    col = (jnp.arange(rows * GRID_W) % GRID_W).astype(F32)
    inv = ROPE_THETA ** (-jnp.arange(ROPE_QUARTER, dtype=F32) / ROPE_QUARTER)
    ar = row[:, None] * inv
    ac = col[:, None] * inv
    ang = jnp.concatenate([ar, ar, ac, ac], axis=-1)
    rot = np.zeros((2 * HEAD_DIM, 2 * HEAD_DIM), np.float32)
    rot[_ROT_PARTNER, np.arange(HEAD_DIM)] = np.where(_ROT_FIRST, -1.0, 1.0)
    rot[HEAD_DIM:, HEAD_DIM:] = 1.0 / HEAD_DIM
    return jnp.cos(ang), jnp.sin(ang), jnp.asarray(rot, BF16)


def _tile(m, pref):
    t = pref
    while m % t:
        t //= 2
    return t


def kernel(x_prompt, x_sample, state_l0_hgrn, cache_l0_k, cache_l0_v, cache_l1_k, cache_l1_v, c, c_ctx, lb_gamma, l0_norm, l0_w_mod, l0_b_mod, l0_w_in, l0_w_out, l0_a_onorm, l0_b_qnorm, l0_b_knorm, l1_norm, l1_w_mod, l1_b_mod, l1_w_in, l1_w_out, l1_c_qnorm, l1_c_knorm, l1_c_sink):
    pb, pl_, d = x_prompt.shape
    sb, sl, _ = x_sample.shape
    aw = A_HEADS * HEAD_DIM
    bkv = B_KV_HEADS * HEAD_DIM
    bw = B_KV_HEADS * GROUP * HEAD_DIM
    ckv = C_KV_HEADS * HEAD_DIM
    cw = C_KV_HEADS * GROUP * HEAD_DIM

    lb = jnp.cumsum(jax.nn.softmax(lb_gamma.astype(F32), axis=0), axis=0)[0]
    rope = _rope_tables(sl)

    nrow = -(-(sb + 1) // 8) * 8
    cond = jnp.zeros((nrow, d), F32).at[:sb].set(c).at[sb].set(c_ctx)
    xs = (x_prompt.reshape(pb * pl_, d), x_sample.reshape(sb * sl, d))
    tms = (_tile(pb * pl_, INPROJ_ROWS), _tile(sl, INPROJ_ROWS))
    rows_per_mod = (pb * pl_, sl)

    def mods(w_mod, b_mod):
        m = _adaln(cond, w_mod, b_mod)
        parts = [m[:, i * d:(i + 1) * d] for i in range(3)]
        return ([p[sb:sb + 1, None, :] for p in parts], [p[:sb, None, :] for p in parts])

    def tn_for(n):
        for t in INPROJ_COLS:
            if n % t == 0:
                return t
        return n

    mod_p, mod_s = mods(l0_w_mod, l0_b_mod)
    w_in = l0_w_in.astype(BF16)
    w_out = l0_w_out.astype(BF16)
    tn = tn_for(w_in.shape[1])
    u_p = _inproj(xs[0], l0_norm, mod_p[0], mod_p[1], w_in, rows_per_mod[0], tms[0], tn).reshape(pb, pl_, -1)
    u_s = _inproj(xs[1], l0_norm, mod_s[0], mod_s[1], w_in, rows_per_mod[1], tms[1], tn).reshape(sb, sl, -1)

    oa_p, new_state = _hgrn(u_p, lb, l0_a_onorm, None, hb=A_HEADS, emit_state=True)
    (oa_s,) = _hgrn(u_s, lb, l0_a_onorm, state_l0_hgrn, hb=HGRN_HEADS_SAMPLE, emit_state=False)

    q_off = 5 * aw
    k_off = q_off + bw
    v_off = k_off + bkv
    g_off = v_off + bkv
    ob_p, k0, v0 = _attn(u_p, q_off, k_off, v_off, g_off, B_KV_HEADS, l0_b_qnorm, l0_b_knorm,
                         tq=Q_ROWS, nqb=Q_BLOCKS_PROMPT, emit_kv=True)
    (ob_s,) = _attn(u_s, q_off, k_off, v_off, g_off, B_KV_HEADS, l0_b_qnorm, l0_b_knorm,
                    tq=Q_ROWS, nqb=_tile(sl // Q_ROWS, Q_BLOCKS_DENSE), ctx=(cache_l0_k, cache_l0_v), rope=rope)

    otm = (_tile(pb * pl_, OUTPROJ_ROWS), _tile(sl, OUTPROJ_ROWS))
    y_p = _outproj([oa_p.reshape(pb * pl_, aw), ob_p.reshape(pb * pl_, bw)], w_out, xs[0], mod_p[2],
                   rows_per_mod[0], otm[0])
    y_s = _outproj([oa_s.reshape(sb * sl, aw), ob_s.reshape(sb * sl, bw)], w_out, xs[1], mod_s[2],
                   rows_per_mod[1], otm[1])

    mod_p, mod_s = mods(l1_w_mod, l1_b_mod)
    w_in = l1_w_in.astype(BF16)
    w_out = l1_w_out.astype(BF16)
    tn = tn_for(w_in.shape[1])
    u_p = _inproj(y_p, l1_norm, mod_p[0], mod_p[1], w_in, rows_per_mod[0], tms[0], tn).reshape(pb, pl_, -1)
    u_s = _inproj(y_s, l1_norm, mod_s[0], mod_s[1], w_in, rows_per_mod[1], tms[1], tn).reshape(sb, sl, -1)

    k_off = cw
    v_off = k_off + ckv
    g_off = v_off + ckv
    oc_p, k1, v1 = _attn(u_p, 0, k_off, v_off, g_off, C_KV_HEADS, l1_c_qnorm, l1_c_knorm,
                         tq=Q_ROWS, nqb=Q_BLOCKS_PROMPT, sink=l1_c_sink, emit_kv=True)
    (oc_s,) = _attn(u_s, 0, k_off, v_off, g_off, C_KV_HEADS, l1_c_qnorm, l1_c_knorm,
                    tq=WINDOW, nqb=_tile(sl // WINDOW, Q_BLOCKS_WINDOW), ctx=(cache_l1_k, cache_l1_v), rope=rope,
                    window=True, sink=l1_c_sink)

    z_p = _outproj([oc_p.reshape(pb * pl_, cw)], w_out, y_p, mod_p[2], rows_per_mod[0], otm[0])
    z_s = _outproj([oc_s.reshape(sb * sl, cw)], w_out, y_s, mod_s[2], rows_per_mod[1], otm[1])

    return (z_p.reshape(pb, pl_, d), z_s.reshape(sb, sl, d), new_state,
            k0.reshape(pb, pl_, B_KV_HEADS, HEAD_DIM), v0.reshape(pb, pl_, B_KV_HEADS, HEAD_DIM),
            k1.reshape(pb, pl_, C_KV_HEADS, HEAD_DIM), v1.reshape(pb, pl_, C_KV_HEADS, HEAD_DIM))
```

```python
import functools
import math

import jax
import jax.numpy as jnp
import numpy as np
from jax import lax
from jax.experimental import pallas as pl
from jax.experimental.pallas import tpu as pltpu

F32 = jnp.float32
BF16 = jnp.bfloat16

HEAD_DIM = 128
GRID_W = 64
ROPE_QUARTER = HEAD_DIM // 4
ROPE_THETA = 10000.0
NORM_EPS = 1e-6
A_HEADS = 8
A_CHUNK = 32
B_KV_HEADS = 2
C_KV_HEADS = 4
GROUP = 4
WINDOW = 128
NEG = -1e30
LOG2E = math.log2(math.e)
MAX_UNSHIFTED_LOGIT = 60.0
LOGIT_BOUND_MARGIN = 1.1
VMEM_LIMIT = 56 * 1024 * 1024

ADALN_COLS = 1024
INPROJ_ROWS = 1024
INPROJ_COLS = (1536, 1280, 1024, 768, 512, 256, 128)
NORM_ROWS = 32
OUTPROJ_ROWS = 512
Q_ROWS = 128
Q_BLOCKS_PROMPT = 2
Q_BLOCKS_DENSE = 16
Q_BLOCKS_WINDOW = 16
KEY_PREP_ROWS = 256
HGRN_HEADS_SAMPLE = 4

_ROT_FIRST = (np.arange(HEAD_DIM) % (2 * ROPE_QUARTER)) < ROPE_QUARTER
_ROT_PARTNER = np.where(_ROT_FIRST, np.arange(HEAD_DIM) + ROPE_QUARTER, np.arange(HEAD_DIM) - ROPE_QUARTER)

_NT = (((1,), (1,)), ((), ()))
_TN = (((0,), (0,)), ((), ()))


def _params(*sem):
    return pltpu.CompilerParams(dimension_semantics=sem, vmem_limit_bytes=VMEM_LIMIT)


def _silu(x):
    return x * jax.nn.sigmoid(x)


def _rms(x, g):
    ms = jnp.mean(x * x, axis=-1, keepdims=True)
    return x * lax.rsqrt(ms + NORM_EPS) * g


def _adaln_kernel(c_ref, w_ref, b_ref, o_ref):
    a = _silu(c_ref[...]).astype(BF16)
    o_ref[...] = jnp.dot(a, w_ref[...].astype(BF16), preferred_element_type=F32) + b_ref[...]


def _adaln(cond, w_mod, b_mod):
    r, d = cond.shape
    n = w_mod.shape[1]
    tn = _tile(n, ADALN_COLS)
    return pl.pallas_call(
        _adaln_kernel,
        grid=(n // tn,),
        in_specs=[pl.BlockSpec((r, d), lambda j: (0, 0)),
                  pl.BlockSpec((d, tn), lambda j: (0, j)),
                  pl.BlockSpec((1, tn), lambda j: (0, j))],
        out_specs=pl.BlockSpec((r, tn), lambda j: (0, j)),
        out_shape=jax.ShapeDtypeStruct((r, n), F32),
        compiler_params=_params("parallel"),
        name="adaln",
    )(cond, w_mod, b_mod.reshape(1, n))


def _inproj_kernel(x_ref, g_ref, sh_ref, sc_ref, w_ref, o_ref, h_ref, *, rc, n_tiles, norm_steps):
    i = pl.program_id(0)
    j = pl.program_id(1)
    slice_rows = x_ref.shape[0] // norm_steps

    def matmul():
        o_ref[...] = jnp.dot(h_ref[(i + 1) % 2], w_ref[...], preferred_element_type=F32).astype(o_ref.dtype)

    def norm_slice():
        gain = g_ref[...] * (1.0 + sc_ref[...])
        sh = sh_ref[...]
        for r in range(slice_rows // rc):
            rows = pl.ds(pl.multiple_of(j * slice_rows + r * rc, rc), rc)
            h_ref[i % 2, rows, :] = (_rms(x_ref[rows, :], gain) + sh).astype(BF16)

    has_mm = i >= 1
    has_norm = (i < n_tiles) & (j < norm_steps)

    @pl.when(has_mm & has_norm)
    def _():
        matmul()
        norm_slice()

    @pl.when(has_mm & jnp.logical_not(has_norm))
    def _():
        matmul()

    @pl.when(jnp.logical_not(has_mm) & has_norm)
    def _():
        norm_slice()


def _inproj(x2d, norm_g, shift, scale, w, rows_per_mod, tm, tn):
    m, d = x2d.shape
    n = w.shape[1]
    n_tiles = m // tm
    n_cols = n // tn
    norm_steps = max(s for s in (1, 2, 4, 8) if s <= n_cols and (tm // s) % 32 == 0)

    def tile(i):
        return jnp.minimum(i, n_tiles - 1)

    def col(i, j):
        return jnp.where(i == 0, 0, j)

    mod_spec = pl.BlockSpec((None, 1, d), lambda i, j: ((tile(i) * tm) // rows_per_mod, 0, 0))
    return pl.pallas_call(
        functools.partial(_inproj_kernel, rc=NORM_ROWS, n_tiles=n_tiles, norm_steps=norm_steps),
        grid=(n_tiles + 1, n_cols),
        in_specs=[pl.BlockSpec((tm, d), lambda i, j: (tile(i), 0)),
                  pl.BlockSpec((1, d), lambda i, j: (0, 0)),
                  mod_spec, mod_spec,
                  pl.BlockSpec((d, tn), lambda i, j: (0, col(i, j)))],
        out_specs=pl.BlockSpec((tm, tn), lambda i, j: (jnp.maximum(i - 1, 0), col(i, j))),
        out_shape=jax.ShapeDtypeStruct((m, n), BF16),
        scratch_shapes=[pltpu.VMEM((2, tm, d), BF16)],
        compiler_params=_params("arbitrary", "arbitrary"),
        name="inproj",
    )(x2d, norm_g.reshape(1, d), shift, scale, w)


def _outproj_kernel(*refs, widths):
    o_refs = refs[:len(widths)]
    w_ref, x_ref, gt_ref, y_ref = refs[len(widths):]
    acc = None
    start = 0
    for o_ref, wd in zip(o_refs, widths):
        part = jnp.dot(o_ref[...], w_ref[start:start + wd, :], preferred_element_type=F32)
        acc = part if acc is None else acc + part
        start += wd
    y_ref[...] = x_ref[...] + gt_ref[...] * acc


def _outproj(os, w, x2d, gate, rows_per_mod, tm):
    m, d = x2d.shape
    widths = tuple(o.shape[1] for o in os)
    k = w.shape[0]
    return pl.pallas_call(
        functools.partial(_outproj_kernel, widths=widths),
        grid=(m // tm,),
        in_specs=[pl.BlockSpec((tm, wd), lambda i: (i, 0)) for wd in widths] + [
            pl.BlockSpec((k, d), lambda i: (0, 0)),
            pl.BlockSpec((tm, d), lambda i: (i, 0)),
            pl.BlockSpec((None, 1, d), lambda i: ((i * tm) // rows_per_mod, 0, 0))],
        out_specs=pl.BlockSpec((tm, d), lambda i: (i, 0)),
        out_shape=jax.ShapeDtypeStruct((m, d), F32),
        compiler_params=_params("parallel"),
        name="outproj",
    )(*os, w, x2d, gate)


SUBLANES = 8
HGRN_BLOCK = 128


def _chunk_cumprod(f, reverse):
    n = f.shape[0]
    tiles = n // SUBLANES
    per_chunk = A_CHUNK // SUBLANES
    x = f.reshape(tiles, SUBLANES, HEAD_DIM)
    sub = lax.broadcasted_iota(jnp.int32, x.shape, 1)
    s = 1
    while s < SUBLANES:
        if reverse:
            x = x * jnp.where(sub < SUBLANES - s, pltpu.roll(x, SUBLANES - s, axis=1), 1.0)
        else:
            x = x * jnp.where(sub >= s, pltpu.roll(x, s, axis=1), 1.0)
        s *= 2
    edge = 0 if reverse else SUBLANES - 1
    out, tot = [], []
    for c in range(n // A_CHUNK):
        ts = [x[c * per_chunk + i] for i in range(per_chunk)]
        order = range(per_chunk - 1, -1, -1) if reverse else range(per_chunk)
        carry = None
        done = {}
        for i in order:
            t = ts[i] if carry is None else ts[i] * carry
            done[i] = t
            carry = t[edge:edge + 1, :]
        out += [done[i] for i in range(per_chunk)]
        tot += [jnp.broadcast_to(carry, (A_CHUNK, HEAD_DIM))]
    return jnp.concatenate(out, axis=0), jnp.concatenate(tot, axis=0)


def _hgrn_kernel(*refs, seq, hb, has_s0, emit_state):
    q_ref, ff_ref, fb_ref, v_ref, g_ref, lb_ref, on_ref = refs[:7]
    pos = 7
    s0_ref = None
    if has_s0:
        s0_ref = refs[pos]
        pos += 1
    o_ref = refs[pos]
    pos += 1
    so_ref = None
    if emit_state:
        so_ref = refs[pos]
        pos += 1
    oacc_ref, st_ref, qd_ref, ke_ref, dec_ref = refs[pos:]

    nc = seq // A_CHUNK
    rb = HGRN_BLOCK
    cpb = rb // A_CHUNK
    ri = lax.broadcasted_iota(jnp.int32, (rb, rb), 0)
    ci = lax.broadcasted_iota(jnp.int32, (rb, rb), 1)
    shift = A_CHUNK.bit_length() - 1
    same_chunk = jnp.right_shift(ri, shift) == jnp.right_shift(ci, shift)
    masks = (same_chunk & (ri >= ci), same_chunk & (ci >= ri))

    for j in range(hb):
        for d in range(2):
            lanes = slice(d * HEAD_DIM, (d + 1) * HEAD_DIM)
            if has_s0:
                st_ref[j, :, lanes] = s0_ref[d, j].T
            else:
                st_ref[j, :, lanes] = jnp.zeros((HEAD_DIM, HEAD_DIM), F32)

    def stage1(blk, carry):
        rows = pl.ds(pl.multiple_of(blk * rb, rb), rb)
        heads = range(hb)
        chains = [(j, d) for j in heads for d in range(2)]
        cols = [slice(j * HEAD_DIM, (j + 1) * HEAD_DIM) for j in heads]
        qs = [_silu(q_ref[rows, cols[j]].astype(F32)) for j in heads]
        ts, fs, scans, qds, kinvs, atts = {}, {}, {}, {}, {}, {}
        for j, d in chains:
            fr = (fb_ref if d else ff_ref)[rows, cols[j]].astype(F32)
            lb = lb_ref[d:d + 1, cols[j]]
            ts[j, d] = (1.0 - lb) * jax.nn.sigmoid(fr)
            fs[j, d] = lb + ts[j, d]
        for j, d in chains:
            scans[j, d] = _chunk_cumprod(fs[j, d], reverse=bool(d))
        for j, d in chains:
            eb, dec = scans[j, d]
            lb = lb_ref[d:d + 1, cols[j]]
            qds[j, d] = (qs[j] * eb).astype(BF16)
            kinvs[j, d] = ((1.0 - lb) - ts[j, d]) * (1.0 / eb)
            qd_ref[d, j, rows, :] = qds[j, d]
            ke_ref[d, j, rows, :] = (kinvs[j, d] * dec).astype(BF16)
            for c in range(cpb):
                dec_ref[d, j, pl.ds(blk * cpb + c, 1), :] = dec[c * A_CHUNK:c * A_CHUNK + 1, :]
        for j, d in chains:
            att = lax.dot_general(qds[j, d], kinvs[j, d].astype(BF16), _NT, preferred_element_type=F32)
            atts[j, d] = jnp.where(masks[d], att, 0.0).astype(BF16)
        for j in heads:
            v = v_ref[rows, cols[j]]
            oacc_ref[rows, cols[j]] = (jnp.dot(atts[j, 0], v, preferred_element_type=F32)
                                       + jnp.dot(atts[j, 1], v, preferred_element_type=F32))
        return carry

    zero = jnp.zeros((A_CHUNK, HEAD_DIM), BF16)

    def blockdiag(a, b):
        return jnp.concatenate([jnp.concatenate([a, zero], axis=1), jnp.concatenate([zero, b], axis=1)], axis=0)

    def stage2(n, carry):
        cf, cb = n, nc - 1 - n
        rows_f = pl.ds(pl.multiple_of(cf * A_CHUNK, A_CHUNK), A_CHUNK)
        rows_b = pl.ds(pl.multiple_of(cb * A_CHUNK, A_CHUNK), A_CHUNK)
        for j in range(hb):
            cols = slice(j * HEAD_DIM, (j + 1) * HEAD_DIM)
            st = st_ref[j]
            qd = blockdiag(qd_ref[0, j, rows_f, :], qd_ref[1, j, rows_b, :])
            ke = blockdiag(ke_ref[0, j, rows_f, :], ke_ref[1, j, rows_b, :])
            v = jnp.concatenate([v_ref[rows_f, cols], v_ref[rows_b, cols]], axis=0)
            dec = jnp.concatenate([dec_ref[0, j, pl.ds(cf, 1), :], dec_ref[1, j, pl.ds(cb, 1), :]], axis=1)
            o = lax.dot_general(qd, st.astype(BF16), _NT, preferred_element_type=F32)
            st_ref[j] = st * dec + lax.dot_general(v, ke, _TN, preferred_element_type=F32)
            oacc_ref[rows_f, cols] += o[:A_CHUNK, :]
            oacc_ref[rows_b, cols] += o[A_CHUNK:, :]
        return carry

    nb = seq // rb
    mid = nb // 2
    on = on_ref[...]

    def merged(p, carry):
        stage1(p, carry)
        stage1(nb - 1 - p, carry)
        for i in range(cpb):
            stage2(p * cpb + i, carry)
        return carry

    def finish_block(r):
        rows = pl.ds(pl.multiple_of(r * rb, rb), rb)
        for j in range(hb):
            cols = slice(j * HEAD_DIM, (j + 1) * HEAD_DIM)
            y = _rms(oacc_ref[rows, cols], on)
            o_ref[rows, cols] = (y * _silu(g_ref[rows, cols].astype(F32))).astype(o_ref.dtype)

    def tail(g, carry):
        finish_block(mid - g)
        finish_block(mid + g - 1)
        for i in range(cpb):
            stage2(nc // 2 + g * cpb + i, carry)
        return carry

    lax.fori_loop(0, mid, merged, 0)
    for i in range(cpb):
        stage2(nc // 2 + i, 0)
    lax.fori_loop(1, mid, tail, 0)
    finish_block(0)
    finish_block(nb - 1)

    if emit_state:
        for j in range(hb):
            for d in range(2):
                so_ref[d, j] = st_ref[j, :, d * HEAD_DIM:(d + 1) * HEAD_DIM].T


def _hgrn(u3, lb, onorm, s0, hb, emit_state):
    bsz, seq, _ = u3.shape
    bw = hb * HEAD_DIM
    nh = A_HEADS // hb

    def seg(s):
        return pl.BlockSpec((None, seq, bw), lambda b, h: (b, 0, s * nh + h))

    in_specs = [seg(0), seg(1), seg(2), seg(3), seg(4),
                pl.BlockSpec((2, bw), lambda b, h: (0, h)),
                pl.BlockSpec((1, HEAD_DIM), lambda b, h: (0, 0))]
    args = [u3, u3, u3, u3, u3, lb, onorm.reshape(1, HEAD_DIM)]
    st_spec = pl.BlockSpec((None, 2, hb, HEAD_DIM, HEAD_DIM), lambda b, h: (b, 0, h, 0, 0))
    if s0 is not None:
        in_specs.append(st_spec)
        args.append(s0)
    out_specs = [pl.BlockSpec((None, seq, bw), lambda b, h: (b, 0, h))]
    out_shape = [jax.ShapeDtypeStruct((bsz, seq, A_HEADS * HEAD_DIM), BF16)]
    if emit_state:
        out_specs.append(st_spec)
        out_shape.append(jax.ShapeDtypeStruct((bsz, 2, A_HEADS, HEAD_DIM, HEAD_DIM), F32))
    return pl.pallas_call(
        functools.partial(_hgrn_kernel, seq=seq, hb=hb, has_s0=s0 is not None,
                          emit_state=emit_state),
        grid=(bsz, nh),
        in_specs=in_specs,
        out_specs=out_specs,
        out_shape=out_shape,
        scratch_shapes=[pltpu.VMEM((seq, bw), F32),
                        pltpu.VMEM((hb, HEAD_DIM, 2 * HEAD_DIM), F32),
                        pltpu.VMEM((2, hb, seq, HEAD_DIM), BF16),
                        pltpu.VMEM((2, hb, seq, HEAD_DIM), BF16),
                        pltpu.VMEM((2, hb, seq // A_CHUNK, HEAD_DIM), F32)],
        compiler_params=_params("parallel", "parallel"),
        name="hgrn",
    )(*args)


def _norm_rope(xb, g_cos, gp_sin, rot_ref, out_scale):
    x = xb.astype(F32)
    sq = (x * x).astype(BF16)
    y = x * g_cos
    if gp_sin is not None:
        both = jnp.dot(jnp.concatenate([xb, sq], axis=1), rot_ref[...], preferred_element_type=F32)
        y = y + both[:, :HEAD_DIM] * gp_sin
        ms = both[:, HEAD_DIM:]
    else:
        mean_mat = jnp.full((HEAD_DIM, HEAD_DIM), 1.0 / HEAD_DIM, BF16)
        ms = jnp.dot(sq, mean_mat, preferred_element_type=F32)
    r = lax.rsqrt(ms + NORM_EPS)
    if out_scale != 1.0:
        r = r * out_scale
    return y * r


def _attn_kernel(*refs, seq, tq, nqb, ctx_len, rope, window, sink, emit_kv, shift_free):
    refs = list(refs)
    q_ref, gate_ref, k_ref, v_ref = refs[:4]
    pos = 4
    kc_ref = vc_ref = cos_ref = sin_ref = rot_ref = bias_ref = sink_ref = ko_ref = vo_ref = None
    if ctx_len:
        kc_ref, vc_ref = refs[pos:pos + 2]
        pos += 2
    if rope:
        cos_ref, sin_ref, rot_ref = refs[pos:pos + 3]
        pos += 3
    if window:
        bias_ref = refs[pos]
        pos += 1
    qn_ref, kn_ref = refs[pos:pos + 2]
    pos += 2
    if sink:
        sink_ref = refs[pos]
        pos += 1
    o_ref = refs[pos]
    pos += 1
    if emit_kv:
        ko_ref, vo_ref = refs[pos:pos + 2]
        pos += 2
    ks_ref, vs_ref = refs[pos:pos + 2]
    pos += 2
    qc_ref = qs_ref = None
    if rope:
        qc_ref, qs_ref = refs[pos:]

    kvh = pl.program_id(1)
    step = pl.program_id(2)
    nq = seq // tq
    pad = WINDOW if window else 0
    ctx_off = seq + 2 * pad
    qscale = LOG2E / math.sqrt(HEAD_DIM)

    @pl.when(step == 0)
    def _():
        if window:
            zeros = jnp.zeros((pad, HEAD_DIM), BF16)
            for ref in (ks_ref, vs_ref):
                ref[0:pad, 0:HEAD_DIM] = zeros
                ref[pad + seq:ctx_off, 0:HEAD_DIM] = zeros
        rc = min(KEY_PREP_ROWS, seq)
        kg = kn_ref[0:1, :]
        kgp = kn_ref[1:2, :]
        ones = jnp.ones((rc, HEAD_DIM), BF16)

        def body(r, carry):
            rows = pl.ds(pl.multiple_of(r * rc, rc), rc)
            dst = pl.ds(pl.multiple_of(pad + r * rc, HEAD_DIM), rc)
            if rope:
                cos = cos_ref[rows, :]
                sin = sin_ref[rows, :]
                qc_ref[rows, :] = cos * qn_ref[0:1, :]
                qs_ref[rows, :] = sin * qn_ref[1:2, :]
                k = _norm_rope(k_ref[rows, :], cos * kg, sin * kgp, rot_ref, 1.0)
            else:
                k = _norm_rope(k_ref[rows, :], kg, None, None, 1.0)
            if emit_kv:
                ko_ref[rows, :] = k
                vo_ref[rows, :] = v_ref[rows, :].astype(F32)
            ks_ref[dst, :] = k.astype(BF16)
            vs_ref[dst, 0:HEAD_DIM] = v_ref[rows, :]
            return carry

        lax.fori_loop(0, seq // rc, body, 0)
        if ctx_len:
            ks_ref[ctx_off:ctx_off + ctx_len, :] = kc_ref[...].astype(BF16)
            vs_ref[ctx_off:ctx_off + ctx_len, 0:HEAD_DIM] = vc_ref[...].astype(BF16)

        def fill(r, carry):
            vs_ref[pl.ds(pl.multiple_of(r * rc, rc), rc), HEAD_DIM:2 * HEAD_DIM] = ones
            return carry

        lax.fori_loop(0, vs_ref.shape[0] // rc, fill, 0)

    sink_col = None
    if sink:
        sink_col = jnp.concatenate(
            [jnp.full((tq, 1), sink_ref[kvh * GROUP + h] * LOG2E, F32) for h in range(GROUP)], axis=0)

    def prep(t):
        qi = step * nqb + t
        qrows = pl.ds(pl.multiple_of(qi * tq, tq), tq)
        xs = []
        for h in range(GROUP):
            xb = q_ref[t * tq:(t + 1) * tq, h * HEAD_DIM:(h + 1) * HEAD_DIM]
            if rope:
                x = _norm_rope(xb, qc_ref[qrows, :], qs_ref[qrows, :], rot_ref, qscale)
            else:
                x = _norm_rope(xb, qn_ref[0:1, :], None, None, qscale)
            xs.append(x.astype(BF16))
        return jnp.concatenate(xs, axis=0)

    def key_parts(t):
        qi = step * nqb + t
        if not window:
            return [(slice(None), None)]
        start = pl.multiple_of(qi * tq, HEAD_DIM)
        edge = jnp.where(qi == 0, 0, jnp.where(qi == nq - 1, 2, 1))
        parts = [(pl.ds(start, 3 * WINDOW), bias_ref[edge])]
        if ctx_len:
            parts.append((slice(ctx_off, ctx_off + ctx_len), None))
        return parts

    def logits(q, parts):
        ss = []
        for rows, bias in parts:
            s = lax.dot_general(q, ks_ref[rows, :], _NT, preferred_element_type=F32)
            ss.append(s if bias is None else s + bias)
        return ss

    def row_max(ss):
        slabs = [s[:, c:c + HEAD_DIM] for s in ss for c in range(0, s.shape[1], HEAD_DIM)]
        return jnp.max(functools.reduce(jnp.maximum, slabs), axis=-1, keepdims=True)

    def weighted(ss, m, parts):
        acc = None
        for s, (rows, _) in zip(ss, parts):
            p = jnp.exp2(s if m is None else s - m)
            pv = jnp.dot(p.astype(BF16), vs_ref[rows, :], preferred_element_type=F32)
            acc = pv if acc is None else acc + pv
        return acc

    def finish(t, acc, m):
        l = acc[:, HEAD_DIM:]
        if sink:
            l = l + jnp.exp2(sink_col if m is None else sink_col - m)
        for h in range(GROUP):
            cols = slice(h * HEAD_DIM, (h + 1) * HEAD_DIM)
            rows = slice(h * tq, (h + 1) * tq)
            gate = gate_ref[t * tq:(t + 1) * tq, cols].astype(F32)
            inv = 1.0 / (l[rows, :] * (1.0 + jnp.exp(-gate)))
            o_ref[t * tq:(t + 1) * tq, cols] = (acc[rows, :HEAD_DIM] * gate * inv).astype(o_ref.dtype)

    blocks = range(nqb)
    parts = [key_parts(t) for t in blocks]
    ss, ms = {}, {}
    q_next = prep(0)
    for t in blocks:
        q = q_next
        if t + 1 < nqb:
            q_next = prep(t + 1)
        ss[t] = logits(q, parts[t])
        ms[t] = None if shift_free else row_max(ss[t])
        if t >= 1:
            finish(t - 1, weighted(ss.pop(t - 1), ms[t - 1], parts[t - 1]), ms.pop(t - 1))
    finish(nqb - 1, weighted(ss[nqb - 1], ms[nqb - 1], parts[nqb - 1]), ms[nqb - 1])


def _window_bias(tq):
    r = (jnp.arange(GROUP * tq) % tq)[:, None]
    c = jnp.arange(3 * WINDOW)[None, :]
    band = (c >= r) & (c <= r + 2 * WINDOW)
    keep = jnp.stack([band & (c >= WINDOW), band, band & (c < 2 * WINDOW)])
    return jnp.where(keep, 0.0, NEG).astype(F32)


def _attn(u3, q_off, k_off, v_off, g_off, kv_heads, qnorm, knorm, *, tq, nqb, ctx=None, rope=None,
          window=False, sink=None, emit_kv=False):
    bsz, seq, _ = u3.shape
    gw = GROUP * HEAD_DIM
    heads = kv_heads * GROUP
    ctx_len = 0 if ctx is None else ctx[0].shape[1]
    pad = WINDOW if window else 0
    assert not window or (tq == WINDOW and seq // tq >= 2)
    assert seq % (tq * nqb) == 0

    def wide(off):
        return pl.BlockSpec((None, tq * nqb, gw), lambda b, h, i: (b, i, off // gw + h))

    def narrow(off, rows):
        return pl.BlockSpec((None, rows, HEAD_DIM), lambda b, h, i: (b, 0, off // HEAD_DIM + h))

    def whole(shape):
        return pl.BlockSpec(shape, lambda b, h, i: (0,) * len(shape))

    in_specs = [wide(q_off), wide(g_off), narrow(k_off, seq), narrow(v_off, seq)]
    args = [u3, u3, u3, u3]
    if ctx is not None:
        for cache in ctx:
            in_specs.append(narrow(0, ctx_len))
            args.append(cache.reshape(bsz, ctx_len, kv_heads * HEAD_DIM))
    if rope is not None:
        for t in rope:
            in_specs.append(whole(t.shape))
            args.append(t)
    if window:
        in_specs.append(whole((3, GROUP * tq, 3 * WINDOW)))
        args.append(_window_bias(tq))
    in_specs += [whole((2, HEAD_DIM)), whole((2, HEAD_DIM))]
    args += [jnp.stack([qnorm, qnorm[_ROT_PARTNER]]), jnp.stack([knorm, knorm[_ROT_PARTNER]])]
    if sink is not None:
        in_specs.append(pl.BlockSpec(memory_space=pltpu.SMEM))
        args.append(sink)
    out_specs = [pl.BlockSpec((None, tq * nqb, gw), lambda b, h, i: (b, i, h))]
    out_shape = [jax.ShapeDtypeStruct((bsz, seq, heads * HEAD_DIM), BF16)]
    if emit_kv:
        for _ in range(2):
            out_specs.append(narrow(0, seq))
            out_shape.append(jax.ShapeDtypeStruct((bsz, seq, kv_heads * HEAD_DIM), F32))
    rows = seq + 2 * pad + ctx_len

    def call(shift_free, *operands):
        kern = functools.partial(_attn_kernel, seq=seq, tq=tq, nqb=nqb, ctx_len=ctx_len,
                                 rope=rope is not None, window=window, sink=sink is not None,
                                 emit_kv=emit_kv, shift_free=shift_free)
        return pl.pallas_call(
            kern,
            grid=(bsz, kv_heads, seq // (tq * nqb)),
            in_specs=in_specs,
            out_specs=out_specs,
            out_shape=out_shape,
            scratch_shapes=[pltpu.VMEM((rows, HEAD_DIM), BF16),
                            pltpu.VMEM((rows, 2 * HEAD_DIM), BF16)]
            + ([pltpu.VMEM((seq, HEAD_DIM), F32)] * 2 if rope is not None else []),
            compiler_params=_params("parallel", "parallel", "arbitrary"),
            name="attn",
        )(*operands)

    k_norm = math.sqrt(HEAD_DIM) * jnp.max(jnp.abs(knorm))
    if ctx is not None:
        k_norm = jnp.maximum(k_norm, jnp.sqrt(jnp.max(jnp.sum(jnp.square(ctx[0]), axis=-1))))
    bound = jnp.max(jnp.abs(qnorm)) * LOG2E * k_norm * LOGIT_BOUND_MARGIN
    return lax.cond(bound <= MAX_UNSHIFTED_LOGIT,
                    functools.partial(call, True), functools.partial(call, False), *args)


def _rope_tables(n_tokens):
    rows = n_tokens // GRID_W
    row = jnp.repeat(jnp.arange(rows), GRID_W).astype(F32)
    col = (jnp.arange(rows * GRID_W) % GRID_W).astype(F32)
    inv = ROPE_THETA ** (-jnp.arange(ROPE_QUARTER, dtype=F32) / ROPE_QUARTER)
    ar = row[:, None] * inv
    ac = col[:, None] * inv
    ang = jnp.concatenate([ar, ar, ac, ac], axis=-1)
    rot = np.zeros((2 * HEAD_DIM, 2 * HEAD_DIM), np.float32)
    rot[_ROT_PARTNER, np.arange(HEAD_DIM)] = np.where(_ROT_FIRST, -1.0, 1.0)
    rot[HEAD_DIM:, HEAD_DIM:] = 1.0 / HEAD_DIM
    return jnp.cos(ang), jnp.sin(ang), jnp.asarray(rot, BF16)


def _tile(m, pref):
    t = pref
    while m % t:
        t //= 2
    return t


def kernel(x_prompt, x_sample, state_l0_hgrn, cache_l0_k, cache_l0_v, cache_l1_k, cache_l1_v, c, c_ctx, lb_gamma, l0_norm, l0_w_mod, l0_b_mod, l0_w_in, l0_w_out, l0_a_onorm, l0_b_qnorm, l0_b_knorm, l1_norm, l1_w_mod, l1_b_mod, l1_w_in, l1_w_out, l1_c_qnorm, l1_c_knorm, l1_c_sink):
    pb, pl_, d = x_prompt.shape
    sb, sl, _ = x_sample.shape
    aw = A_HEADS * HEAD_DIM
    bkv = B_KV_HEADS * HEAD_DIM
    bw = B_KV_HEADS * GROUP * HEAD_DIM
    ckv = C_KV_HEADS * HEAD_DIM
    cw = C_KV_HEADS * GROUP * HEAD_DIM

    lb = jnp.cumsum(jax.nn.softmax(lb_gamma.astype(F32), axis=0), axis=0)[0]
    rope = _rope_tables(sl)

    nrow = -(-(sb + 1) // 8) * 8
    cond = jnp.zeros((nrow, d), F32).at[:sb].set(c).at[sb].set(c_ctx)
    xs = (x_prompt.reshape(pb * pl_, d), x_sample.reshape(sb * sl, d))
    tms = (_tile(pb * pl_, INPROJ_ROWS), _tile(sl, INPROJ_ROWS))
    rows_per_mod = (pb * pl_, sl)

    def mods(w_mod, b_mod):
        m = _adaln(cond, w_mod, b_mod)
        parts = [m[:, i * d:(i + 1) * d] for i in range(3)]
        return ([p[sb:sb + 1, None, :] for p in parts], [p[:sb, None, :] for p in parts])

    def tn_for(n):
        for t in INPROJ_COLS:
            if n % t == 0:
                return t
        return n

    mod_p, mod_s = mods(l0_w_mod, l0_b_mod)
    w_in = l0_w_in.astype(BF16)
    w_out = l0_w_out.astype(BF16)
    tn = tn_for(w_in.shape[1])
    u_p = _inproj(xs[0], l0_norm, mod_p[0], mod_p[1], w_in, rows_per_mod[0], tms[0], tn).reshape(pb, pl_, -1)
    u_s = _inproj(xs[1], l0_norm, mod_s[0], mod_s[1], w_in, rows_per_mod[1], tms[1], tn).reshape(sb, sl, -1)

    oa_p, new_state = _hgrn(u_p, lb, l0_a_onorm, None, hb=A_HEADS, emit_state=True)
    (oa_s,) = _hgrn(u_s, lb, l0_a_onorm, state_l0_hgrn, hb=HGRN_HEADS_SAMPLE, emit_state=False)

    q_off = 5 * aw
    k_off = q_off + bw
    v_off = k_off + bkv
    g_off = v_off + bkv
    ob_p, k0, v0 = _attn(u_p, q_off, k_off, v_off, g_off, B_KV_HEADS, l0_b_qnorm, l0_b_knorm,
                         tq=Q_ROWS, nqb=Q_BLOCKS_PROMPT, emit_kv=True)
    (ob_s,) = _attn(u_s, q_off, k_off, v_off, g_off, B_KV_HEADS, l0_b_qnorm, l0_b_knorm,
                    tq=Q_ROWS, nqb=_tile(sl // Q_ROWS, Q_BLOCKS_DENSE), ctx=(cache_l0_k, cache_l0_v), rope=rope)

    otm = (_tile(pb * pl_, OUTPROJ_ROWS), _tile(sl, OUTPROJ_ROWS))
    y_p = _outproj([oa_p.reshape(pb * pl_, aw), ob_p.reshape(pb * pl_, bw)], w_out, xs[0], mod_p[2],
                   rows_per_mod[0], otm[0])
    y_s = _outproj([oa_s.reshape(sb * sl, aw), ob_s.reshape(sb * sl, bw)], w_out, xs[1], mod_s[2],
                   rows_per_mod[1], otm[1])

    mod_p, mod_s = mods(l1_w_mod, l1_b_mod)
    w_in = l1_w_in.astype(BF16)
    w_out = l1_w_out.astype(BF16)
    tn = tn_for(w_in.shape[1])
    u_p = _inproj(y_p, l1_norm, mod_p[0], mod_p[1], w_in, rows_per_mod[0], tms[0], tn).reshape(pb, pl_, -1)
    u_s = _inproj(y_s, l1_norm, mod_s[0], mod_s[1], w_in, rows_per_mod[1], tms[1], tn).reshape(sb, sl, -1)

    k_off = cw
    v_off = k_off + ckv
    g_off = v_off + ckv
    oc_p, k1, v1 = _attn(u_p, 0, k_off, v_off, g_off, C_KV_HEADS, l1_c_qnorm, l1_c_knorm,
                         tq=Q_ROWS, nqb=Q_BLOCKS_PROMPT, sink=l1_c_sink, emit_kv=True)
    (oc_s,) = _attn(u_s, 0, k_off, v_off, g_off, C_KV_HEADS, l1_c_qnorm, l1_c_knorm,
                    tq=WINDOW, nqb=_tile(sl // WINDOW, Q_BLOCKS_WINDOW), ctx=(cache_l1_k, cache_l1_v), rope=rope,
                    window=True, sink=l1_c_sink)

    z_p = _outproj([oc_p.reshape(pb * pl_, cw)], w_out, y_p, mod_p[2], rows_per_mod[0], otm[0])
    z_s = _outproj([oc_s.reshape(sb * sl, cw)], w_out, y_s, mod_s[2], rows_per_mod[1], otm[1])

    return (z_p.reshape(pb, pl_, d), z_s.reshape(sb, sl, d), new_state,
            k0.reshape(pb, pl_, B_KV_HEADS, HEAD_DIM), v0.reshape(pb, pl_, B_KV_HEADS, HEAD_DIM),
            k1.reshape(pb, pl_, C_KV_HEADS, HEAD_DIM), v1.reshape(pb, pl_, C_KV_HEADS, HEAD_DIM))
```
